```python
import math
import jax, jax.numpy as jnp
from jax import lax
import numpy as np

D_MODEL = 1024
BATCH = 16
SEQ = 2048
DEPTH = 1

D_FF = 2816
FOX_HEADS = 8
FOX_HEAD_DIM = 64
FOX_WIDTH = FOX_HEADS * FOX_HEAD_DIM
Q_BLOCK = 128
GLA_HEADS = 4
GLA_DK = 64
GLA_DV = 128
GLA_KEY_WIDTH = GLA_HEADS * GLA_DK
GLA_VAL_WIDTH = GLA_HEADS * GLA_DV
GLA_GATE_RANK = 16
GLA_TAU = 16.0
GLA_CHUNK = 64
NORM_EPS = 1e-6

IN_SPLITS = [FOX_WIDTH, FOX_WIDTH, FOX_WIDTH, FOX_HEADS,
             GLA_KEY_WIDTH, GLA_KEY_WIDTH, GLA_VAL_WIDTH, GLA_GATE_RANK, GLA_VAL_WIDTH,
             D_MODEL, D_MODEL]
D_IN = sum(IN_SPLITS)
IN_OFFSETS = [sum(IN_SPLITS[:i + 1]) for i in range(len(IN_SPLITS) - 1)]

kernel_name = "hybrid_fox_gla_gated_macaron"


def rmsnorm(x, g):
    xf = x.astype(jnp.float32)
    y = xf * lax.rsqrt(jnp.mean(xf * xf, axis=-1, keepdims=True) + NORM_EPS)
    return (y * g.astype(jnp.float32)).astype(x.dtype)


def swiglu(x, w_gate, w_up, w_down):
    return (jax.nn.silu(x @ w_gate) * (x @ w_up)) @ w_down


def forgetting_attention(q, k, v, log_f):
    B, S, H, Dh = q.shape
    c = jnp.cumsum(log_f, axis=1).transpose(0, 2, 1)
    scale = Dh ** -0.5
    qf, kf = q.astype(jnp.float32), k.astype(jnp.float32)
    outs = []
    for i in range(S // Q_BLOCK):
        q0 = i * Q_BLOCK
        kend = q0 + Q_BLOCK
        s = jnp.einsum('bqhd,bkhd->bhqk', qf[:, q0:kend], kf[:, :kend]) * scale
        bias = c[:, :, q0:kend, None] - c[:, :, None, :kend]
        mask = (q0 + jnp.arange(Q_BLOCK))[:, None] >= jnp.arange(kend)[None, :]
        s = jnp.where(mask, s + bias, -jnp.inf)
        p = jax.nn.softmax(s, axis=-1)
        outs.append(jnp.einsum('bhqk,bkhd->bqhd', p.astype(v.dtype), v[:, :kend]))
    return jnp.concatenate(outs, axis=1)


def gla_chunked(q, k, v, log_a):
    B, S, H, Dk = q.shape
    Dv = v.shape[-1]
    N, C = S // GLA_CHUNK, GLA_CHUNK

    def to_chunks(t):
        return t.astype(jnp.float32).reshape(B, N, C, H, t.shape[-1]).transpose(0, 3, 1, 2, 4)

    qc = to_chunks(q) * (Dk ** -0.5)
    kc, vc, gc = to_chunks(k), to_chunks(v), to_chunks(log_a)
    b = jnp.cumsum(gc, axis=3)
    b_last = b[:, :, :, -1:, :]
    q_in = qc * jnp.exp(b)
    k_in = kc * jnp.exp(-b)
    k_st = kc * jnp.exp(b_last - b)
    tril = jnp.tril(jnp.ones((C, C), dtype=bool))
    attn = jnp.where(tril, jnp.einsum('bhncd,bhnsd->bhncs', q_in, k_in), 0.0)
    o_intra = jnp.einsum('bhncs,bhnsv->bhncv', attn, vc)
    kv = jnp.einsum('bhnsd,bhnsv->bhndv', k_st, vc)
    decay = jnp.exp(b_last[:, :, :, 0, :])

    def step(state, inp):
        d, kvn = inp
        return d[..., None] * state + kvn, state

    s0 = jnp.zeros((B, H, Dk, Dv), jnp.float32)
    _, s_prev = lax.scan(step, s0, (jnp.moveaxis(decay, 2, 0), jnp.moveaxis(kv, 2, 0)))
    s_prev = jnp.moveaxis(s_prev, 0, 2)
    o = o_intra + jnp.einsum('bhncd,bhndv->bhncv', q_in, s_prev)
    return o.transpose(0, 2, 3, 1, 4).reshape(B, S, H, Dv).astype(v.dtype)


def hybrid_mixer(u, w_in, b_forget, w_alpha_up, b_alpha, gla_norm_g,
                 w_branch_fox, w_branch_gla, w_out):
    B, S, _ = u.shape
    z = u @ w_in
    (fq, fk, fv, f_logit, gq, gk, gv, a_lr, g_r, gate_fox, gate_gla) = jnp.split(z, IN_OFFSETS, axis=-1)
    log_f = jax.nn.log_sigmoid((f_logit + b_forget).astype(jnp.float32))
    o_fox = forgetting_attention(fq.reshape(B, S, FOX_HEADS, FOX_HEAD_DIM),
                                 fk.reshape(B, S, FOX_HEADS, FOX_HEAD_DIM),
                                 fv.reshape(B, S, FOX_HEADS, FOX_HEAD_DIM), log_f)
    o_fox = o_fox.reshape(B, S, FOX_WIDTH)
    log_a = jax.nn.log_sigmoid((a_lr @ w_alpha_up + b_alpha).astype(jnp.float32)) / GLA_TAU
    o_gla = gla_chunked(gq.reshape(B, S, GLA_HEADS, GLA_DK),
                        gk.reshape(B, S, GLA_HEADS, GLA_DK),
                        gv.reshape(B, S, GLA_HEADS, GLA_DV),
                        log_a.reshape(B, S, GLA_HEADS, GLA_DK))
    o_gla = rmsnorm(o_gla, gla_norm_g.reshape(GLA_HEADS, GLA_DV)).reshape(B, S, GLA_VAL_WIDTH)
    o_gla = o_gla * jax.nn.silu(g_r)
    y = jax.nn.sigmoid(gate_fox) * (o_fox @ w_branch_fox) + jax.nn.sigmoid(gate_gla) * (o_gla @ w_branch_gla)
    return y @ w_out


def setup_inputs(seed: int = 0) -> dict:
    key = jax.random.key(seed)
    ks = jax.random.split(key, 24)
    L = DEPTH

    def w(k, shape, fan_in):
        return jax.random.normal(k, shape, jnp.float32) * fan_in ** -0.5

    def gain(k, n):
        return 1.0 + 0.05 * jax.random.normal(k, (L, n), jnp.float32)

    return {
        "x": jax.random.normal(ks[0], (BATCH, SEQ, D_MODEL), jnp.float32),
        "ffn1_pre_g": gain(ks[1], D_MODEL),
        "ffn1_w_gate": w(ks[2], (L, D_MODEL, D_FF), D_MODEL),
        "ffn1_w_up": w(ks[3], (L, D_MODEL, D_FF), D_MODEL),
        "ffn1_w_down": w(ks[4], (L, D_FF, D_MODEL), D_FF),
        "ffn1_post_g": gain(ks[5], D_MODEL),
        "mix_pre_g": gain(ks[6], D_MODEL),
        "w_in": w(ks[7], (L, D_MODEL, D_IN), D_MODEL),
        "b_forget": 0.1 * jax.random.normal(ks[8], (L, FOX_HEADS), jnp.float32),
        "w_alpha_up": w(ks[9], (L, GLA_GATE_RANK, GLA_KEY_WIDTH), GLA_GATE_RANK),
        "b_alpha": 0.1 * jax.random.normal(ks[10], (L, GLA_KEY_WIDTH), jnp.float32),
        "gla_norm_g": gain(ks[11], GLA_VAL_WIDTH),
        "w_branch_fox": w(ks[12], (L, FOX_WIDTH, D_MODEL), FOX_WIDTH),
        "w_branch_gla": w(ks[13], (L, GLA_VAL_WIDTH, D_MODEL), GLA_VAL_WIDTH),
        "w_out": w(ks[14], (L, D_MODEL, D_MODEL), D_MODEL),
        "mix_post_g": gain(ks[15], D_MODEL),
        "ffn2_pre_g": gain(ks[16], D_MODEL),
        "ffn2_w_gate": w(ks[17], (L, D_MODEL, D_FF), D_MODEL),
        "ffn2_w_up": w(ks[18], (L, D_MODEL, D_FF), D_MODEL),
        "ffn2_w_down": w(ks[19], (L, D_FF, D_MODEL), D_FF),
        "ffn2_post_g": gain(ks[20], D_MODEL),
    }


def reference(x, ffn1_pre_g, ffn1_w_gate, ffn1_w_up, ffn1_w_down, ffn1_post_g,
              mix_pre_g, w_in, b_forget, w_alpha_up, b_alpha, gla_norm_g,
              w_branch_fox, w_branch_gla, w_out, mix_post_g,
              ffn2_pre_g, ffn2_w_gate, ffn2_w_up, ffn2_w_down, ffn2_post_g):
    h = x
    for l in range(DEPTH):
        f1 = swiglu(rmsnorm(h, ffn1_pre_g[l]), ffn1_w_gate[l], ffn1_w_up[l], ffn1_w_down[l])
        h = h + 0.5 * rmsnorm(f1, ffn1_post_g[l])
        m = hybrid_mixer(rmsnorm(h, mix_pre_g[l]), w_in[l], b_forget[l], w_alpha_up[l],
                         b_alpha[l], gla_norm_g[l], w_branch_fox[l], w_branch_gla[l], w_out[l])
        h = h + rmsnorm(m, mix_post_g[l])
        f2 = swiglu(rmsnorm(h, ffn2_pre_g[l]), ffn2_w_gate[l], ffn2_w_up[l], ffn2_w_down[l])
        h = h + 0.5 * rmsnorm(f2, ffn2_post_g[l])
    return h
```

```python
import functools

import jax
import jax.numpy as jnp
from jax import lax
from jax.experimental import pallas as pl
from jax.experimental.pallas import tpu as pltpu

F32 = jnp.float32
BF16 = jnp.bfloat16

D_MODEL = 1024
D_FF = 2816
FOX_HEADS = 8
FOX_HEAD_DIM = 64
FOX_WIDTH = FOX_HEADS * FOX_HEAD_DIM
GLA_HEADS = 4
GLA_DK = 64
GLA_DV = 128
GLA_KEY_WIDTH = GLA_HEADS * GLA_DK
GLA_VAL_WIDTH = GLA_HEADS * GLA_DV
GLA_GATE_RANK = 16
GLA_TAU = 16.0
GLA_CHUNK = 64
NORM_EPS = 1e-6

LANES = 128
HEAD_PAIR = LANES
VMEM_LIMIT_BYTES = 56 * 1024 * 1024

TM = 512
FF_CHUNK = 256
TQ = 256
TK = 256
TT = 256
SMALL_W = LANES
CUM_ROWS = 16


def _rms(x, g):
    return x * lax.rsqrt(jnp.mean(x * x, axis=-1, keepdims=True) + NORM_EPS) * g


def _log_sigmoid(x):
    return jnp.minimum(x, 0.0) - jnp.log1p(jnp.exp(-jnp.abs(x)))


def _dot(a, b):
    return jnp.dot(a, b, preferred_element_type=F32)


def _dot_nt(a, b):
    return lax.dot_general(a, b, (((1,), (1,)), ((), ())), preferred_element_type=F32)


def _dot_tn(a, b):
    return lax.dot_general(a, b, (((0,), (0,)), ((), ())), preferred_element_type=F32)


def _split2(x):
    hi = x.astype(BF16)
    lo = (x - hi.astype(F32)).astype(BF16)
    return hi, lo


def _split3(x):
    hi = x.astype(BF16)
    r = x - hi.astype(F32)
    mid = r.astype(BF16)
    lo = (r - mid.astype(F32)).astype(BF16)
    return hi, mid, lo


def _const_spec(shape):
    return pl.BlockSpec(shape, lambda *_: (0,) * len(shape), pipeline_mode=pl.Buffered(1))


def _params(n_axes):
    return pltpu.CompilerParams(dimension_semantics=("arbitrary",) * n_axes,
                                vmem_limit_bytes=VMEM_LIMIT_BYTES)


def _ffn_kernel(x_ref, gpre_ref, wg_ref, wu_ref, wd_ref, gpost_ref, o_ref, xn_ref, acc_ref):
    x = x_ref[...]
    xn_ref[...] = _rms(x, gpre_ref[...]).astype(BF16)
    acc_ref[...] = jnp.zeros_like(acc_ref)

    def chunk(c, carry):
        off = pl.multiple_of(c * FF_CHUNK, FF_CHUNK)
        xn = xn_ref[...]
        g = _dot(xn, wg_ref[:, pl.ds(off, FF_CHUNK)])
        u = _dot(xn, wu_ref[:, pl.ds(off, FF_CHUNK)])
        h = (g * jax.nn.sigmoid(g) * u).astype(BF16)
        acc_ref[...] += _dot(h, wd_ref[pl.ds(off, FF_CHUNK), :])
        return carry

    lax.fori_loop(0, D_FF // FF_CHUNK, chunk, 0)
    o_ref[...] = x + 0.5 * _rms(acc_ref[...], gpost_ref[...])


def _ffn(x2d, g_pre, w_gate, w_up, w_down, g_post):
    t = x2d.shape[0]
    row = pl.BlockSpec((TM, D_MODEL), lambda i: (i, 0))
    return pl.pallas_call(
        _ffn_kernel,
        grid=(t // TM,),
        in_specs=[row, _const_spec((1, D_MODEL)), _const_spec((D_MODEL, D_FF)),
                  _const_spec((D_MODEL, D_FF)), _const_spec((D_FF, D_MODEL)),
                  _const_spec((1, D_MODEL))],
        out_specs=row,
        out_shape=jax.ShapeDtypeStruct((t, D_MODEL), F32),
        scratch_shapes=[pltpu.VMEM((TM, D_MODEL), BF16), pltpu.VMEM((TM, D_MODEL), F32)],
        compiler_params=_params(1),
        name="ffn",
    )(x2d, g_pre, w_gate, w_up, w_down, g_post)


def _inproj_kernel(tiles_per_seq, h_ref, g_ref, wfox_ref, wgla_ref, wsm_ref, bsm_ref, wa_ref,
                   ba_ref, upper_ref, fq_ref, fk_ref, fv_ref, c_ref, gq_ref, gk_ref, gv_ref,
                   la_ref, carry_ref):
    @pl.when(pl.program_id(0) % tiles_per_seq == 0)
    def _():
        carry_ref[...] = jnp.zeros_like(carry_ref)

    u = _rms(h_ref[...], g_ref[...]).astype(BF16)
    scale = FOX_HEAD_DIM ** -0.5

    zf = _dot(u, wfox_ref[...])
    fq_ref[...] = (zf[:, :FOX_WIDTH] * scale).astype(BF16)
    fk_ref[...] = zf[:, FOX_WIDTH:2 * FOX_WIDTH].astype(BF16)
    fv_ref[...] = zf[:, 2 * FOX_WIDTH:].astype(BF16)

    zg = _dot(u, wgla_ref[...])
    gq_ref[...] = zg[:, :GLA_KEY_WIDTH] * (GLA_DK ** -0.5)
    gk_ref[...] = zg[:, GLA_KEY_WIDTH:2 * GLA_KEY_WIDTH]
    gv_ref[...] = zg[:, 2 * GLA_KEY_WIDTH:].astype(BF16)

    zs = _dot(u, wsm_ref[...])
    a = _dot(zs.astype(BF16), wa_ref[...]) + ba_ref[...]
    la_ref[...] = _log_sigmoid(a) * (1.0 / GLA_TAU)

    lf_t = _log_sigmoid(zs + bsm_ref[...]).T[:CUM_ROWS]
    upper = upper_ref[...]
    hi, mid, lo = _split3(lf_t)
    cs = _dot(hi, upper) + _dot(mid, upper) + _dot(lo, upper)
    carry = carry_ref[...]
    c_tile = cs + jnp.concatenate([carry] * (TM // LANES), axis=1)
    c_ref[0] = c_tile[:FOX_HEADS]
    carry_ref[...] = carry + jnp.sum(lf_t, axis=1, keepdims=True)


def _inproj(h2d, seq, g, wfox, wgla, wsm, bsm, wa, ba, upper):
    t = h2d.shape[0]
    batch = t // seq
    tiles_per_seq = seq // TM
    row = lambda w: pl.BlockSpec((TM, w), lambda i: (i, 0))
    out_shapes = (
        jax.ShapeDtypeStruct((t, FOX_WIDTH), BF16),
        jax.ShapeDtypeStruct((t, FOX_WIDTH), BF16),
        jax.ShapeDtypeStruct((t, FOX_WIDTH), BF16),
        jax.ShapeDtypeStruct((batch, FOX_HEADS, seq), F32),
        jax.ShapeDtypeStruct((t, GLA_KEY_WIDTH), F32),
        jax.ShapeDtypeStruct((t, GLA_KEY_WIDTH), F32),
        jax.ShapeDtypeStruct((t, GLA_VAL_WIDTH), BF16),
        jax.ShapeDtypeStruct((t, GLA_KEY_WIDTH), F32),
    )
    out_specs = (
        row(FOX_WIDTH), row(FOX_WIDTH), row(FOX_WIDTH),
        pl.BlockSpec((1, FOX_HEADS, TM), lambda i: (i // tiles_per_seq, 0, i % tiles_per_seq)),
        row(GLA_KEY_WIDTH), row(GLA_KEY_WIDTH), row(GLA_VAL_WIDTH), row(GLA_KEY_WIDTH),
    )
    return pl.pallas_call(
        functools.partial(_inproj_kernel, tiles_per_seq),
        grid=(t // TM,),
        in_specs=[row(D_MODEL), _const_spec((1, D_MODEL)),
                  _const_spec((D_MODEL, 3 * FOX_WIDTH)),
                  _const_spec((D_MODEL, 2 * GLA_KEY_WIDTH + GLA_VAL_WIDTH)),
                  _const_spec((D_MODEL, SMALL_W)), _const_spec((1, SMALL_W)),
                  _const_spec((SMALL_W, GLA_KEY_WIDTH)), _const_spec((1, GLA_KEY_WIDTH)),
                  _const_spec((TM, TM))],
        out_specs=out_specs,
        out_shape=out_shapes,
        scratch_shapes=[pltpu.VMEM((CUM_ROWS, LANES), F32)],
        compiler_params=_params(1),
        name="inproj",
    )(h2d, g, wfox, wgla, wsm, bsm, wa, ba, upper)


def _fox_kernel(seq, q_ref, k_ref, v_ref, c_ref, o_ref, qm_ref, m_ref, l_ref, acc_ref):
    lane = lax.broadcasted_iota(jnp.int32, (1, HEAD_PAIR), 1)
    head_mask = (lane < FOX_HEAD_DIM, lane >= FOX_HEAD_DIM)
    causal = (lax.broadcasted_iota(jnp.int32, (TQ, TK), 0)
              >= lax.broadcasted_iota(jnp.int32, (TQ, TK), 1))

    def kv_step(kj, diagonal):
        k0 = pl.multiple_of(kj * TK, TK)
        k = k_ref[0, pl.ds(k0, TK), :]
        v = v_ref[0, pl.ds(k0, TK), :]
        alphas = []
        pv = None
        for j in range(2):
            s = _dot_nt(qm_ref[j], k) - c_ref[0, 0, j:j + 1, pl.ds(k0, TK)]
            if diagonal:
                s = jnp.where(causal, s, -jnp.inf)
            m_prev = m_ref[j]
            m_next = jnp.maximum(m_prev, jnp.max(s, axis=1, keepdims=True))
            p = jnp.exp(s - jnp.concatenate([m_next] * (TK // LANES), axis=1))
            alpha = jnp.exp(m_prev - m_next)
            p_lanes = p[:, :LANES]
            for blk in range(1, TK // LANES):
                p_lanes = p_lanes + p[:, blk * LANES:(blk + 1) * LANES]
            l_ref[j] = alpha * l_ref[j] + p_lanes
            m_ref[j] = m_next
            alphas.append(alpha)
            vj = jnp.where(head_mask[j], v, jnp.zeros_like(v))
            contrib = _dot(p.astype(BF16), vj)
            pv = contrib if pv is None else pv + contrib
        acc_ref[...] = acc_ref[...] * jnp.where(head_mask[0], alphas[0], alphas[1]) + pv

    def q_tile(qi, carry):
        q0 = pl.multiple_of(qi * TQ, TQ)
        q = q_ref[0, pl.ds(q0, TQ), :]
        for j in range(2):
            qm_ref[j] = jnp.where(head_mask[j], q, jnp.zeros_like(q))
            m_ref[j] = jnp.full((TQ, LANES), -jnp.inf, F32)
            l_ref[j] = jnp.zeros((TQ, LANES), F32)
        acc_ref[...] = jnp.zeros_like(acc_ref)

        def full_step(kj, c):
            kv_step(kj, False)
            return c

        lax.fori_loop(0, qi, full_step, 0)
        kv_step(qi, True)

        inv = [1.0 / jnp.sum(l_ref[j], axis=1, keepdims=True) for j in range(2)]
        o_ref[0, pl.ds(q0, TQ), :] = (
            acc_ref[...] * jnp.where(head_mask[0], inv[0], inv[1])).astype(BF16)
        return carry

    lax.fori_loop(0, seq // TQ, q_tile, 0)


def _fox(fq, fk, fv, c):
    batch, seq, _ = fq.shape
    n_pairs = FOX_WIDTH // HEAD_PAIR
    qkv = pl.BlockSpec((1, seq, HEAD_PAIR), lambda b, p: (b, 0, p))
    return pl.pallas_call(
        functools.partial(_fox_kernel, seq),
        grid=(batch, n_pairs),
        in_specs=[qkv, qkv, qkv, pl.BlockSpec((1, 1, 2, seq), lambda b, p: (b, p, 0, 0))],
        out_specs=qkv,
        out_shape=jax.ShapeDtypeStruct((batch, seq, FOX_WIDTH), BF16),
        scratch_shapes=[pltpu.VMEM((2, TQ, HEAD_PAIR), BF16),
                        pltpu.VMEM((2, TQ, LANES), F32),
                        pltpu.VMEM((2, TQ, LANES), F32),
                        pltpu.VMEM((TQ, HEAD_PAIR), F32)],
        compiler_params=_params(2),
        name="fox",
    )(fq, fk, fv, c)


def _gla_kernel(q_ref, k_ref, v_ref, la_ref, ltri_ref, lone_ref, o_ref, st_ref):
    @pl.when(pl.program_id(1) == 0)
    def _():
        st_ref[...] = jnp.zeros_like(st_ref)

    n_chunks = TT // GLA_CHUNK
    lane = lax.broadcasted_iota(jnp.int32, (1, HEAD_PAIR), 1)
    head_mask = (lane < GLA_DK, lane >= GLA_DK)
    row_chunk = lax.broadcasted_iota(jnp.int32, (TT, 1), 0) // GLA_CHUNK
    rr = lax.broadcasted_iota(jnp.int32, (TT, TT), 0)
    cc = lax.broadcasted_iota(jnp.int32, (TT, TT), 1)
    intra = (rr >= cc) & (rr // GLA_CHUNK == cc // GLA_CHUNK)
    ltri = ltri_ref[...]
    lone = lone_ref[...]

    for p in range(GLA_HEADS // 2):
        ksl = slice(p * HEAD_PAIR, (p + 1) * HEAD_PAIR)
        hi, lo = _split2(la_ref[:, ksl])
        b = _dot(ltri, hi) + _dot(ltri, lo)
        b_last = _dot(lone, hi) + _dot(lone, lo)
        q_in = (q_ref[:, ksl] * jnp.exp(b)).astype(BF16)
        k = k_ref[:, ksl]
        k_in = (k * jnp.exp(-b)).astype(BF16)
        k_st = (k * jnp.exp(b_last - b)).astype(BF16)

        kv = None
        for j in range(2):
            h = 2 * p + j
            vj = v_ref[:, h * GLA_DV:(h + 1) * GLA_DV]
            kcat = jnp.concatenate(
                [jnp.where((row_chunk == i) & head_mask[j], k_st, jnp.zeros_like(k_st))
                 for i in range(n_chunks)], axis=1)
            contrib = _dot_tn(vj, kcat)
            kv = contrib if kv is None else kv + contrib

        state = st_ref[p]
        states = []
        for i in range(n_chunks):
            states.append(state.astype(BF16))
            decay = jnp.exp(b_last[i * GLA_CHUNK:i * GLA_CHUNK + 1, :])
            state = state * decay + kv[:, i * HEAD_PAIR:(i + 1) * HEAD_PAIR]
        st_ref[p] = state
        st_cat = jnp.concatenate(states, axis=1)

        for j in range(2):
            h = 2 * p + j
            vj = v_ref[:, h * GLA_DV:(h + 1) * GLA_DV]
            qh = jnp.where(head_mask[j], q_in, jnp.zeros_like(q_in))
            attn = jnp.where(intra, _dot_nt(qh, k_in), 0.0).astype(BF16)
            qcat = jnp.concatenate(
                [jnp.where(row_chunk == i, qh, jnp.zeros_like(qh)) for i in range(n_chunks)],
                axis=1)
            o_ref[:, h * GLA_DV:(h + 1) * GLA_DV] = _dot(attn, vj) + _dot_nt(qcat, st_cat)


def _gla(gq, gk, gv, la, seq, ltri, lone):
    t = gq.shape[0]
    batch = t // seq
    tiles = seq // TT
    row = lambda w: pl.BlockSpec((TT, w), lambda b, i: (b * tiles + i, 0))
    return pl.pallas_call(
        _gla_kernel,
        grid=(batch, tiles),
        in_specs=[row(GLA_KEY_WIDTH), row(GLA_KEY_WIDTH), row(GLA_VAL_WIDTH), row(GLA_KEY_WIDTH),
                  _const_spec((TT, TT)), _const_spec((TT, TT))],
        out_specs=row(GLA_VAL_WIDTH),
        out_shape=jax.ShapeDtypeStruct((t, GLA_VAL_WIDTH), F32),
        scratch_shapes=[pltpu.VMEM((GLA_HEADS // 2, GLA_DV, HEAD_PAIR), F32)],
        compiler_params=_params(2),
        name="gla",
    )(gq, gk, gv, la, ltri, lone)


def _outproj_kernel(h_ref, ofox_ref, ogla_ref, gpre_ref, wgate_ref, gnorm_ref, wbf_ref, wbg_ref,
                    wout_ref, gpost_ref, o_ref):
    h = h_ref[...]
    u = _rms(h, gpre_ref[...]).astype(BF16)
    gates = _dot(u, wgate_ref[...])
    g_r = gates[:, :GLA_VAL_WIDTH]
    gate_fox = gates[:, GLA_VAL_WIDTH:GLA_VAL_WIDTH + D_MODEL]
    gate_gla = gates[:, GLA_VAL_WIDTH + D_MODEL:]

    heads = []
    for hd in range(GLA_HEADS):
        sl = slice(hd * GLA_DV, (hd + 1) * GLA_DV)
        heads.append(_rms(ogla_ref[:, sl], gnorm_ref[:, sl]))
    o_gla = jnp.concatenate(heads, axis=1)
    o_gla = (o_gla * (g_r * jax.nn.sigmoid(g_r))).astype(BF16)

    y = (jax.nn.sigmoid(gate_fox) * _dot(ofox_ref[...], wbf_ref[...])
         + jax.nn.sigmoid(gate_gla) * _dot(o_gla, wbg_ref[...]))
    m = _dot(y.astype(BF16), wout_ref[...])
    o_ref[...] = h + _rms(m, gpost_ref[...])


def _outproj(h2d, o_fox, o_gla, g_pre, wgate, gnorm, wbf, wbg, wout, g_post):
    t = h2d.shape[0]
    row = lambda w: pl.BlockSpec((TM, w), lambda i: (i, 0))
    return pl.pallas_call(
        _outproj_kernel,
        grid=(t // TM,),
        in_specs=[row(D_MODEL), row(FOX_WIDTH), row(GLA_VAL_WIDTH), _const_spec((1, D_MODEL)),
                  _const_spec((D_MODEL, GLA_VAL_WIDTH + 2 * D_MODEL)),
                  _const_spec((1, GLA_VAL_WIDTH)),
                  _const_spec((FOX_WIDTH, D_MODEL)), _const_spec((GLA_VAL_WIDTH, D_MODEL)),
                  _const_spec((D_MODEL, D_MODEL)), _const_spec((1, D_MODEL))],
        out_specs=row(D_MODEL),
        out_shape=jax.ShapeDtypeStruct((t, D_MODEL), F32),
        compiler_params=_params(1),
        name="outproj",
    )(h2d, o_fox, o_gla, g_pre, wgate, gnorm, wbf, wbg, wout, g_post)


def _chunk_matrices(n):
    r = jnp.arange(n)[:, None]
    c = jnp.arange(n)[None, :]
    same = (r // GLA_CHUNK) == (c // GLA_CHUNK)
    return (same & (r >= c)).astype(BF16), same.astype(BF16)


def _layer(h2d, batch, seq, ffn1_pre_g, ffn1_w_gate, ffn1_w_up, ffn1_w_down, ffn1_post_g,
           mix_pre_g, w_in, b_forget, w_alpha_up, b_alpha, gla_norm_g, w_branch_fox,
           w_branch_gla, w_out, mix_post_g, ffn2_pre_g, ffn2_w_gate, ffn2_w_up, ffn2_w_down,
           ffn2_post_g):
    vec = lambda g: g.reshape(1, -1)
    bf = lambda w: w.astype(BF16)

    splits = [FOX_WIDTH, FOX_WIDTH, FOX_WIDTH, FOX_HEADS, GLA_KEY_WIDTH, GLA_KEY_WIDTH,
              GLA_VAL_WIDTH, GLA_GATE_RANK, GLA_VAL_WIDTH, D_MODEL, D_MODEL]
    offs = [0]
    for s in splits:
        offs.append(offs[-1] + s)
    col = lambda a, b: w_in[:, offs[a]:offs[b]]
    wfox = bf(col(0, 3))
    wgla = bf(col(4, 7))
    pad_cols = SMALL_W - FOX_HEADS - GLA_GATE_RANK
    wsm = bf(jnp.concatenate([col(3, 4), col(7, 8), jnp.zeros((D_MODEL, pad_cols), F32)], axis=1))
    bsm = jnp.concatenate([b_forget, jnp.zeros((SMALL_W - FOX_HEADS,), F32)]).reshape(1, SMALL_W)
    wa = bf(jnp.concatenate([jnp.zeros((FOX_HEADS, GLA_KEY_WIDTH), F32), w_alpha_up,
                             jnp.zeros((pad_cols, GLA_KEY_WIDTH), F32)], axis=0))
    wgate = bf(col(8, 11))
    tok = jnp.arange(TM)
    upper = (tok[:, None] <= tok[None, :]).astype(BF16)
    ltri, lone = _chunk_matrices(TT)

    h1 = _ffn(h2d, vec(ffn1_pre_g), bf(ffn1_w_gate), bf(ffn1_w_up), bf(ffn1_w_down),
              vec(ffn1_post_g))

    fq, fk, fv, c, gq, gk, gv, la = _inproj(h1, seq, vec(mix_pre_g), wfox, wgla, wsm, bsm, wa,
                                            vec(b_alpha), upper)
    shape3 = (batch, seq, FOX_WIDTH)
    o_fox = _fox(fq.reshape(shape3), fk.reshape(shape3), fv.reshape(shape3),
                 c.reshape(batch, FOX_HEADS // 2, 2, seq)).reshape(batch * seq, FOX_WIDTH)
    o_gla = _gla(gq, gk, gv, la, seq, ltri, lone)

    h2 = _outproj(h1, o_fox, o_gla, vec(mix_pre_g), wgate, vec(gla_norm_g), bf(w_branch_fox),
                  bf(w_branch_gla), bf(w_out), vec(mix_post_g))

    return _ffn(h2, vec(ffn2_pre_g), bf(ffn2_w_gate), bf(ffn2_w_up), bf(ffn2_w_down),
                vec(ffn2_post_g))


def kernel(x, ffn1_pre_g, ffn1_w_gate, ffn1_w_up, ffn1_w_down, ffn1_post_g, mix_pre_g, w_in,
           b_forget, w_alpha_up, b_alpha, gla_norm_g, w_branch_fox, w_branch_gla, w_out,
           mix_post_g, ffn2_pre_g, ffn2_w_gate, ffn2_w_up, ffn2_w_down, ffn2_post_g):
    batch, seq, d = x.shape
    h = x.reshape(batch * seq, d)
    depth = ffn1_pre_g.shape[0]
    for l in range(depth):
        h = _layer(h, batch, seq, ffn1_pre_g[l], ffn1_w_gate[l], ffn1_w_up[l], ffn1_w_down[l],
                   ffn1_post_g[l], mix_pre_g[l], w_in[l], b_forget[l], w_alpha_up[l], b_alpha[l],
                   gla_norm_g[l], w_branch_fox[l], w_branch_gla[l], w_out[l], mix_post_g[l],
                   ffn2_pre_g[l], ffn2_w_gate[l], ffn2_w_up[l], ffn2_w_down[l], ffn2_post_g[l])
    return h.reshape(batch, seq, d)
```

```python
import functools

import jax
import jax.numpy as jnp
from jax import lax
from jax.experimental import pallas as pl
from jax.experimental.pallas import tpu as pltpu

F32 = jnp.float32
BF16 = jnp.bfloat16

D_MODEL = 1024
D_FF = 2816
FOX_HEADS = 8
FOX_HEAD_DIM = 64
FOX_WIDTH = FOX_HEADS * FOX_HEAD_DIM
GLA_HEADS = 4
GLA_DK = 64
GLA_DV = 128
GLA_KEY_WIDTH = GLA_HEADS * GLA_DK
GLA_VAL_WIDTH = GLA_HEADS * GLA_DV
GLA_GATE_RANK = 16
GLA_TAU = 16.0
GLA_CHUNK = 64
NORM_EPS = 1e-6

LANES = 128
HEAD_PAIR = LANES
VMEM_LIMIT_BYTES = 56 * 1024 * 1024

TM = 512
FF_CHUNK = 256
TQ = 256
FOX_SUB = 2
TT = 256
SMALL_W = LANES
CUM_ROWS = 16


def _rms(x, g):
    return x * lax.rsqrt(jnp.mean(x * x, axis=-1, keepdims=True) + NORM_EPS) * g


def _log_sigmoid(x):
    return jnp.minimum(x, 0.0) - jnp.log1p(jnp.exp(-jnp.abs(x)))


def _dot(a, b):
    return jnp.dot(a, b, preferred_element_type=F32)


def _dot_nt(a, b):
    return lax.dot_general(a, b, (((1,), (1,)), ((), ())), preferred_element_type=F32)


def _dot_tn(a, b):
    return lax.dot_general(a, b, (((0,), (0,)), ((), ())), preferred_element_type=F32)


def _split2(x):
    hi = x.astype(BF16)
    lo = (x - hi.astype(F32)).astype(BF16)
    return hi, lo


def _split3(x):
    hi = x.astype(BF16)
    r = x - hi.astype(F32)
    mid = r.astype(BF16)
    lo = (r - mid.astype(F32)).astype(BF16)
    return hi, mid, lo


def _const_spec(shape):
    return pl.BlockSpec(shape, lambda *_: (0,) * len(shape), pipeline_mode=pl.Buffered(1))


def _params(n_axes):
    return pltpu.CompilerParams(dimension_semantics=("arbitrary",) * n_axes,
                                vmem_limit_bytes=VMEM_LIMIT_BYTES)


def _ffn_kernel(x_ref, gpre_ref, wg_ref, wu_ref, wd_ref, gpost_ref, o_ref):
    x = x_ref[...]
    xn = _rms(x, gpre_ref[...]).astype(BF16)
    acc = None
    for c in range(D_FF // FF_CHUNK):
        cols = slice(c * FF_CHUNK, (c + 1) * FF_CHUNK)
        g = _dot(xn, wg_ref[:, cols])
        u = _dot(xn, wu_ref[:, cols])
        h = (g * jax.nn.sigmoid(g) * u).astype(BF16)
        part = _dot(h, wd_ref[cols, :])
        acc = part if acc is None else acc + part
    o_ref[...] = x + 0.5 * _rms(acc, gpost_ref[...])


def _ffn(x2d, g_pre, w_gate, w_up, w_down, g_post):
    t = x2d.shape[0]
    row = pl.BlockSpec((TM, D_MODEL), lambda i: (i, 0))
    return pl.pallas_call(
        _ffn_kernel,
        grid=(t // TM,),
        in_specs=[row, _const_spec((1, D_MODEL)), _const_spec((D_MODEL, D_FF)),
                  _const_spec((D_MODEL, D_FF)), _const_spec((D_FF, D_MODEL)),
                  _const_spec((1, D_MODEL))],
        out_specs=row,
        out_shape=jax.ShapeDtypeStruct((t, D_MODEL), F32),
        compiler_params=_params(1),
        name="ffn",
    )(x2d, g_pre, w_gate, w_up, w_down, g_post)


def _inproj_kernel(tiles_per_seq, h_ref, g_ref, wfox_ref, wgla_ref, wsm_ref, bsm_ref, wa_ref,
                   ba_ref, upper_ref, fq_ref, fk_ref, fv_ref, c_ref, gq_ref, gk_ref, gv_ref,
                   la_ref, carry_ref):
    @pl.when(pl.program_id(0) % tiles_per_seq == 0)
    def _():
        carry_ref[...] = jnp.zeros_like(carry_ref)

    u = _rms(h_ref[...], g_ref[...]).astype(BF16)
    scale = FOX_HEAD_DIM ** -0.5

    zf = _dot(u, wfox_ref[...])
    fq_ref[...] = (zf[:, :FOX_WIDTH] * scale).astype(BF16)
    fk_ref[...] = zf[:, FOX_WIDTH:2 * FOX_WIDTH].astype(BF16)
    fv_ref[...] = zf[:, 2 * FOX_WIDTH:].astype(BF16)

    zg = _dot(u, wgla_ref[...])
    gq_ref[...] = zg[:, :GLA_KEY_WIDTH] * (GLA_DK ** -0.5)
    gk_ref[...] = zg[:, GLA_KEY_WIDTH:2 * GLA_KEY_WIDTH]
    gv_ref[...] = zg[:, 2 * GLA_KEY_WIDTH:].astype(BF16)

    zs = _dot(u, wsm_ref[...])
    a = _dot(zs.astype(BF16), wa_ref[...]) + ba_ref[...]
    la_ref[...] = _log_sigmoid(a) * (1.0 / GLA_TAU)

    lf_t = _log_sigmoid(zs + bsm_ref[...]).T[:CUM_ROWS]
    upper = upper_ref[...]
    hi, mid, lo = _split3(lf_t)
    cs = _dot(hi, upper) + _dot(mid, upper) + _dot(lo, upper)
    carry = carry_ref[...]
    c_tile = cs + jnp.concatenate([carry] * (TM // LANES), axis=1)
    c_ref[0] = c_tile[:FOX_HEADS]
    carry_ref[...] = carry + jnp.sum(lf_t, axis=1, keepdims=True)


def _inproj(h2d, seq, g, wfox, wgla, wsm, bsm, wa, ba, upper):
    t = h2d.shape[0]
    batch = t // seq
    tiles_per_seq = seq // TM
    row = lambda w: pl.BlockSpec((TM, w), lambda i: (i, 0))
    out_shapes = (
        jax.ShapeDtypeStruct((t, FOX_WIDTH), BF16),
        jax.ShapeDtypeStruct((t, FOX_WIDTH), BF16),
        jax.ShapeDtypeStruct((t, FOX_WIDTH), BF16),
        jax.ShapeDtypeStruct((batch, FOX_HEADS, seq), F32),
        jax.ShapeDtypeStruct((t, GLA_KEY_WIDTH), F32),
        jax.ShapeDtypeStruct((t, GLA_KEY_WIDTH), F32),
        jax.ShapeDtypeStruct((t, GLA_VAL_WIDTH), BF16),
        jax.ShapeDtypeStruct((t, GLA_KEY_WIDTH), F32),
    )
    out_specs = (
        row(FOX_WIDTH), row(FOX_WIDTH), row(FOX_WIDTH),
        pl.BlockSpec((1, FOX_HEADS, TM), lambda i: (i // tiles_per_seq, 0, i % tiles_per_seq)),
        row(GLA_KEY_WIDTH), row(GLA_KEY_WIDTH), row(GLA_VAL_WIDTH), row(GLA_KEY_WIDTH),
    )
    return pl.pallas_call(
        functools.partial(_inproj_kernel, tiles_per_seq),
        grid=(t // TM,),
        in_specs=[row(D_MODEL), _const_spec((1, D_MODEL)),
                  _const_spec((D_MODEL, 3 * FOX_WIDTH)),
                  _const_spec((D_MODEL, 2 * GLA_KEY_WIDTH + GLA_VAL_WIDTH)),
                  _const_spec((D_MODEL, SMALL_W)), _const_spec((1, SMALL_W)),
                  _const_spec((SMALL_W, GLA_KEY_WIDTH)), _const_spec((1, GLA_KEY_WIDTH)),
                  _const_spec((TM, TM))],
        out_specs=out_specs,
        out_shape=out_shapes,
        scratch_shapes=[pltpu.VMEM((CUM_ROWS, LANES), F32)],
        compiler_params=_params(1),
        name="inproj",
    )(h2d, g, wfox, wgla, wsm, bsm, wa, ba, upper)


def _fox_kernel(seq, q_ref, k_ref, v_ref, c_ref, o_ref, vx_ref):
    lane = lax.broadcasted_iota(jnp.int32, (1, HEAD_PAIR), 1)
    head_mask = (lane < FOX_HEAD_DIM, lane >= FOX_HEAD_DIM)
    qrows = FOX_SUB * TQ

    v_all = v_ref[0]
    for j in range(2):
        ones = jnp.broadcast_to(head_mask[j].astype(BF16), v_all.shape)
        vx_ref[j] = jnp.concatenate(
            [jnp.where(head_mask[j], v_all, jnp.zeros_like(v_all)), ones], axis=1)

    def step(state, k0, width, row_offset):
        qm, m, acc = state
        k = k_ref[0, pl.ds(k0, width), :]
        alphas = []
        m_out = []
        pv = None
        for j in range(2):
            s = _dot_nt(qm[j], k) - c_ref[0, 0, j:j + 1, pl.ds(k0, width)]
            if row_offset is not None:
                causal = (lax.broadcasted_iota(jnp.int32, (TQ, width), 0) + row_offset
                          >= lax.broadcasted_iota(jnp.int32, (TQ, width), 1))
                s = jnp.where(causal, s, -jnp.inf)
            m_next = jnp.maximum(m[j], jnp.max(s, axis=1, keepdims=True))
            p = jnp.exp(s - jnp.concatenate([m_next] * (width // LANES), axis=1))
            alphas.append(jnp.exp(m[j] - m_next))
            m_out.append(m_next)
            contrib = _dot(p.astype(BF16), vx_ref[j, pl.ds(k0, width), :])
            pv = contrib if pv is None else pv + contrib
        alpha = jnp.where(head_mask[0], alphas[0], alphas[1])
        return qm, m_out, acc * jnp.concatenate([alpha, alpha], axis=1) + pv

    for qi in range(seq // qrows):
        q0 = qi * qrows
        states = []
        for sub in range(FOX_SUB):
            q = q_ref[0, pl.ds(q0 + sub * TQ, TQ), :]
            qm = [jnp.where(head_mask[j], q, jnp.zeros_like(q)) for j in range(2)]
            m = [jnp.full((TQ, LANES), -jnp.inf, F32)] * 2
            states.append((qm, m, jnp.zeros((TQ, 2 * HEAD_PAIR), F32)))
        for kj in range(qi):
            for sub in range(FOX_SUB):
                states[sub] = step(states[sub], kj * qrows, qrows, None)
        for sub in range(FOX_SUB):
            acc = step(states[sub], q0, (sub + 1) * TQ, sub * TQ)[2]
            o_ref[0, pl.ds(q0 + sub * TQ, TQ), :] = (
                acc[:, :HEAD_PAIR] / acc[:, HEAD_PAIR:]).astype(BF16)


def _fox(fq, fk, fv, c):
    batch, seq, _ = fq.shape
    n_pairs = FOX_WIDTH // HEAD_PAIR
    qkv = pl.BlockSpec((1, seq, HEAD_PAIR), lambda b, p: (b, 0, p))
    return pl.pallas_call(
        functools.partial(_fox_kernel, seq),
        grid=(batch, n_pairs),
        in_specs=[qkv, qkv, qkv, pl.BlockSpec((1, 1, 2, seq), lambda b, p: (b, p, 0, 0))],
        out_specs=qkv,
        out_shape=jax.ShapeDtypeStruct((batch, seq, FOX_WIDTH), BF16),
        scratch_shapes=[pltpu.VMEM((2, seq, 2 * HEAD_PAIR), BF16)],
        compiler_params=_params(2),
        name="fox",
    )(fq, fk, fv, c)


def _gla_kernel(q_ref, k_ref, v_ref, la_ref, ltri_ref, lone_ref, o_ref, st_ref):
    @pl.when(pl.program_id(1) == 0)
    def _():
        st_ref[...] = jnp.zeros_like(st_ref)

    n_chunks = TT // GLA_CHUNK
    lane = lax.broadcasted_iota(jnp.int32, (1, HEAD_PAIR), 1)
    head_mask = (lane < GLA_DK, lane >= GLA_DK)
    row_chunk = lax.broadcasted_iota(jnp.int32, (TT, 1), 0) // GLA_CHUNK
    rr = lax.broadcasted_iota(jnp.int32, (TT, TT), 0)
    cc = lax.broadcasted_iota(jnp.int32, (TT, TT), 1)
    intra = (rr >= cc) & (rr // GLA_CHUNK == cc // GLA_CHUNK)
    ltri = ltri_ref[...]
    lone = lone_ref[...]

    for p in range(GLA_HEADS // 2):
        ksl = slice(p * HEAD_PAIR, (p + 1) * HEAD_PAIR)
        hi, lo = _split2(la_ref[:, ksl])
        b = _dot(ltri, hi) + _dot(ltri, lo)
        b_last = _dot(lone, hi) + _dot(lone, lo)
        q_in = (q_ref[:, ksl] * jnp.exp(b)).astype(BF16)
        k = k_ref[:, ksl]
        k_in = (k * jnp.exp(-b)).astype(BF16)
        k_st = (k * jnp.exp(b_last - b)).astype(BF16)

        kv = None
        for j in range(2):
            h = 2 * p + j
            vj = v_ref[:, h * GLA_DV:(h + 1) * GLA_DV]
            kcat = jnp.concatenate(
                [jnp.where((row_chunk == i) & head_mask[j], k_st, jnp.zeros_like(k_st))
                 for i in range(n_chunks)], axis=1)
            contrib = _dot_tn(vj, kcat)
            kv = contrib if kv is None else kv + contrib

        state = st_ref[p]
        states = []
        for i in range(n_chunks):
            states.append(state.astype(BF16))
            decay = jnp.exp(b_last[i * GLA_CHUNK:i * GLA_CHUNK + 1, :])
            state = state * decay + kv[:, i * HEAD_PAIR:(i + 1) * HEAD_PAIR]
        st_ref[p] = state
        st_cat = jnp.concatenate(states, axis=1)

        for j in range(2):
            h = 2 * p + j
            vj = v_ref[:, h * GLA_DV:(h + 1) * GLA_DV]
            qh = jnp.where(head_mask[j], q_in, jnp.zeros_like(q_in))
            attn = jnp.where(intra, _dot_nt(qh, k_in), 0.0).astype(BF16)
            qcat = jnp.concatenate(
                [jnp.where(row_chunk == i, qh, jnp.zeros_like(qh)) for i in range(n_chunks)],
                axis=1)
            o_ref[:, h * GLA_DV:(h + 1) * GLA_DV] = _dot(attn, vj) + _dot_nt(qcat, st_cat)


def _gla(gq, gk, gv, la, seq, ltri, lone):
    t = gq.shape[0]
    batch = t // seq
    tiles = seq // TT
    row = lambda w: pl.BlockSpec((TT, w), lambda b, i: (b * tiles + i, 0))
    return pl.pallas_call(
        _gla_kernel,
        grid=(batch, tiles),
        in_specs=[row(GLA_KEY_WIDTH), row(GLA_KEY_WIDTH), row(GLA_VAL_WIDTH), row(GLA_KEY_WIDTH),
                  _const_spec((TT, TT)), _const_spec((TT, TT))],
        out_specs=row(GLA_VAL_WIDTH),
        out_shape=jax.ShapeDtypeStruct((t, GLA_VAL_WIDTH), F32),
        scratch_shapes=[pltpu.VMEM((GLA_HEADS // 2, GLA_DV, HEAD_PAIR), F32)],
        compiler_params=_params(2),
        name="gla",
    )(gq, gk, gv, la, ltri, lone)


def _outproj_kernel(h_ref, ofox_ref, ogla_ref, gpre_ref, wgate_ref, gnorm_ref, wbf_ref, wbg_ref,
                    wout_ref, gpost_ref, o_ref):
    h = h_ref[...]
    u = _rms(h, gpre_ref[...]).astype(BF16)
    gates = _dot(u, wgate_ref[...])
    g_r = gates[:, :GLA_VAL_WIDTH]
    gate_fox = gates[:, GLA_VAL_WIDTH:GLA_VAL_WIDTH + D_MODEL]
    gate_gla = gates[:, GLA_VAL_WIDTH + D_MODEL:]

    heads = []
    for hd in range(GLA_HEADS):
        sl = slice(hd * GLA_DV, (hd + 1) * GLA_DV)
        heads.append(_rms(ogla_ref[:, sl], gnorm_ref[:, sl]))
    o_gla = jnp.concatenate(heads, axis=1)
    o_gla = (o_gla * (g_r * jax.nn.sigmoid(g_r))).astype(BF16)

    y = (jax.nn.sigmoid(gate_fox) * _dot(ofox_ref[...], wbf_ref[...])
         + jax.nn.sigmoid(gate_gla) * _dot(o_gla, wbg_ref[...]))
    m = _dot(y.astype(BF16), wout_ref[...])
    o_ref[...] = h + _rms(m, gpost_ref[...])


def _outproj(h2d, o_fox, o_gla, g_pre, wgate, gnorm, wbf, wbg, wout, g_post):
    t = h2d.shape[0]
    row = lambda w: pl.BlockSpec((TM, w), lambda i: (i, 0))
    return pl.pallas_call(
        _outproj_kernel,
        grid=(t // TM,),
        in_specs=[row(D_MODEL), row(FOX_WIDTH), row(GLA_VAL_WIDTH), _const_spec((1, D_MODEL)),
                  _const_spec((D_MODEL, GLA_VAL_WIDTH + 2 * D_MODEL)),
                  _const_spec((1, GLA_VAL_WIDTH)),
                  _const_spec((FOX_WIDTH, D_MODEL)), _const_spec((GLA_VAL_WIDTH, D_MODEL)),
                  _const_spec((D_MODEL, D_MODEL)), _const_spec((1, D_MODEL))],
        out_specs=row(D_MODEL),
        out_shape=jax.ShapeDtypeStruct((t, D_MODEL), F32),
        compiler_params=_params(1),
        name="outproj",
    )(h2d, o_fox, o_gla, g_pre, wgate, gnorm, wbf, wbg, wout, g_post)


def _chunk_matrices(n):
    r = jnp.arange(n)[:, None]
    c = jnp.arange(n)[None, :]
    same = (r // GLA_CHUNK) == (c // GLA_CHUNK)
    return (same & (r >= c)).astype(BF16), same.astype(BF16)


def _layer(h2d, batch, seq, ffn1_pre_g, ffn1_w_gate, ffn1_w_up, ffn1_w_down, ffn1_post_g,
           mix_pre_g, w_in, b_forget, w_alpha_up, b_alpha, gla_norm_g, w_branch_fox,
           w_branch_gla, w_out, mix_post_g, ffn2_pre_g, ffn2_w_gate, ffn2_w_up, ffn2_w_down,
           ffn2_post_g):
    vec = lambda g: g.reshape(1, -1)
    bf = lambda w: w.astype(BF16)

    splits = [FOX_WIDTH, FOX_WIDTH, FOX_WIDTH, FOX_HEADS, GLA_KEY_WIDTH, GLA_KEY_WIDTH,
              GLA_VAL_WIDTH, GLA_GATE_RANK, GLA_VAL_WIDTH, D_MODEL, D_MODEL]
    offs = [0]
    for s in splits:
        offs.append(offs[-1] + s)
    col = lambda a, b: w_in[:, offs[a]:offs[b]]
    wfox = bf(col(0, 3))
    wgla = bf(col(4, 7))
    pad_cols = SMALL_W - FOX_HEADS - GLA_GATE_RANK
    wsm = bf(jnp.concatenate([col(3, 4), col(7, 8), jnp.zeros((D_MODEL, pad_cols), F32)], axis=1))
    bsm = jnp.concatenate([b_forget, jnp.zeros((SMALL_W - FOX_HEADS,), F32)]).reshape(1, SMALL_W)
    wa = bf(jnp.concatenate([jnp.zeros((FOX_HEADS, GLA_KEY_WIDTH), F32), w_alpha_up,
                             jnp.zeros((pad_cols, GLA_KEY_WIDTH), F32)], axis=0))
    wgate = bf(col(8, 11))
    tok = jnp.arange(TM)
    upper = (tok[:, None] <= tok[None, :]).astype(BF16)
    ltri, lone = _chunk_matrices(TT)

    h1 = _ffn(h2d, vec(ffn1_pre_g), bf(ffn1_w_gate), bf(ffn1_w_up), bf(ffn1_w_down),
              vec(ffn1_post_g))

    fq, fk, fv, c, gq, gk, gv, la = _inproj(h1, seq, vec(mix_pre_g), wfox, wgla, wsm, bsm, wa,
                                            vec(b_alpha), upper)
    shape3 = (batch, seq, FOX_WIDTH)
    o_fox = _fox(fq.reshape(shape3), fk.reshape(shape3), fv.reshape(shape3),
                 c.reshape(batch, FOX_HEADS // 2, 2, seq)).reshape(batch * seq, FOX_WIDTH)
    o_gla = _gla(gq, gk, gv, la, seq, ltri, lone)

    h2 = _outproj(h1, o_fox, o_gla, vec(mix_pre_g), wgate, vec(gla_norm_g), bf(w_branch_fox),
                  bf(w_branch_gla), bf(w_out), vec(mix_post_g))

    return _ffn(h2, vec(ffn2_pre_g), bf(ffn2_w_gate), bf(ffn2_w_up), bf(ffn2_w_down),
                vec(ffn2_post_g))


def kernel(x, ffn1_pre_g, ffn1_w_gate, ffn1_w_up, ffn1_w_down, ffn1_post_g, mix_pre_g, w_in,
           b_forget, w_alpha_up, b_alpha, gla_norm_g, w_branch_fox, w_branch_gla, w_out,
           mix_post_g, ffn2_pre_g, ffn2_w_gate, ffn2_w_up, ffn2_w_down, ffn2_post_g):
    batch, seq, d = x.shape
    h = x.reshape(batch * seq, d)
    depth = ffn1_pre_g.shape[0]
    for l in range(depth):
        h = _layer(h, batch, seq, ffn1_pre_g[l], ffn1_w_gate[l], ffn1_w_up[l], ffn1_w_down[l],
                   ffn1_post_g[l], mix_pre_g[l], w_in[l], b_forget[l], w_alpha_up[l], b_alpha[l],
                   gla_norm_g[l], w_branch_fox[l], w_branch_gla[l], w_out[l], mix_post_g[l],
                   ffn2_pre_g[l], ffn2_w_gate[l], ffn2_w_up[l], ffn2_w_down[l], ffn2_post_g[l])
    return h.reshape(batch, seq, d)
```

```python
import functools

import jax
import jax.numpy as jnp
from jax import lax
from jax.experimental import pallas as pl
from jax.experimental.pallas import tpu as pltpu

F32 = jnp.float32
BF16 = jnp.bfloat16

D_MODEL = 1024
D_FF = 2816
FOX_HEADS = 8
FOX_HEAD_DIM = 64
FOX_WIDTH = FOX_HEADS * FOX_HEAD_DIM
GLA_HEADS = 4
GLA_DK = 64
GLA_DV = 128
GLA_KEY_WIDTH = GLA_HEADS * GLA_DK
GLA_VAL_WIDTH = GLA_HEADS * GLA_DV
GLA_GATE_RANK = 16
GLA_TAU = 16.0
GLA_CHUNK = 64
NORM_EPS = 1e-6

LANES = 128
HEAD_PAIR = LANES
VMEM_LIMIT_BYTES = 56 * 1024 * 1024

TM = 512
FFN_TM = 1024
FF_CHUNK = 256
FFN_SUB = 512
TQ = 256
FOX_SUB = 2
TT = 256
SMALL_W = LANES
CUM_ROWS = 16


def _rms(x, g):
    return x * lax.rsqrt(jnp.mean(x * x, axis=-1, keepdims=True) + NORM_EPS) * g


def _log_sigmoid(x):
    return jnp.minimum(x, 0.0) - jnp.log1p(jnp.exp(-jnp.abs(x)))


def _dot(a, b):
    return jnp.dot(a, b, preferred_element_type=F32)


def _dot_nt(a, b):
    return lax.dot_general(a, b, (((1,), (1,)), ((), ())), preferred_element_type=F32)


def _dot_tn(a, b):
    return lax.dot_general(a, b, (((0,), (0,)), ((), ())), preferred_element_type=F32)


def _split2(x):
    hi = x.astype(BF16)
    lo = (x - hi.astype(F32)).astype(BF16)
    return hi, lo


def _split3(x):
    hi = x.astype(BF16)
    r = x - hi.astype(F32)
    mid = r.astype(BF16)
    lo = (r - mid.astype(F32)).astype(BF16)
    return hi, mid, lo


def _const_spec(shape):
    return pl.BlockSpec(shape, lambda *_: (0,) * len(shape), pipeline_mode=pl.Buffered(1))


def _params(n_axes):
    return pltpu.CompilerParams(dimension_semantics=("arbitrary",) * n_axes,
                                vmem_limit_bytes=VMEM_LIMIT_BYTES)


def _ffn_kernel(x_ref, gpre_ref, wg_ref, wu_ref, wd_ref, gpost_ref, o_ref):
    subs = [slice(s * FFN_SUB, (s + 1) * FFN_SUB) for s in range(FFN_TM // FFN_SUB)]
    xn = [_rms(x_ref[rows, :], gpre_ref[...]).astype(BF16) for rows in subs]
    acc = [None] * len(subs)
    for c in range(D_FF // FF_CHUNK):
        cols = slice(c * FF_CHUNK, (c + 1) * FF_CHUNK)
        for s in range(len(subs)):
            g = _dot(xn[s], wg_ref[:, cols])
            u = _dot(xn[s], wu_ref[:, cols])
            h = (g * jax.nn.sigmoid(g) * u).astype(BF16)
            part = _dot(h, wd_ref[cols, :])
            acc[s] = part if acc[s] is None else acc[s] + part
    for s, rows in enumerate(subs):
        o_ref[rows, :] = x_ref[rows, :] + 0.5 * _rms(acc[s], gpost_ref[...])


def _ffn(x2d, g_pre, w_gate, w_up, w_down, g_post):
    t = x2d.shape[0]
    row = pl.BlockSpec((FFN_TM, D_MODEL), lambda i: (i, 0))
    return pl.pallas_call(
        _ffn_kernel,
        grid=(t // FFN_TM,),
        in_specs=[row, _const_spec((1, D_MODEL)), _const_spec((D_MODEL, D_FF)),
                  _const_spec((D_MODEL, D_FF)), _const_spec((D_FF, D_MODEL)),
                  _const_spec((1, D_MODEL))],
        out_specs=row,
        out_shape=jax.ShapeDtypeStruct((t, D_MODEL), F32),
        compiler_params=_params(1),
        name="ffn",
    )(x2d, g_pre, w_gate, w_up, w_down, g_post)


def _inproj_kernel(tiles_per_seq, h_ref, g_ref, wfox_ref, wgla_ref, wsm_ref, bsm_ref, wa_ref,
                   ba_ref, upper_ref, fq_ref, fk_ref, fv_ref, c_ref, gq_ref, gk_ref, gv_ref,
                   la_ref, carry_ref):
    @pl.when(pl.program_id(0) % tiles_per_seq == 0)
    def _():
        carry_ref[...] = jnp.zeros_like(carry_ref)

    u = _rms(h_ref[...], g_ref[...]).astype(BF16)
    scale = FOX_HEAD_DIM ** -0.5

    zs = _dot(u, wsm_ref[...])

    zf = _dot(u, wfox_ref[...])
    fq_ref[...] = (zf[:, :FOX_WIDTH] * scale).astype(BF16)
    fk_ref[...] = zf[:, FOX_WIDTH:2 * FOX_WIDTH].astype(BF16)
    fv_ref[...] = zf[:, 2 * FOX_WIDTH:].astype(BF16)

    a = _dot(zs.astype(BF16), wa_ref[...]) + ba_ref[...]
    la_ref[...] = _log_sigmoid(a) * (1.0 / GLA_TAU)

    lf_t = _log_sigmoid(zs + bsm_ref[...]).T[:CUM_ROWS]
    upper = upper_ref[...]
    hi, mid, lo = _split3(lf_t)
    cs = _dot(hi, upper) + _dot(mid, upper) + _dot(lo, upper)
    carry = carry_ref[...]
    c_tile = cs + jnp.concatenate([carry] * (TM // LANES), axis=1)
    c_ref[0] = c_tile[:FOX_HEADS]
    carry_ref[...] = carry + jnp.sum(lf_t, axis=1, keepdims=True)

    zg = _dot(u, wgla_ref[...])
    gq_ref[...] = zg[:, :GLA_KEY_WIDTH] * (GLA_DK ** -0.5)
    gk_ref[...] = zg[:, GLA_KEY_WIDTH:2 * GLA_KEY_WIDTH]
    gv_ref[...] = zg[:, 2 * GLA_KEY_WIDTH:].astype(BF16)


def _inproj(h2d, seq, g, wfox, wgla, wsm, bsm, wa, ba, upper):
    t = h2d.shape[0]
    batch = t // seq
    tiles_per_seq = seq // TM
    row = lambda w: pl.BlockSpec((TM, w), lambda i: (i, 0))
    out_shapes = (
        jax.ShapeDtypeStruct((t, FOX_WIDTH), BF16),
        jax.ShapeDtypeStruct((t, FOX_WIDTH), BF16),
        jax.ShapeDtypeStruct((t, FOX_WIDTH), BF16),
        jax.ShapeDtypeStruct((batch, FOX_HEADS, seq), F32),
        jax.ShapeDtypeStruct((t, GLA_KEY_WIDTH), F32),
        jax.ShapeDtypeStruct((t, GLA_KEY_WIDTH), F32),
        jax.ShapeDtypeStruct((t, GLA_VAL_WIDTH), BF16),
        jax.ShapeDtypeStruct((t, GLA_KEY_WIDTH), F32),
    )
    out_specs = (
        row(FOX_WIDTH), row(FOX_WIDTH), row(FOX_WIDTH),
        pl.BlockSpec((1, FOX_HEADS, TM), lambda i: (i // tiles_per_seq, 0, i % tiles_per_seq)),
        row(GLA_KEY_WIDTH), row(GLA_KEY_WIDTH), row(GLA_VAL_WIDTH), row(GLA_KEY_WIDTH),
    )
    return pl.pallas_call(
        functools.partial(_inproj_kernel, tiles_per_seq),
        grid=(t // TM,),
        in_specs=[row(D_MODEL), _const_spec((1, D_MODEL)),
                  _const_spec((D_MODEL, 3 * FOX_WIDTH)),
                  _const_spec((D_MODEL, 2 * GLA_KEY_WIDTH + GLA_VAL_WIDTH)),
                  _const_spec((D_MODEL, SMALL_W)), _const_spec((1, SMALL_W)),
                  _const_spec((SMALL_W, GLA_KEY_WIDTH)), _const_spec((1, GLA_KEY_WIDTH)),
                  _const_spec((TM, TM))],
        out_specs=out_specs,
        out_shape=out_shapes,
        scratch_shapes=[pltpu.VMEM((CUM_ROWS, LANES), F32)],
        compiler_params=_params(1),
        name="inproj",
    )(h2d, g, wfox, wgla, wsm, bsm, wa, ba, upper)


def _fox_kernel(seq, q_ref, k_ref, v_ref, c_ref, o_ref, vx_ref):
    lane = lax.broadcasted_iota(jnp.int32, (1, HEAD_PAIR), 1)
    head_mask = (lane < FOX_HEAD_DIM, lane >= FOX_HEAD_DIM)
    qrows = FOX_SUB * TQ

    v_all = v_ref[0]
    for j in range(2):
        ones = jnp.broadcast_to(head_mask[j].astype(BF16), v_all.shape)
        vx_ref[j] = jnp.concatenate(
            [jnp.where(head_mask[j], v_all, jnp.zeros_like(v_all)), ones], axis=1)

    def step(state, k0, width, row_offset):
        qm, m, acc = state
        k = k_ref[0, pl.ds(k0, width), :]
        alphas = []
        m_out = []
        pv = None
        for j in range(2):
            s = _dot_nt(qm[j], k) - c_ref[0, 0, j:j + 1, pl.ds(k0, width)]
            if row_offset is not None:
                causal = (lax.broadcasted_iota(jnp.int32, (TQ, width), 0) + row_offset
                          >= lax.broadcasted_iota(jnp.int32, (TQ, width), 1))
                s = jnp.where(causal, s, -jnp.inf)
            m_next = jnp.maximum(m[j], jnp.max(s, axis=1, keepdims=True))
            p = jnp.exp(s - jnp.concatenate([m_next] * (width // LANES), axis=1))
            alphas.append(jnp.exp(m[j] - m_next))
            m_out.append(m_next)
            contrib = _dot(p.astype(BF16), vx_ref[j, pl.ds(k0, width), :])
            pv = contrib if pv is None else pv + contrib
        alpha = jnp.where(head_mask[0], alphas[0], alphas[1])
        return qm, m_out, acc * jnp.concatenate([alpha, alpha], axis=1) + pv

    for qi in range(seq // qrows):
        q0 = qi * qrows
        states = []
        for sub in range(FOX_SUB):
            q = q_ref[0, pl.ds(q0 + sub * TQ, TQ), :]
            qm = [jnp.where(head_mask[j], q, jnp.zeros_like(q)) for j in range(2)]
            m = [jnp.full((TQ, LANES), -jnp.inf, F32)] * 2
            states.append((qm, m, jnp.zeros((TQ, 2 * HEAD_PAIR), F32)))
        for kj in range(qi):
            for sub in range(FOX_SUB):
                states[sub] = step(states[sub], kj * qrows, qrows, None)
        for sub in range(FOX_SUB):
            acc = step(states[sub], q0, (sub + 1) * TQ, sub * TQ)[2]
            o_ref[0, pl.ds(q0 + sub * TQ, TQ), :] = (
                acc[:, :HEAD_PAIR] / acc[:, HEAD_PAIR:]).astype(BF16)


def _fox(fq, fk, fv, c):
    batch, seq, _ = fq.shape
    n_pairs = FOX_WIDTH // HEAD_PAIR
    qkv = pl.BlockSpec((1, seq, HEAD_PAIR), lambda b, p: (b, 0, p))
    return pl.pallas_call(
        functools.partial(_fox_kernel, seq),
        grid=(batch, n_pairs),
        in_specs=[qkv, qkv, qkv, pl.BlockSpec((1, 1, 2, seq), lambda b, p: (b, p, 0, 0))],
        out_specs=qkv,
        out_shape=jax.ShapeDtypeStruct((batch, seq, FOX_WIDTH), BF16),
        scratch_shapes=[pltpu.VMEM((2, seq, 2 * HEAD_PAIR), BF16)],
        compiler_params=_params(2),
        name="fox",
    )(fq, fk, fv, c)


def _gla_kernel(seq, q_ref, k_ref, v_ref, la_ref, ltri_ref, o_ref):
    n_chunks = TT // GLA_CHUNK
    lane = lax.broadcasted_iota(jnp.int32, (1, HEAD_PAIR), 1)
    head_mask = (lane < GLA_DK, lane >= GLA_DK)
    row_chunk = lax.broadcasted_iota(jnp.int32, (TT, 1), 0) // GLA_CHUNK
    rr = lax.broadcasted_iota(jnp.int32, (TT, TT), 0)
    cc = lax.broadcasted_iota(jnp.int32, (TT, TT), 1)
    intra = (rr >= cc) & (rr // GLA_CHUNK == cc // GLA_CHUNK)
    ltri = ltri_ref[...]

    state = [jnp.zeros((GLA_DV, HEAD_PAIR), F32) for _ in range(GLA_HEADS // 2)]

    for t in range(seq // TT):
        rows = slice(t * TT, (t + 1) * TT)
        hi, lo = _split2(la_ref[rows, :])
        b_all = _dot(ltri, hi) + _dot(ltri, lo)
        for p in range(GLA_HEADS // 2):
            ksl = slice(p * HEAD_PAIR, (p + 1) * HEAD_PAIR)
            b = b_all[:, ksl]
            last_rows = [b[(i + 1) * GLA_CHUNK - 1:(i + 1) * GLA_CHUNK, :] for i in range(n_chunks)]
            b_last = jnp.concatenate(
                [jnp.broadcast_to(r, (GLA_CHUNK, HEAD_PAIR)) for r in last_rows], axis=0)
            q_in = (q_ref[rows, ksl] * jnp.exp(b)).astype(BF16)
            k = k_ref[rows, ksl]
            k_in = (k * jnp.exp(-b)).astype(BF16)
            k_st = (k * jnp.exp(b_last - b)).astype(BF16)

            kv = None
            for j in range(2):
                h = 2 * p + j
                vj = v_ref[rows, h * GLA_DV:(h + 1) * GLA_DV]
                kcat = jnp.concatenate(
                    [jnp.where((row_chunk == i) & head_mask[j], k_st, jnp.zeros_like(k_st))
                     for i in range(n_chunks)], axis=1)
                contrib = _dot_tn(vj, kcat)
                kv = contrib if kv is None else kv + contrib

            o_inter = []
            for i in range(n_chunks):
                st = state[p].astype(BF16)
                st2 = jnp.concatenate(
                    [jnp.where(head_mask[j], st, jnp.zeros_like(st)) for j in range(2)], axis=0)
                o_inter.append(_dot_nt(q_in[i * GLA_CHUNK:(i + 1) * GLA_CHUNK, :], st2))
                state[p] = (state[p] * jnp.exp(last_rows[i])
                            + kv[:, i * HEAD_PAIR:(i + 1) * HEAD_PAIR])

            o_intra = []
            for j in range(2):
                h = 2 * p + j
                vj = v_ref[rows, h * GLA_DV:(h + 1) * GLA_DV]
                qh = jnp.where(head_mask[j], q_in, jnp.zeros_like(q_in))
                attn = jnp.where(intra, _dot_nt(qh, k_in), 0.0).astype(BF16)
                o_intra.append(_dot(attn, vj))
            o_ref[rows, 2 * p * GLA_DV:(2 * p + 2) * GLA_DV] = (
                jnp.concatenate(o_intra, axis=1) + jnp.concatenate(o_inter, axis=0))


def _gla(gq, gk, gv, la, seq, ltri):
    t = gq.shape[0]
    row = lambda w: pl.BlockSpec((seq, w), lambda b: (b, 0))
    return pl.pallas_call(
        functools.partial(_gla_kernel, seq),
        grid=(t // seq,),
        in_specs=[row(GLA_KEY_WIDTH), row(GLA_KEY_WIDTH), row(GLA_VAL_WIDTH), row(GLA_KEY_WIDTH),
                  _const_spec((TT, TT))],
        out_specs=row(GLA_VAL_WIDTH),
        out_shape=jax.ShapeDtypeStruct((t, GLA_VAL_WIDTH), F32),
        compiler_params=_params(1),
        name="gla",
    )(gq, gk, gv, la, ltri)


def _outproj_kernel(h_ref, ofox_ref, ogla_ref, gpre_ref, wgate_ref, gnorm_ref, wbf_ref, wbg_ref,
                    wout_ref, gpost_ref, o_ref):
    h = h_ref[...]
    u = _rms(h, gpre_ref[...]).astype(BF16)
    gates = _dot(u, wgate_ref[...])
    g_r = gates[:, :GLA_VAL_WIDTH]
    gate_fox = gates[:, GLA_VAL_WIDTH:GLA_VAL_WIDTH + D_MODEL]
    gate_gla = gates[:, GLA_VAL_WIDTH + D_MODEL:]

    heads = []
    for hd in range(GLA_HEADS):
        sl = slice(hd * GLA_DV, (hd + 1) * GLA_DV)
        heads.append(_rms(ogla_ref[:, sl], gnorm_ref[:, sl]))
    o_gla = jnp.concatenate(heads, axis=1)
    o_gla = (o_gla * (g_r * jax.nn.sigmoid(g_r))).astype(BF16)

    y = (jax.nn.sigmoid(gate_fox) * _dot(ofox_ref[...], wbf_ref[...])
         + jax.nn.sigmoid(gate_gla) * _dot(o_gla, wbg_ref[...]))
    m = _dot(y.astype(BF16), wout_ref[...])
    o_ref[...] = h + _rms(m, gpost_ref[...])


def _outproj(h2d, o_fox, o_gla, g_pre, wgate, gnorm, wbf, wbg, wout, g_post):
    t = h2d.shape[0]
    row = lambda w: pl.BlockSpec((TM, w), lambda i: (i, 0))
    return pl.pallas_call(
        _outproj_kernel,
        grid=(t // TM,),
        in_specs=[row(D_MODEL), row(FOX_WIDTH), row(GLA_VAL_WIDTH), _const_spec((1, D_MODEL)),
                  _const_spec((D_MODEL, GLA_VAL_WIDTH + 2 * D_MODEL)),
                  _const_spec((1, GLA_VAL_WIDTH)),
                  _const_spec((FOX_WIDTH, D_MODEL)), _const_spec((GLA_VAL_WIDTH, D_MODEL)),
                  _const_spec((D_MODEL, D_MODEL)), _const_spec((1, D_MODEL))],
        out_specs=row(D_MODEL),
        out_shape=jax.ShapeDtypeStruct((t, D_MODEL), F32),
        compiler_params=_params(1),
        name="outproj",
    )(h2d, o_fox, o_gla, g_pre, wgate, gnorm, wbf, wbg, wout, g_post)


def _chunk_tril(n):
    r = jnp.arange(n)[:, None]
    c = jnp.arange(n)[None, :]
    return (((r // GLA_CHUNK) == (c // GLA_CHUNK)) & (r >= c)).astype(BF16)


def _layer(h2d, batch, seq, ffn1_pre_g, ffn1_w_gate, ffn1_w_up, ffn1_w_down, ffn1_post_g,
           mix_pre_g, w_in, b_forget, w_alpha_up, b_alpha, gla_norm_g, w_branch_fox,
           w_branch_gla, w_out, mix_post_g, ffn2_pre_g, ffn2_w_gate, ffn2_w_up, ffn2_w_down,
           ffn2_post_g):
    vec = lambda g: g.reshape(1, -1)
    bf = lambda w: w.astype(BF16)

    splits = [FOX_WIDTH, FOX_WIDTH, FOX_WIDTH, FOX_HEADS, GLA_KEY_WIDTH, GLA_KEY_WIDTH,
              GLA_VAL_WIDTH, GLA_GATE_RANK, GLA_VAL_WIDTH, D_MODEL, D_MODEL]
    offs = [0]
    for s in splits:
        offs.append(offs[-1] + s)
    col = lambda a, b: w_in[:, offs[a]:offs[b]]
    wfox = bf(col(0, 3))
    wgla = bf(col(4, 7))
    pad_cols = SMALL_W - FOX_HEADS - GLA_GATE_RANK
    wsm = bf(jnp.concatenate([col(3, 4), col(7, 8), jnp.zeros((D_MODEL, pad_cols), F32)], axis=1))
    bsm = jnp.concatenate([b_forget, jnp.zeros((SMALL_W - FOX_HEADS,), F32)]).reshape(1, SMALL_W)
    wa = bf(jnp.concatenate([jnp.zeros((FOX_HEADS, GLA_KEY_WIDTH), F32), w_alpha_up,
                             jnp.zeros((pad_cols, GLA_KEY_WIDTH), F32)], axis=0))
    wgate = bf(col(8, 11))
    tok = jnp.arange(TM)
    upper = (tok[:, None] <= tok[None, :]).astype(BF16)
    ltri = _chunk_tril(TT)

    h1 = _ffn(h2d, vec(ffn1_pre_g), bf(ffn1_w_gate), bf(ffn1_w_up), bf(ffn1_w_down),
              vec(ffn1_post_g))

    fq, fk, fv, c, gq, gk, gv, la = _inproj(h1, seq, vec(mix_pre_g), wfox, wgla, wsm, bsm, wa,
                                            vec(b_alpha), upper)
    shape3 = (batch, seq, FOX_WIDTH)
    o_fox = _fox(fq.reshape(shape3), fk.reshape(shape3), fv.reshape(shape3),
                 c.reshape(batch, FOX_HEADS // 2, 2, seq)).reshape(batch * seq, FOX_WIDTH)
    o_gla = _gla(gq, gk, gv, la, seq, ltri)

    h2 = _outproj(h1, o_fox, o_gla, vec(mix_pre_g), wgate, vec(gla_norm_g), bf(w_branch_fox),
                  bf(w_branch_gla), bf(w_out), vec(mix_post_g))

    return _ffn(h2, vec(ffn2_pre_g), bf(ffn2_w_gate), bf(ffn2_w_up), bf(ffn2_w_down),
                vec(ffn2_post_g))


def kernel(x, ffn1_pre_g, ffn1_w_gate, ffn1_w_up, ffn1_w_down, ffn1_post_g, mix_pre_g, w_in,
           b_forget, w_alpha_up, b_alpha, gla_norm_g, w_branch_fox, w_branch_gla, w_out,
           mix_post_g, ffn2_pre_g, ffn2_w_gate, ffn2_w_up, ffn2_w_down, ffn2_post_g):
    batch, seq, d = x.shape
    h = x.reshape(batch * seq, d)
    depth = ffn1_pre_g.shape[0]
    for l in range(depth):
        h = _layer(h, batch, seq, ffn1_pre_g[l], ffn1_w_gate[l], ffn1_w_up[l], ffn1_w_down[l],
                   ffn1_post_g[l], mix_pre_g[l], w_in[l], b_forget[l], w_alpha_up[l], b_alpha[l],
                   gla_norm_g[l], w_branch_fox[l], w_branch_gla[l], w_out[l], mix_post_g[l],
                   ffn2_pre_g[l], ffn2_w_gate[l], ffn2_w_up[l], ffn2_w_down[l], ffn2_post_g[l])
    return h.reshape(batch, seq, d)
```

```python
import functools

import jax
import jax.numpy as jnp
from jax import lax
from jax.experimental import pallas as pl
from jax.experimental.pallas import tpu as pltpu

F32 = jnp.float32
BF16 = jnp.bfloat16

D_MODEL = 1024
D_FF = 2816
FOX_HEADS = 8
FOX_HEAD_DIM = 64
FOX_WIDTH = FOX_HEADS * FOX_HEAD_DIM
GLA_HEADS = 4
GLA_DK = 64
GLA_DV = 128
GLA_KEY_WIDTH = GLA_HEADS * GLA_DK
GLA_VAL_WIDTH = GLA_HEADS * GLA_DV
GLA_GATE_RANK = 16
GLA_TAU = 16.0
GLA_CHUNK = 64
NORM_EPS = 1e-6

LANES = 128
HEAD_PAIR = LANES
VMEM_LIMIT_BYTES = 56 * 1024 * 1024

IN_TM = 1024
IN_SUB = 512
FFN_TM = 1024
FF_CHUNK = 256
OUT_TM = 1024
OUT_SUB = 512
FFN_SUB = 512
TQ = 256
FOX_WIDE = 512
FOX_ONES_ROWS = 16
FOX_GROUP = 8
FOX_AHEAD = 4
LOG2_E = 1.4426950408889634
TT = 256
SMALL_W = LANES
CUM_ROWS = 16


def _rms(x, g):
    return x * lax.rsqrt(jnp.mean(x * x, axis=-1, keepdims=True) + NORM_EPS) * g


def _log_sigmoid(x):
    return jnp.minimum(x, 0.0) - jnp.log1p(jnp.exp(-jnp.abs(x)))


def _dot(a, b):
    return jnp.dot(a, b, preferred_element_type=F32)


def _dot_nt(a, b):
    return lax.dot_general(a, b, (((1,), (1,)), ((), ())), preferred_element_type=F32)


def _dot_tn(a, b):
    return lax.dot_general(a, b, (((0,), (0,)), ((), ())), preferred_element_type=F32)


def _split2(x):
    hi = x.astype(BF16)
    lo = (x - hi.astype(F32)).astype(BF16)
    return hi, lo


def _split3(x):
    hi = x.astype(BF16)
    r = x - hi.astype(F32)
    mid = r.astype(BF16)
    lo = (r - mid.astype(F32)).astype(BF16)
    return hi, mid, lo


def _const_spec(shape):
    return pl.BlockSpec(shape, lambda *_: (0,) * len(shape), pipeline_mode=pl.Buffered(1))


def _params(n_axes):
    return pltpu.CompilerParams(dimension_semantics=("arbitrary",) * n_axes,
                                vmem_limit_bytes=VMEM_LIMIT_BYTES)


def _ffn_kernel(x_ref, gpre_ref, wg_ref, wu_ref, wd_ref, gpost_ref, o_ref):
    subs = [slice(s * FFN_SUB, (s + 1) * FFN_SUB) for s in range(FFN_TM // FFN_SUB)]
    xn = [_rms(x_ref[rows, :], gpre_ref[...]).astype(BF16) for rows in subs]
    acc = [None] * len(subs)
    for c in range(D_FF // FF_CHUNK):
        cols = slice(c * FF_CHUNK, (c + 1) * FF_CHUNK)
        for s in range(len(subs)):
            g = _dot(xn[s], wg_ref[:, cols])
            u = _dot(xn[s], wu_ref[:, cols])
            h = (g * jax.nn.sigmoid(g) * u).astype(BF16)
            part = _dot(h, wd_ref[cols, :])
            acc[s] = part if acc[s] is None else acc[s] + part
    for s, rows in enumerate(subs):
        o_ref[rows, :] = x_ref[rows, :] + 0.5 * _rms(acc[s], gpost_ref[...])


def _ffn(x2d, g_pre, w_gate, w_up, w_down, g_post):
    t = x2d.shape[0]
    row = pl.BlockSpec((FFN_TM, D_MODEL), lambda i: (i, 0))
    return pl.pallas_call(
        _ffn_kernel,
        grid=(t // FFN_TM,),
        in_specs=[row, _const_spec((1, D_MODEL)), _const_spec((D_MODEL, D_FF)),
                  _const_spec((D_MODEL, D_FF)), _const_spec((D_FF, D_MODEL)),
                  _const_spec((1, D_MODEL))],
        out_specs=row,
        out_shape=jax.ShapeDtypeStruct((t, D_MODEL), F32),
        compiler_params=_params(1),
        name="ffn",
    )(x2d, g_pre, w_gate, w_up, w_down, g_post)


def _inproj_kernel(tiles_per_seq, h_ref, g_ref, wfox_ref, wgla_ref, wsm_ref, bsm_ref, wa_ref,
                   ba_ref, upper_ref, fq_ref, fk_ref, fv_ref, c_ref, gq_ref, gk_ref, gv_ref,
                   la_ref, carry_ref):
    @pl.when(pl.program_id(0) % tiles_per_seq == 0)
    def _():
        carry_ref[...] = jnp.zeros_like(carry_ref)

    subs = [slice(s * IN_SUB, (s + 1) * IN_SUB) for s in range(IN_TM // IN_SUB)]
    scale = FOX_HEAD_DIM ** -0.5 * LOG2_E
    u = [_rms(h_ref[rows, :], g_ref[...]).astype(BF16) for rows in subs]

    zs = [_dot(us, wsm_ref[...]) for us in u]

    for s, rows in enumerate(subs):
        zf = _dot(u[s], wfox_ref[...])
        fq_ref[rows, :] = (zf[:, :FOX_WIDTH] * scale).astype(BF16)
        fk_ref[rows, :] = zf[:, FOX_WIDTH:2 * FOX_WIDTH].astype(BF16)
        fv_ref[rows, :] = zf[:, 2 * FOX_WIDTH:].astype(BF16)

    for s, rows in enumerate(subs):
        a = _dot(zs[s].astype(BF16), wa_ref[...]) + ba_ref[...]
        la_ref[rows, :] = _log_sigmoid(a) * (1.0 / GLA_TAU)

    for s, rows in enumerate(subs):
        zg = _dot(u[s], wgla_ref[...])
        gq_ref[rows, :] = zg[:, :GLA_KEY_WIDTH] * (GLA_DK ** -0.5)
        gk_ref[rows, :] = zg[:, GLA_KEY_WIDTH:2 * GLA_KEY_WIDTH]
        gv_ref[rows, :] = zg[:, 2 * GLA_KEY_WIDTH:].astype(BF16)

    upper = upper_ref[...]
    carry = carry_ref[...]
    for s, rows in enumerate(subs):
        lf_t = _log_sigmoid(zs[s] + bsm_ref[...]).T[:CUM_ROWS]
        hi, mid, lo = _split3(lf_t)
        cs = _dot(hi, upper) + _dot(mid, upper) + _dot(lo, upper)
        c_tile = cs + jnp.concatenate([carry] * (IN_SUB // LANES), axis=1)
        c_ref[0, :, rows] = c_tile[:FOX_HEADS] * LOG2_E
        carry = carry + jnp.sum(lf_t, axis=1, keepdims=True)
    carry_ref[...] = carry


def _inproj(h2d, seq, g, wfox, wgla, wsm, bsm, wa, ba, upper):
    t = h2d.shape[0]
    batch = t // seq
    tiles_per_seq = seq // IN_TM
    row = lambda w: pl.BlockSpec((IN_TM, w), lambda i: (i, 0))
    out_shapes = (
        jax.ShapeDtypeStruct((t, FOX_WIDTH), BF16),
        jax.ShapeDtypeStruct((t, FOX_WIDTH), BF16),
        jax.ShapeDtypeStruct((t, FOX_WIDTH), BF16),
        jax.ShapeDtypeStruct((batch, FOX_HEADS, seq), F32),
        jax.ShapeDtypeStruct((t, GLA_KEY_WIDTH), F32),
        jax.ShapeDtypeStruct((t, GLA_KEY_WIDTH), F32),
        jax.ShapeDtypeStruct((t, GLA_VAL_WIDTH), BF16),
        jax.ShapeDtypeStruct((t, GLA_KEY_WIDTH), F32),
    )
    out_specs = (
        row(FOX_WIDTH), row(FOX_WIDTH), row(FOX_WIDTH),
        pl.BlockSpec((1, FOX_HEADS, IN_TM), lambda i: (i // tiles_per_seq, 0, i % tiles_per_seq)),
        row(GLA_KEY_WIDTH), row(GLA_KEY_WIDTH), row(GLA_VAL_WIDTH), row(GLA_KEY_WIDTH),
    )
    return pl.pallas_call(
        functools.partial(_inproj_kernel, tiles_per_seq),
        grid=(t // IN_TM,),
        in_specs=[row(D_MODEL), _const_spec((1, D_MODEL)),
                  _const_spec((D_MODEL, 3 * FOX_WIDTH)),
                  _const_spec((D_MODEL, 2 * GLA_KEY_WIDTH + GLA_VAL_WIDTH)),
                  _const_spec((D_MODEL, SMALL_W)), _const_spec((1, SMALL_W)),
                  _const_spec((SMALL_W, GLA_KEY_WIDTH)), _const_spec((1, GLA_KEY_WIDTH)),
                  _const_spec((IN_SUB, IN_SUB))],
        out_specs=out_specs,
        out_shape=out_shapes,
        scratch_shapes=[pltpu.VMEM((CUM_ROWS, LANES), F32)],
        compiler_params=_params(1),
        name="inproj",
    )(h2d, g, wfox, wgla, wsm, bsm, wa, ba, upper)


def _fox_kernel(seq, q_ref, k_ref, v_ref, c_ref, o_ref, kaug_ref, qaug_ref, vt_ref):
    lane = lax.broadcasted_iota(jnp.int32, (1, HEAD_PAIR), 1)
    row = lax.broadcasted_iota(jnp.int32, (HEAD_PAIR, 1), 0)
    k_all = k_ref[0]
    q_t = q_ref[0].astype(F32).T
    v_t = v_ref[0].astype(F32).T
    for j in range(2):
        lo_lane, hi_lane = j * FOX_HEAD_DIM, (j + 1) * FOX_HEAD_DIM
        bias0 = (1 - j) * FOX_HEAD_DIM
        parts = [p.astype(F32) for p in _split3(-c_ref[0, 0, j:j + 1, :])]
        bias_t = jnp.zeros((HEAD_PAIR, seq), F32)
        for i, part in enumerate(parts):
            bias_t = jnp.where(row == bias0 + i, part, bias_t)
        own_lane = (lane >= lo_lane) & (lane < hi_lane)
        kaug_ref[j] = jnp.where(own_lane, k_all, bias_t.T.astype(BF16))
        own_row = (row >= lo_lane) & (row < hi_lane)
        one_row = (row >= bias0) & (row < bias0 + len(parts))
        qaug_ref[j] = jnp.where(own_row, q_t, jnp.where(one_row, 1.0, 0.0)).astype(BF16)
        vt_ref[j] = jnp.concatenate(
            [v_t[lo_lane:hi_lane], jnp.ones((FOX_ONES_ROWS, seq), F32)], axis=0).astype(BF16)

    chains = []
    for qi in range(seq // TQ):
        q0 = qi * TQ
        for j in range(2):
            steps, k0 = [], 0
            while k0 + FOX_WIDE <= q0:
                steps.append((k0, FOX_WIDE, False))
                k0 += FOX_WIDE
            if k0 < q0:
                steps.append((k0, q0 - k0, False))
            steps.append((q0, TQ, True))
            chains.append((q0, j, steps))
    items = []
    for g in range(0, len(chains), FOX_GROUP):
        group = chains[g:g + FOX_GROUP]
        for t in range(max(len(c[2]) for c in group)):
            for q0, j, steps in group:
                if t < len(steps):
                    items.append((q0, j) + steps[t])

    def scores(item):
        q0, j, k0, width, _ = item
        return _dot(kaug_ref[j, k0:k0 + width, :], qaug_ref[j, :, q0:q0 + TQ])

    state, out_t = {}, {}
    pending = [scores(item) for item in items[:FOX_AHEAD]]
    for t, item in enumerate(items):
        q0, j, k0, width, diagonal = item
        s_t = pending.pop(0)
        if t + FOX_AHEAD < len(items):
            pending.append(scores(items[t + FOX_AHEAD]))
        m, acc = state.get((q0, j), (jnp.full((1, TQ), -jnp.inf, F32),
                                     jnp.zeros((FOX_HEAD_DIM + FOX_ONES_ROWS, TQ), F32)))
        if diagonal:
            key = lax.broadcasted_iota(jnp.int32, (width, TQ), 0) + k0
            qry = lax.broadcasted_iota(jnp.int32, (width, TQ), 1) + q0
            s_t = jnp.where(key <= qry, s_t, -jnp.inf)
        m_next = jnp.maximum(m, jnp.max(s_t, axis=0, keepdims=True))
        p_t = jnp.exp2(s_t - m_next)
        acc = acc * jnp.exp2(m - m_next) + _dot(vt_ref[j, :, k0:k0 + width], p_t.astype(BF16))
        state[(q0, j)] = (m_next, acc)
        if diagonal:
            denom = acc[FOX_HEAD_DIM:FOX_HEAD_DIM + 8]
            out_t[(q0, j)] = acc[:FOX_HEAD_DIM] / jnp.concatenate(
                [denom] * (FOX_HEAD_DIM // 8), axis=0)
            if (q0, 1 - j) in out_t:
                both = jnp.concatenate([out_t[(q0, 0)], out_t[(q0, 1)]], axis=0)
                o_ref[0, q0:q0 + TQ, :] = both.T.astype(BF16)


def _fox(fq, fk, fv, c):
    batch, seq, _ = fq.shape
    n_pairs = FOX_WIDTH // HEAD_PAIR
    qkv = pl.BlockSpec((1, seq, HEAD_PAIR), lambda b, p: (b, 0, p))
    return pl.pallas_call(
        functools.partial(_fox_kernel, seq),
        grid=(batch, n_pairs),
        in_specs=[qkv, qkv, qkv, pl.BlockSpec((1, 1, 2, seq), lambda b, p: (b, p, 0, 0))],
        out_specs=qkv,
        out_shape=jax.ShapeDtypeStruct((batch, seq, FOX_WIDTH), BF16),
        scratch_shapes=[pltpu.VMEM((2, seq, HEAD_PAIR), BF16),
                        pltpu.VMEM((2, HEAD_PAIR, seq), BF16),
                        pltpu.VMEM((2, FOX_HEAD_DIM + FOX_ONES_ROWS, seq), BF16)],
        compiler_params=_params(2),
        name="fox",
    )(fq, fk, fv, c)


def _gla_kernel(seq, q_ref, k_ref, v_ref, la_ref, ltri_ref, o_ref):
    n_chunks = TT // GLA_CHUNK
    lane = lax.broadcasted_iota(jnp.int32, (1, HEAD_PAIR), 1)
    head_mask = (lane < GLA_DK, lane >= GLA_DK)
    row_chunk = lax.broadcasted_iota(jnp.int32, (TT, 1), 0) // GLA_CHUNK
    rr = lax.broadcasted_iota(jnp.int32, (TT, TT), 0)
    cc = lax.broadcasted_iota(jnp.int32, (TT, TT), 1)
    intra = (rr >= cc) & (rr // GLA_CHUNK == cc // GLA_CHUNK)
    ltri = ltri_ref[...]

    n_pairs = GLA_HEADS // 2
    n_tiles = seq // TT

    def rows_of(t):
        return slice(t * TT, (t + 1) * TT)

    def cumulate(t):
        hi, lo = _split2(la_ref[rows_of(t), :])
        return _dot(ltri, hi) + _dot(ltri, lo)

    def products(t, b_all):
        rows = rows_of(t)
        out = []
        for p in range(n_pairs):
            ksl = slice(p * HEAD_PAIR, (p + 1) * HEAD_PAIR)
            b = b_all[:, ksl]
            last_rows = [b[(i + 1) * GLA_CHUNK - 1:(i + 1) * GLA_CHUNK, :] for i in range(n_chunks)]
            b_last = jnp.concatenate(
                [jnp.broadcast_to(r, (GLA_CHUNK, HEAD_PAIR)) for r in last_rows], axis=0)
            q_in = (q_ref[rows, ksl] * jnp.exp(b)).astype(BF16)
            k = k_ref[rows, ksl]
            k_in = (k * jnp.exp(-b)).astype(BF16)
            k_st = (k * jnp.exp(b_last - b)).astype(BF16)
            kv = None
            attn = []
            for j in range(2):
                h = 2 * p + j
                vj = v_ref[rows, h * GLA_DV:(h + 1) * GLA_DV]
                kcat = jnp.concatenate(
                    [jnp.where((row_chunk == i) & head_mask[j], k_st, jnp.zeros_like(k_st))
                     for i in range(n_chunks)], axis=1)
                contrib = _dot_tn(vj, kcat)
                kv = contrib if kv is None else kv + contrib
                qh = jnp.where(head_mask[j], q_in, jnp.zeros_like(q_in))
                attn.append(_dot_nt(qh, k_in))
            out.append((q_in, [jnp.exp(r) for r in last_rows], kv, attn))
        return out

    state = [jnp.zeros((GLA_DV, HEAD_PAIR), F32) for _ in range(n_pairs)]

    def finish(t, prods):
        rows = rows_of(t)
        for p in range(n_pairs):
            q_in, decays, kv, attn = prods[p]
            o_inter = []
            for i in range(n_chunks):
                st = state[p].astype(BF16)
                st2 = jnp.concatenate(
                    [jnp.where(head_mask[j], st, jnp.zeros_like(st)) for j in range(2)], axis=0)
                o_inter.append(_dot_nt(q_in[i * GLA_CHUNK:(i + 1) * GLA_CHUNK, :], st2))
                state[p] = state[p] * decays[i] + kv[:, i * HEAD_PAIR:(i + 1) * HEAD_PAIR]
            o_intra = []
            for j in range(2):
                h = 2 * p + j
                vj = v_ref[rows, h * GLA_DV:(h + 1) * GLA_DV]
                o_intra.append(_dot(jnp.where(intra, attn[j], 0.0).astype(BF16), vj))
            o_ref[rows, 2 * p * GLA_DV:(2 * p + 2) * GLA_DV] = (
                jnp.concatenate(o_intra, axis=1) + jnp.concatenate(o_inter, axis=0))

    b_alls, prods = {}, {}
    for step in range(n_tiles + 2):
        if step < n_tiles:
            b_alls[step] = cumulate(step)
        if 0 <= step - 1 < n_tiles:
            prods[step - 1] = products(step - 1, b_alls.pop(step - 1))
        if 0 <= step - 2 < n_tiles:
            finish(step - 2, prods.pop(step - 2))


def _gla(gq, gk, gv, la, seq, ltri):
    t = gq.shape[0]
    row = lambda w: pl.BlockSpec((seq, w), lambda b: (b, 0))
    return pl.pallas_call(
        functools.partial(_gla_kernel, seq),
        grid=(t // seq,),
        in_specs=[row(GLA_KEY_WIDTH), row(GLA_KEY_WIDTH), row(GLA_VAL_WIDTH), row(GLA_KEY_WIDTH),
                  _const_spec((TT, TT))],
        out_specs=row(GLA_VAL_WIDTH),
        out_shape=jax.ShapeDtypeStruct((t, GLA_VAL_WIDTH), F32),
        compiler_params=_params(1),
        name="gla",
    )(gq, gk, gv, la, ltri)


def _outproj_kernel(h_ref, ofox_ref, ogla_ref, gpre_ref, wgate_ref, gnorm_ref, wbf_ref, wbg_ref,
                    wout_ref, gpost_ref, o_ref):
    subs = [slice(s * OUT_SUB, (s + 1) * OUT_SUB) for s in range(OUT_TM // OUT_SUB)]
    u = [_rms(h_ref[rows, :], gpre_ref[...]).astype(BF16) for rows in subs]
    gates = [_dot(us, wgate_ref[...]) for us in u]
    branch_fox = [_dot(ofox_ref[rows, :], wbf_ref[...]) for rows in subs]
    branch_gla = []
    for s, rows in enumerate(subs):
        g_r = gates[s][:, :GLA_VAL_WIDTH]
        heads = []
        for hd in range(GLA_HEADS):
            sl = slice(hd * GLA_DV, (hd + 1) * GLA_DV)
            heads.append(_rms(ogla_ref[rows, sl], gnorm_ref[:, sl]))
        o_gla = jnp.concatenate(heads, axis=1)
        o_gla = (o_gla * (g_r * jax.nn.sigmoid(g_r))).astype(BF16)
        branch_gla.append(_dot(o_gla, wbg_ref[...]))
    merged = []
    for s in range(len(subs)):
        gate_fox = gates[s][:, GLA_VAL_WIDTH:GLA_VAL_WIDTH + D_MODEL]
        gate_gla = gates[s][:, GLA_VAL_WIDTH + D_MODEL:]
        y = jax.nn.sigmoid(gate_fox) * branch_fox[s] + jax.nn.sigmoid(gate_gla) * branch_gla[s]
        merged.append(_dot(y.astype(BF16), wout_ref[...]))
    for s, rows in enumerate(subs):
        o_ref[rows, :] = h_ref[rows, :] + _rms(merged[s], gpost_ref[...])


def _outproj(h2d, o_fox, o_gla, g_pre, wgate, gnorm, wbf, wbg, wout, g_post):
    t = h2d.shape[0]
    row = lambda w: pl.BlockSpec((OUT_TM, w), lambda i: (i, 0))
    return pl.pallas_call(
        _outproj_kernel,
        grid=(t // OUT_TM,),
        in_specs=[row(D_MODEL), row(FOX_WIDTH), row(GLA_VAL_WIDTH), _const_spec((1, D_MODEL)),
                  _const_spec((D_MODEL, GLA_VAL_WIDTH + 2 * D_MODEL)),
                  _const_spec((1, GLA_VAL_WIDTH)),
                  _const_spec((FOX_WIDTH, D_MODEL)), _const_spec((GLA_VAL_WIDTH, D_MODEL)),
                  _const_spec((D_MODEL, D_MODEL)), _const_spec((1, D_MODEL))],
        out_specs=row(D_MODEL),
        out_shape=jax.ShapeDtypeStruct((t, D_MODEL), F32),
        compiler_params=_params(1),
        name="outproj",
    )(h2d, o_fox, o_gla, g_pre, wgate, gnorm, wbf, wbg, wout, g_post)


def _chunk_tril(n):
    r = jnp.arange(n)[:, None]
    c = jnp.arange(n)[None, :]
    return (((r // GLA_CHUNK) == (c // GLA_CHUNK)) & (r >= c)).astype(BF16)


def _layer(h2d, batch, seq, ffn1_pre_g, ffn1_w_gate, ffn1_w_up, ffn1_w_down, ffn1_post_g,
           mix_pre_g, w_in, b_forget, w_alpha_up, b_alpha, gla_norm_g, w_branch_fox,
           w_branch_gla, w_out, mix_post_g, ffn2_pre_g, ffn2_w_gate, ffn2_w_up, ffn2_w_down,
           ffn2_post_g):
    vec = lambda g: g.reshape(1, -1)
    bf = lambda w: w.astype(BF16)

    splits = [FOX_WIDTH, FOX_WIDTH, FOX_WIDTH, FOX_HEADS, GLA_KEY_WIDTH, GLA_KEY_WIDTH,
              GLA_VAL_WIDTH, GLA_GATE_RANK, GLA_VAL_WIDTH, D_MODEL, D_MODEL]
    offs = [0]
    for s in splits:
        offs.append(offs[-1] + s)
    col = lambda a, b: w_in[:, offs[a]:offs[b]]
    wfox = bf(col(0, 3))
    wgla = bf(col(4, 7))
    pad_cols = SMALL_W - FOX_HEADS - GLA_GATE_RANK
    wsm = bf(jnp.concatenate([col(3, 4), col(7, 8), jnp.zeros((D_MODEL, pad_cols), F32)], axis=1))
    bsm = jnp.concatenate([b_forget, jnp.zeros((SMALL_W - FOX_HEADS,), F32)]).reshape(1, SMALL_W)
    wa = bf(jnp.concatenate([jnp.zeros((FOX_HEADS, GLA_KEY_WIDTH), F32), w_alpha_up,
                             jnp.zeros((pad_cols, GLA_KEY_WIDTH), F32)], axis=0))
    wgate = bf(col(8, 11))
    tok = jnp.arange(IN_SUB)
    upper = (tok[:, None] <= tok[None, :]).astype(BF16)
    ltri = _chunk_tril(TT)

    h1 = _ffn(h2d, vec(ffn1_pre_g), bf(ffn1_w_gate), bf(ffn1_w_up), bf(ffn1_w_down),
              vec(ffn1_post_g))

    fq, fk, fv, c, gq, gk, gv, la = _inproj(h1, seq, vec(mix_pre_g), wfox, wgla, wsm, bsm, wa,
                                            vec(b_alpha), upper)
    shape3 = (batch, seq, FOX_WIDTH)
    o_fox = _fox(fq.reshape(shape3), fk.reshape(shape3), fv.reshape(shape3),
                 c.reshape(batch, FOX_HEADS // 2, 2, seq)).reshape(batch * seq, FOX_WIDTH)
    o_gla = _gla(gq, gk, gv, la, seq, ltri)

    h2 = _outproj(h1, o_fox, o_gla, vec(mix_pre_g), wgate, vec(gla_norm_g), bf(w_branch_fox),
                  bf(w_branch_gla), bf(w_out), vec(mix_post_g))

    return _ffn(h2, vec(ffn2_pre_g), bf(ffn2_w_gate), bf(ffn2_w_up), bf(ffn2_w_down),
                vec(ffn2_post_g))


def kernel(x, ffn1_pre_g, ffn1_w_gate, ffn1_w_up, ffn1_w_down, ffn1_post_g, mix_pre_g, w_in,
           b_forget, w_alpha_up, b_alpha, gla_norm_g, w_branch_fox, w_branch_gla, w_out,
           mix_post_g, ffn2_pre_g, ffn2_w_gate, ffn2_w_up, ffn2_w_down, ffn2_post_g):
    batch, seq, d = x.shape
    h = x.reshape(batch * seq, d)
    depth = ffn1_pre_g.shape[0]
    for l in range(depth):
        h = _layer(h, batch, seq, ffn1_pre_g[l], ffn1_w_gate[l], ffn1_w_up[l], ffn1_w_down[l],
                   ffn1_post_g[l], mix_pre_g[l], w_in[l], b_forget[l], w_alpha_up[l], b_alpha[l],
                   gla_norm_g[l], w_branch_fox[l], w_branch_gla[l], w_out[l], mix_post_g[l],
                   ffn2_pre_g[l], ffn2_w_gate[l], ffn2_w_up[l], ffn2_w_down[l], ffn2_post_g[l])
    return h.reshape(batch, seq, d)
```

```python
import functools

import jax
import jax.numpy as jnp
from jax import lax
from jax.experimental import pallas as pl
from jax.experimental.pallas import tpu as pltpu

F32 = jnp.float32
BF16 = jnp.bfloat16

D_MODEL = 1024
D_FF = 2816
FOX_HEADS = 8
FOX_HEAD_DIM = 64
FOX_WIDTH = FOX_HEADS * FOX_HEAD_DIM
GLA_HEADS = 4
GLA_DK = 64
GLA_DV = 128
GLA_KEY_WIDTH = GLA_HEADS * GLA_DK
GLA_VAL_WIDTH = GLA_HEADS * GLA_DV
GLA_GATE_RANK = 16
GLA_TAU = 16.0
GLA_CHUNK = 64
NORM_EPS = 1e-6

LANES = 128
HEAD_PAIR = LANES
VMEM_LIMIT_BYTES = 56 * 1024 * 1024

IN_TM = 1024
IN_SUB = 512
FFN_TM = 1024
FF_CHUNK = 256
OUT_TM = 1024
OUT_SUB = 512
FFN_SUB = 512
TQ = 256
FOX_WIDE = 512
FOX_ONES_ROWS = 16
FOX_GROUP = 8
FOX_AHEAD = 4
LOG2_E = 1.4426950408889634
TT = 256
SMALL_W = LANES
CUM_ROWS = 16


def _rms(x, g):
    return x * lax.rsqrt(jnp.mean(x * x, axis=-1, keepdims=True) + NORM_EPS) * g


def _log_sigmoid(x):
    return jnp.minimum(x, 0.0) - jnp.log1p(jnp.exp(-jnp.abs(x)))


def _dot(a, b):
    return jnp.dot(a, b, preferred_element_type=F32)


def _dot_nt(a, b):
    return lax.dot_general(a, b, (((1,), (1,)), ((), ())), preferred_element_type=F32)


def _dot_tn(a, b):
    return lax.dot_general(a, b, (((0,), (0,)), ((), ())), preferred_element_type=F32)


def _split2(x):
    hi = x.astype(BF16)
    lo = (x - hi.astype(F32)).astype(BF16)
    return hi, lo


def _split3(x):
    hi = x.astype(BF16)
    r = x - hi.astype(F32)
    mid = r.astype(BF16)
    lo = (r - mid.astype(F32)).astype(BF16)
    return hi, mid, lo


def _const_spec(shape):
    return pl.BlockSpec(shape, lambda *_: (0,) * len(shape), pipeline_mode=pl.Buffered(1))


def _params(n_axes):
    return pltpu.CompilerParams(dimension_semantics=("arbitrary",) * n_axes,
                                vmem_limit_bytes=VMEM_LIMIT_BYTES)


def _ffn_kernel(x_ref, gpre_ref, wg_ref, wu_ref, wd_ref, gpost_ref, o_ref):
    subs = [slice(s * FFN_SUB, (s + 1) * FFN_SUB) for s in range(FFN_TM // FFN_SUB)]
    xn = [_rms(x_ref[rows, :], gpre_ref[...]).astype(BF16) for rows in subs]
    acc = [None] * len(subs)
    for c in range(D_FF // FF_CHUNK):
        cols = slice(c * FF_CHUNK, (c + 1) * FF_CHUNK)
        for s in range(len(subs)):
            g = _dot(xn[s], wg_ref[:, cols])
            u = _dot(xn[s], wu_ref[:, cols])
            h = (g * jax.nn.sigmoid(g) * u).astype(BF16)
            part = _dot(h, wd_ref[cols, :])
            acc[s] = part if acc[s] is None else acc[s] + part
    for s, rows in enumerate(subs):
        o_ref[rows, :] = x_ref[rows, :] + 0.5 * _rms(acc[s], gpost_ref[...])


def _ffn(x2d, g_pre, w_gate, w_up, w_down, g_post):
    t = x2d.shape[0]
    row = pl.BlockSpec((FFN_TM, D_MODEL), lambda i: (i, 0))
    return pl.pallas_call(
        _ffn_kernel,
        grid=(t // FFN_TM,),
        in_specs=[row, _const_spec((1, D_MODEL)), _const_spec((D_MODEL, D_FF)),
                  _const_spec((D_MODEL, D_FF)), _const_spec((D_FF, D_MODEL)),
                  _const_spec((1, D_MODEL))],
        out_specs=row,
        out_shape=jax.ShapeDtypeStruct((t, D_MODEL), F32),
        compiler_params=_params(1),
        name="ffn",
    )(x2d, g_pre, w_gate, w_up, w_down, g_post)


def _inproj_kernel(tiles_per_seq, h_ref, g_ref, wfox_ref, wgla_ref, wsm_ref, bsm_ref, wa_ref,
                   ba_ref, upper_ref, fq_ref, fk_ref, fv_ref, c_ref, gq_ref, gk_ref, gv_ref,
                   la_ref, carry_ref):
    @pl.when(pl.program_id(0) % tiles_per_seq == 0)
    def _():
        carry_ref[...] = jnp.zeros_like(carry_ref)

    subs = [slice(s * IN_SUB, (s + 1) * IN_SUB) for s in range(IN_TM // IN_SUB)]
    scale = FOX_HEAD_DIM ** -0.5 * LOG2_E
    u = [_rms(h_ref[rows, :], g_ref[...]).astype(BF16) for rows in subs]

    zs = [_dot_nt(us, wsm_ref[...]) for us in u]

    for s, rows in enumerate(subs):
        zf = _dot_nt(u[s], wfox_ref[...])
        fq_ref[rows, :] = (zf[:, :FOX_WIDTH] * scale).astype(BF16)
        fk_ref[rows, :] = zf[:, FOX_WIDTH:2 * FOX_WIDTH].astype(BF16)
        fv_ref[rows, :] = zf[:, 2 * FOX_WIDTH:].astype(BF16)

    for s, rows in enumerate(subs):
        a = _dot(zs[s].astype(BF16), wa_ref[...]) + ba_ref[...]
        la_ref[rows, :] = _log_sigmoid(a) * (1.0 / GLA_TAU)

    for s, rows in enumerate(subs):
        zg = _dot_nt(u[s], wgla_ref[...])
        gq_ref[rows, :] = zg[:, :GLA_KEY_WIDTH] * (GLA_DK ** -0.5)
        gk_ref[rows, :] = zg[:, GLA_KEY_WIDTH:2 * GLA_KEY_WIDTH]
        gv_ref[rows, :] = zg[:, 2 * GLA_KEY_WIDTH:].astype(BF16)

    upper = upper_ref[...]
    carry = carry_ref[...]
    n_blocks = IN_SUB // LANES
    for s, rows in enumerate(subs):
        lf_t = _log_sigmoid(zs[s] + bsm_ref[...]).T[:CUM_ROWS]
        parts = jnp.concatenate(_split3(lf_t), axis=0)
        stacked = jnp.concatenate(
            [parts[:, k * LANES:(k + 1) * LANES] for k in range(n_blocks)], axis=0)
        sums = _dot(stacked, upper)
        c_blocks = []
        for k in range(n_blocks):
            blk = sums[k * 3 * CUM_ROWS:(k + 1) * 3 * CUM_ROWS]
            c_blocks.append(blk[:CUM_ROWS] + blk[CUM_ROWS:2 * CUM_ROWS] + blk[2 * CUM_ROWS:] + carry)
            carry = carry + jnp.sum(lf_t[:, k * LANES:(k + 1) * LANES], axis=1, keepdims=True)
        c_ref[0, :, rows] = jnp.concatenate(c_blocks, axis=1)[:FOX_HEADS] * LOG2_E
    carry_ref[...] = carry


def _inproj(h2d, seq, g, wfox, wgla, wsm, bsm, wa, ba, upper):
    t = h2d.shape[0]
    batch = t // seq
    tiles_per_seq = seq // IN_TM
    row = lambda w: pl.BlockSpec((IN_TM, w), lambda i: (i, 0))
    out_shapes = (
        jax.ShapeDtypeStruct((t, FOX_WIDTH), BF16),
        jax.ShapeDtypeStruct((t, FOX_WIDTH), BF16),
        jax.ShapeDtypeStruct((t, FOX_WIDTH), BF16),
        jax.ShapeDtypeStruct((batch, FOX_HEADS, seq), F32),
        jax.ShapeDtypeStruct((t, GLA_KEY_WIDTH), F32),
        jax.ShapeDtypeStruct((t, GLA_KEY_WIDTH), F32),
        jax.ShapeDtypeStruct((t, GLA_VAL_WIDTH), BF16),
        jax.ShapeDtypeStruct((t, GLA_KEY_WIDTH), F32),
    )
    out_specs = (
        row(FOX_WIDTH), row(FOX_WIDTH), row(FOX_WIDTH),
        pl.BlockSpec((1, FOX_HEADS, IN_TM), lambda i: (i // tiles_per_seq, 0, i % tiles_per_seq)),
        row(GLA_KEY_WIDTH), row(GLA_KEY_WIDTH), row(GLA_VAL_WIDTH), row(GLA_KEY_WIDTH),
    )
    return pl.pallas_call(
        functools.partial(_inproj_kernel, tiles_per_seq),
        grid=(t // IN_TM,),
        in_specs=[row(D_MODEL), _const_spec((1, D_MODEL)),
                  _const_spec((3 * FOX_WIDTH, D_MODEL)),
                  _const_spec((2 * GLA_KEY_WIDTH + GLA_VAL_WIDTH, D_MODEL)),
                  _const_spec((SMALL_W, D_MODEL)), _const_spec((1, SMALL_W)),
                  _const_spec((SMALL_W, GLA_KEY_WIDTH)), _const_spec((1, GLA_KEY_WIDTH)),
                  _const_spec((LANES, LANES))],
        out_specs=out_specs,
        out_shape=out_shapes,
        scratch_shapes=[pltpu.VMEM((CUM_ROWS, LANES), F32)],
        compiler_params=_params(1),
        name="inproj",
    )(h2d, g, wfox, wgla, wsm, bsm, wa, ba, upper)


def _fox_kernel(seq, q_ref, k_ref, v_ref, c_ref, o_ref, kaug_ref, qaug_ref, vt_ref):
    lane = lax.broadcasted_iota(jnp.int32, (1, HEAD_PAIR), 1)
    row = lax.broadcasted_iota(jnp.int32, (HEAD_PAIR, 1), 0)
    k_all = k_ref[0]
    q_t = q_ref[0].astype(F32).T
    v_t = v_ref[0].astype(F32).T
    for j in range(2):
        lo_lane, hi_lane = j * FOX_HEAD_DIM, (j + 1) * FOX_HEAD_DIM
        bias0 = (1 - j) * FOX_HEAD_DIM
        head = 2 * pl.program_id(1) + j
        parts = [p.astype(F32) for p in _split3(-c_ref[0, pl.ds(head, 1), :])]
        bias_t = jnp.zeros((HEAD_PAIR, seq), F32)
        for i, part in enumerate(parts):
            bias_t = jnp.where(row == bias0 + i, part, bias_t)
        own_lane = (lane >= lo_lane) & (lane < hi_lane)
        kaug_ref[j] = jnp.where(own_lane, k_all, bias_t.T.astype(BF16))
        own_row = (row >= lo_lane) & (row < hi_lane)
        one_row = (row >= bias0) & (row < bias0 + len(parts))
        qaug_ref[j] = jnp.where(own_row, q_t, jnp.where(one_row, 1.0, 0.0)).astype(BF16)
        vt_ref[j] = jnp.concatenate(
            [v_t[lo_lane:hi_lane], jnp.ones((FOX_ONES_ROWS, seq), F32)], axis=0).astype(BF16)

    chains = []
    for qi in range(seq // TQ):
        q0 = qi * TQ
        for j in range(2):
            steps, k0 = [], 0
            while k0 + FOX_WIDE <= q0:
                steps.append((k0, FOX_WIDE, False))
                k0 += FOX_WIDE
            if k0 < q0:
                steps.append((k0, q0 - k0, False))
            steps.append((q0, TQ, True))
            chains.append((q0, j, steps))
    items = []
    for g in range(0, len(chains), FOX_GROUP):
        group = chains[g:g + FOX_GROUP]
        for t in range(max(len(c[2]) for c in group)):
            for q0, j, steps in group:
                if t < len(steps):
                    items.append((q0, j) + steps[t])

    def scores(item):
        q0, j, k0, width, _ = item
        return _dot(kaug_ref[j, k0:k0 + width, :], qaug_ref[j, :, q0:q0 + TQ])

    state, out_t = {}, {}
    pending = [scores(item) for item in items[:FOX_AHEAD]]
    for t, item in enumerate(items):
        q0, j, k0, width, diagonal = item
        s_t = pending.pop(0)
        if t + FOX_AHEAD < len(items):
            pending.append(scores(items[t + FOX_AHEAD]))
        m, acc = state.get((q0, j), (jnp.full((1, TQ), -jnp.inf, F32),
                                     jnp.zeros((FOX_HEAD_DIM + FOX_ONES_ROWS, TQ), F32)))
        if diagonal:
            key = lax.broadcasted_iota(jnp.int32, (width, TQ), 0) + k0
            qry = lax.broadcasted_iota(jnp.int32, (width, TQ), 1) + q0
            s_t = jnp.where(key <= qry, s_t, -jnp.inf)
        m_next = jnp.maximum(m, jnp.max(s_t, axis=0, keepdims=True))
        p_t = jnp.exp2(s_t - m_next)
        acc = acc * jnp.exp2(m - m_next) + _dot(vt_ref[j, :, k0:k0 + width], p_t.astype(BF16))
        state[(q0, j)] = (m_next, acc)
        if diagonal:
            denom = acc[FOX_HEAD_DIM:FOX_HEAD_DIM + 8]
            out_t[(q0, j)] = acc[:FOX_HEAD_DIM] / jnp.concatenate(
                [denom] * (FOX_HEAD_DIM // 8), axis=0)
            if (q0, 1 - j) in out_t:
                both = jnp.concatenate([out_t[(q0, 0)], out_t[(q0, 1)]], axis=0)
                o_ref[0, q0:q0 + TQ, :] = both.T.astype(BF16)


def _fox(fq, fk, fv, c):
    batch, seq, _ = fq.shape
    n_pairs = FOX_WIDTH // HEAD_PAIR
    qkv = pl.BlockSpec((1, seq, HEAD_PAIR), lambda b, p: (b, 0, p))
    return pl.pallas_call(
        functools.partial(_fox_kernel, seq),
        grid=(batch, n_pairs),
        in_specs=[qkv, qkv, qkv, pl.BlockSpec((1, FOX_HEADS, seq), lambda b, p: (b, 0, 0))],
        out_specs=qkv,
        out_shape=jax.ShapeDtypeStruct((batch, seq, FOX_WIDTH), BF16),
        scratch_shapes=[pltpu.VMEM((2, seq, HEAD_PAIR), BF16),
                        pltpu.VMEM((2, HEAD_PAIR, seq), BF16),
                        pltpu.VMEM((2, FOX_HEAD_DIM + FOX_ONES_ROWS, seq), BF16)],
        compiler_params=_params(2),
        name="fox",
    )(fq, fk, fv, c)


def _gla_kernel(seq, q_ref, k_ref, v_ref, la_ref, ltri_ref, o_ref):
    n_chunks = TT // GLA_CHUNK
    lane = lax.broadcasted_iota(jnp.int32, (1, HEAD_PAIR), 1)
    head_mask = (lane < GLA_DK, lane >= GLA_DK)
    row_chunk = lax.broadcasted_iota(jnp.int32, (TT, 1), 0) // GLA_CHUNK
    rr = lax.broadcasted_iota(jnp.int32, (TT, TT), 0)
    cc = lax.broadcasted_iota(jnp.int32, (TT, TT), 1)
    intra = (rr >= cc) & (rr // GLA_CHUNK == cc // GLA_CHUNK)
    ltri = ltri_ref[...]

    n_pairs = GLA_HEADS // 2
    n_tiles = seq // TT

    def rows_of(t):
        return slice(t * TT, (t + 1) * TT)

    def cumulate(t):
        hi, lo = _split2(la_ref[rows_of(t), :])
        return _dot(ltri, hi) + _dot(ltri, lo)

    def products(t, b_all):
        rows = rows_of(t)
        out = []
        for p in range(n_pairs):
            ksl = slice(p * HEAD_PAIR, (p + 1) * HEAD_PAIR)
            b = b_all[:, ksl]
            last_rows = [b[(i + 1) * GLA_CHUNK - 1:(i + 1) * GLA_CHUNK, :] for i in range(n_chunks)]
            b_last = jnp.concatenate(
                [jnp.broadcast_to(r, (GLA_CHUNK, HEAD_PAIR)) for r in last_rows], axis=0)
            q_in = (q_ref[rows, ksl] * jnp.exp(b)).astype(BF16)
            k = k_ref[rows, ksl]
            k_in = (k * jnp.exp(-b)).astype(BF16)
            k_st = (k * jnp.exp(b_last - b)).astype(BF16)
            kv = None
            attn = []
            for j in range(2):
                h = 2 * p + j
                vj = v_ref[rows, h * GLA_DV:(h + 1) * GLA_DV]
                kcat = jnp.concatenate(
                    [jnp.where((row_chunk == i) & head_mask[j], k_st, jnp.zeros_like(k_st))
                     for i in range(n_chunks)], axis=1)
                contrib = _dot_tn(vj, kcat)
                kv = contrib if kv is None else kv + contrib
                qh = jnp.where(head_mask[j], q_in, jnp.zeros_like(q_in))
                attn.append(_dot_nt(qh, k_in))
            out.append((q_in, [jnp.exp(r) for r in last_rows], kv, attn))
        return out

    state = [jnp.zeros((GLA_DV, HEAD_PAIR), F32) for _ in range(n_pairs)]

    def finish(t, prods):
        rows = rows_of(t)
        for p in range(n_pairs):
            q_in, decays, kv, attn = prods[p]
            o_inter = []
            for i in range(n_chunks):
                st = state[p].astype(BF16)
                st2 = jnp.concatenate(
                    [jnp.where(head_mask[j], st, jnp.zeros_like(st)) for j in range(2)], axis=0)
                o_inter.append(_dot_nt(q_in[i * GLA_CHUNK:(i + 1) * GLA_CHUNK, :], st2))
                state[p] = state[p] * decays[i] + kv[:, i * HEAD_PAIR:(i + 1) * HEAD_PAIR]
            o_intra = []
            for j in range(2):
                h = 2 * p + j
                vj = v_ref[rows, h * GLA_DV:(h + 1) * GLA_DV]
                o_intra.append(_dot(jnp.where(intra, attn[j], 0.0).astype(BF16), vj))
            o_ref[rows, 2 * p * GLA_DV:(2 * p + 2) * GLA_DV] = (
                jnp.concatenate(o_intra, axis=1) + jnp.concatenate(o_inter, axis=0))

    b_alls, prods = {}, {}
    for step in range(n_tiles + 2):
        if step < n_tiles:
            b_alls[step] = cumulate(step)
        if 0 <= step - 1 < n_tiles:
            prods[step - 1] = products(step - 1, b_alls.pop(step - 1))
        if 0 <= step - 2 < n_tiles:
            finish(step - 2, prods.pop(step - 2))


def _gla(gq, gk, gv, la, seq, ltri):
    t = gq.shape[0]
    row = lambda w: pl.BlockSpec((seq, w), lambda b: (b, 0))
    return pl.pallas_call(
        functools.partial(_gla_kernel, seq),
        grid=(t // seq,),
        in_specs=[row(GLA_KEY_WIDTH), row(GLA_KEY_WIDTH), row(GLA_VAL_WIDTH), row(GLA_KEY_WIDTH),
                  _const_spec((TT, TT))],
        out_specs=row(GLA_VAL_WIDTH),
        out_shape=jax.ShapeDtypeStruct((t, GLA_VAL_WIDTH), F32),
        compiler_params=_params(1),
        name="gla",
    )(gq, gk, gv, la, ltri)


def _outproj_kernel(h_ref, ofox_ref, ogla_ref, gpre_ref, wgate_ref, gnorm_ref, wbf_ref, wbg_ref,
                    wout_ref, gpost_ref, o_ref):
    subs = [slice(s * OUT_SUB, (s + 1) * OUT_SUB) for s in range(OUT_TM // OUT_SUB)]
    u = [_rms(h_ref[rows, :], gpre_ref[...]).astype(BF16) for rows in subs]
    gates = [_dot_nt(us, wgate_ref[...]) for us in u]
    branch_fox = [_dot(ofox_ref[rows, :], wbf_ref[...]) for rows in subs]
    branch_gla = []
    for s, rows in enumerate(subs):
        g_r = gates[s][:, :GLA_VAL_WIDTH]
        heads = []
        for hd in range(GLA_HEADS):
            sl = slice(hd * GLA_DV, (hd + 1) * GLA_DV)
            heads.append(_rms(ogla_ref[rows, sl], gnorm_ref[:, sl]))
        o_gla = jnp.concatenate(heads, axis=1)
        o_gla = (o_gla * (g_r * jax.nn.sigmoid(g_r))).astype(BF16)
        branch_gla.append(_dot(o_gla, wbg_ref[...]))
    merged = []
    for s in range(len(subs)):
        gate_fox = gates[s][:, GLA_VAL_WIDTH:GLA_VAL_WIDTH + D_MODEL]
        gate_gla = gates[s][:, GLA_VAL_WIDTH + D_MODEL:]
        y = jax.nn.sigmoid(gate_fox) * branch_fox[s] + jax.nn.sigmoid(gate_gla) * branch_gla[s]
        merged.append(_dot(y.astype(BF16), wout_ref[...]))
    for s, rows in enumerate(subs):
        o_ref[rows, :] = h_ref[rows, :] + _rms(merged[s], gpost_ref[...])


def _outproj(h2d, o_fox, o_gla, g_pre, wgate, gnorm, wbf, wbg, wout, g_post):
    t = h2d.shape[0]
    row = lambda w: pl.BlockSpec((OUT_TM, w), lambda i: (i, 0))
    return pl.pallas_call(
        _outproj_kernel,
        grid=(t // OUT_TM,),
        in_specs=[row(D_MODEL), row(FOX_WIDTH), row(GLA_VAL_WIDTH), _const_spec((1, D_MODEL)),
                  _const_spec((GLA_VAL_WIDTH + 2 * D_MODEL, D_MODEL)),
                  _const_spec((1, GLA_VAL_WIDTH)),
                  _const_spec((FOX_WIDTH, D_MODEL)), _const_spec((GLA_VAL_WIDTH, D_MODEL)),
                  _const_spec((D_MODEL, D_MODEL)), _const_spec((1, D_MODEL))],
        out_specs=row(D_MODEL),
        out_shape=jax.ShapeDtypeStruct((t, D_MODEL), F32),
        compiler_params=_params(1),
        name="outproj",
    )(h2d, o_fox, o_gla, g_pre, wgate, gnorm, wbf, wbg, wout, g_post)


def _chunk_tril(n):
    r = jnp.arange(n)[:, None]
    c = jnp.arange(n)[None, :]
    return (((r // GLA_CHUNK) == (c // GLA_CHUNK)) & (r >= c)).astype(BF16)


def _layer(h2d, batch, seq, ffn1_pre_g, ffn1_w_gate, ffn1_w_up, ffn1_w_down, ffn1_post_g,
           mix_pre_g, w_in, b_forget, w_alpha_up, b_alpha, gla_norm_g, w_branch_fox,
           w_branch_gla, w_out, mix_post_g, ffn2_pre_g, ffn2_w_gate, ffn2_w_up, ffn2_w_down,
           ffn2_post_g):
    vec = lambda g: g.reshape(1, -1)
    bf = lambda w: w.astype(BF16)

    splits = [FOX_WIDTH, FOX_WIDTH, FOX_WIDTH, FOX_HEADS, GLA_KEY_WIDTH, GLA_KEY_WIDTH,
              GLA_VAL_WIDTH, GLA_GATE_RANK, GLA_VAL_WIDTH, D_MODEL, D_MODEL]
    offs = [0]
    for s in splits:
        offs.append(offs[-1] + s)
    w_in_t = w_in.T
    col = lambda a, b: w_in_t[offs[a]:offs[b], :]
    wfox = bf(col(0, 3))
    wgla = bf(col(4, 7))
    pad_cols = SMALL_W - FOX_HEADS - GLA_GATE_RANK
    wsm = bf(jnp.concatenate([col(3, 4), col(7, 8), jnp.zeros((pad_cols, D_MODEL), F32)], axis=0))
    bsm = jnp.concatenate([b_forget, jnp.zeros((SMALL_W - FOX_HEADS,), F32)]).reshape(1, SMALL_W)
    wa = bf(jnp.concatenate([jnp.zeros((FOX_HEADS, GLA_KEY_WIDTH), F32), w_alpha_up,
                             jnp.zeros((pad_cols, GLA_KEY_WIDTH), F32)], axis=0))
    wgate = bf(col(8, 11))
    tok = jnp.arange(LANES)
    upper = (tok[:, None] <= tok[None, :]).astype(BF16)
    ltri = _chunk_tril(TT)

    h1 = _ffn(h2d, vec(ffn1_pre_g), bf(ffn1_w_gate), bf(ffn1_w_up), bf(ffn1_w_down),
              vec(ffn1_post_g))

    fq, fk, fv, c, gq, gk, gv, la = _inproj(h1, seq, vec(mix_pre_g), wfox, wgla, wsm, bsm, wa,
                                            vec(b_alpha), upper)
    shape3 = (batch, seq, FOX_WIDTH)
    o_fox = _fox(fq.reshape(shape3), fk.reshape(shape3), fv.reshape(shape3),
                 c).reshape(batch * seq, FOX_WIDTH)
    o_gla = _gla(gq, gk, gv, la, seq, ltri)

    h2 = _outproj(h1, o_fox, o_gla, vec(mix_pre_g), wgate, vec(gla_norm_g), bf(w_branch_fox),
                  bf(w_branch_gla), bf(w_out), vec(mix_post_g))

    return _ffn(h2, vec(ffn2_pre_g), bf(ffn2_w_gate), bf(ffn2_w_up), bf(ffn2_w_down),
                vec(ffn2_post_g))


def kernel(x, ffn1_pre_g, ffn1_w_gate, ffn1_w_up, ffn1_w_down, ffn1_post_g, mix_pre_g, w_in,
           b_forget, w_alpha_up, b_alpha, gla_norm_g, w_branch_fox, w_branch_gla, w_out,
           mix_post_g, ffn2_pre_g, ffn2_w_gate, ffn2_w_up, ffn2_w_down, ffn2_post_g):
    batch, seq, d = x.shape
    h = x.reshape(batch * seq, d)
    depth = ffn1_pre_g.shape[0]
    for l in range(depth):
        h = _layer(h, batch, seq, ffn1_pre_g[l], ffn1_w_gate[l], ffn1_w_up[l], ffn1_w_down[l],
                   ffn1_post_g[l], mix_pre_g[l], w_in[l], b_forget[l], w_alpha_up[l], b_alpha[l],
                   gla_norm_g[l], w_branch_fox[l], w_branch_gla[l], w_out[l], mix_post_g[l],
                   ffn2_pre_g[l], ffn2_w_gate[l], ffn2_w_up[l], ffn2_w_down[l], ffn2_post_g[l])
    return h.reshape(batch, seq, d)
```

```python
import functools

import jax
import jax.numpy as jnp
from jax import lax
from jax.experimental import pallas as pl
from jax.experimental.pallas import tpu as pltpu

F32 = jnp.float32
BF16 = jnp.bfloat16

D_MODEL = 1024
D_FF = 2816
FOX_HEADS = 8
FOX_HEAD_DIM = 64
FOX_WIDTH = FOX_HEADS * FOX_HEAD_DIM
GLA_HEADS = 4
GLA_DK = 64
GLA_DV = 128
GLA_KEY_WIDTH = GLA_HEADS * GLA_DK
GLA_VAL_WIDTH = GLA_HEADS * GLA_DV
GLA_GATE_RANK = 16
GLA_TAU = 16.0
GLA_CHUNK = 64
NORM_EPS = 1e-6

LANES = 128
HEAD_PAIR = LANES
VMEM_LIMIT_BYTES = 56 * 1024 * 1024

IN_TM = 1024
IN_SUB = 512
FFN_TM = 1024
FF_CHUNK = 256
OUT_TM = 1024
OUT_SUB = 512
FFN_SUB = 512
TQ = 256
FOX_WIDE = 512
FOX_ONES_ROWS = 16
FOX_GROUP = 8
FOX_AHEAD = 4
LOG2_E = 1.4426950408889634
TT = 256
SMALL_W = LANES
CUM_ROWS = 16


def _rms(x, g):
    return x * lax.rsqrt(jnp.mean(x * x, axis=-1, keepdims=True) + NORM_EPS) * g


def _log_sigmoid(x):
    return jnp.minimum(x, 0.0) - jnp.log1p(jnp.exp(-jnp.abs(x)))


def _dot(a, b):
    return jnp.dot(a, b, preferred_element_type=F32)


def _dot_nt(a, b):
    return lax.dot_general(a, b, (((1,), (1,)), ((), ())), preferred_element_type=F32)


def _dot_tn(a, b):
    return lax.dot_general(a, b, (((0,), (0,)), ((), ())), preferred_element_type=F32)


def _split2(x):
    hi = x.astype(BF16)
    lo = (x - hi.astype(F32)).astype(BF16)
    return hi, lo


def _split3(x):
    hi = x.astype(BF16)
    r = x - hi.astype(F32)
    mid = r.astype(BF16)
    lo = (r - mid.astype(F32)).astype(BF16)
    return hi, mid, lo


def _const_spec(shape):
    return pl.BlockSpec(shape, lambda *_: (0,) * len(shape), pipeline_mode=pl.Buffered(1))


def _params(n_axes):
    return pltpu.CompilerParams(dimension_semantics=("arbitrary",) * n_axes,
                                vmem_limit_bytes=VMEM_LIMIT_BYTES)


def _ffn_kernel(n_cast, x_ref, gpre_ref, wg_ref, wu_ref, wd_ref, gpost_ref, *rest):
    cast_in, o_ref, cast_out = rest[:n_cast], rest[n_cast], rest[n_cast + 1:]
    subs = [slice(s * FFN_SUB, (s + 1) * FFN_SUB) for s in range(FFN_TM // FFN_SUB)]
    xn = [_rms(x_ref[rows, :], gpre_ref[...]).astype(BF16) for rows in subs]
    acc = [None] * len(subs)
    for c in range(D_FF // FF_CHUNK):
        cols = slice(c * FF_CHUNK, (c + 1) * FF_CHUNK)
        for s in range(len(subs)):
            g = _dot(xn[s], wg_ref[:, cols])
            u = _dot(xn[s], wu_ref[:, cols])
            h = (g * jax.nn.sigmoid(g) * u).astype(BF16)
            part = _dot(h, wd_ref[cols, :])
            acc[s] = part if acc[s] is None else acc[s] + part
    for s, rows in enumerate(subs):
        o_ref[rows, :] = x_ref[rows, :] + 0.5 * _rms(acc[s], gpost_ref[...])
    for src, dst in zip(cast_in, cast_out):
        dst[...] = src[...].astype(BF16)


def _cast_spec(shape, steps):
    rows, cols = shape
    span = 1 if (rows // steps) % 16 == 0 else 2
    assert rows % steps == 0 and (span * rows // steps) % 16 == 0
    return pl.BlockSpec((span * rows // steps, cols), lambda i: (i // span, 0))


def _ffn(x2d, g_pre, w_gate, w_up, w_down, g_post, cast=()):
    t = x2d.shape[0]
    steps = t // FFN_TM
    row = pl.BlockSpec((FFN_TM, D_MODEL), lambda i: (i, 0))
    cast_specs = [_cast_spec(w.shape, steps) for w in cast]
    out = pl.pallas_call(
        functools.partial(_ffn_kernel, len(cast)),
        grid=(steps,),
        in_specs=[row, _const_spec((1, D_MODEL)), _const_spec((D_MODEL, D_FF)),
                  _const_spec((D_MODEL, D_FF)), _const_spec((D_FF, D_MODEL)),
                  _const_spec((1, D_MODEL))] + cast_specs,
        out_specs=[row] + cast_specs,
        out_shape=[jax.ShapeDtypeStruct((t, D_MODEL), F32)]
        + [jax.ShapeDtypeStruct(w.shape, BF16) for w in cast],
        compiler_params=_params(1),
        name="ffn",
    )(x2d, g_pre, w_gate, w_up, w_down, g_post, *cast)
    return out[0], tuple(out[1:])


def _inproj_kernel(tiles_per_seq, h_ref, g_ref, wfox_ref, wgla_ref, wsm_ref, bsm_ref, wa_ref,
                   ba_ref, upper_ref, fq_ref, fk_ref, fv_ref, c_ref, gq_ref, gk_ref, gv_ref,
                   la_ref, carry_ref):
    @pl.when(pl.program_id(0) % tiles_per_seq == 0)
    def _():
        carry_ref[...] = jnp.zeros_like(carry_ref)

    subs = [slice(s * IN_SUB, (s + 1) * IN_SUB) for s in range(IN_TM // IN_SUB)]
    scale = FOX_HEAD_DIM ** -0.5 * LOG2_E
    u = [_rms(h_ref[rows, :], g_ref[...]).astype(BF16) for rows in subs]

    zs = [_dot_nt(us, wsm_ref[...]) for us in u]

    for s, rows in enumerate(subs):
        zf = _dot_nt(u[s], wfox_ref[...])
        fq_ref[rows, :] = (zf[:, :FOX_WIDTH] * scale).astype(BF16)
        fk_ref[rows, :] = zf[:, FOX_WIDTH:2 * FOX_WIDTH].astype(BF16)
        fv_ref[rows, :] = zf[:, 2 * FOX_WIDTH:].astype(BF16)

    for s, rows in enumerate(subs):
        a = _dot(zs[s].astype(BF16), wa_ref[...]) + ba_ref[...]
        la_ref[rows, :] = _log_sigmoid(a) * (1.0 / GLA_TAU)

    for s, rows in enumerate(subs):
        zg = _dot_nt(u[s], wgla_ref[...])
        gq_ref[rows, :] = zg[:, :GLA_KEY_WIDTH] * (GLA_DK ** -0.5)
        gk_ref[rows, :] = zg[:, GLA_KEY_WIDTH:2 * GLA_KEY_WIDTH]
        gv_ref[rows, :] = zg[:, 2 * GLA_KEY_WIDTH:].astype(BF16)

    upper = upper_ref[...]
    carry = carry_ref[...]
    n_blocks = IN_SUB // LANES
    for s, rows in enumerate(subs):
        lf_t = _log_sigmoid(zs[s] + bsm_ref[...]).T[:CUM_ROWS]
        parts = jnp.concatenate(_split3(lf_t), axis=0)
        stacked = jnp.concatenate(
            [parts[:, k * LANES:(k + 1) * LANES] for k in range(n_blocks)], axis=0)
        sums = _dot(stacked, upper)
        c_blocks = []
        for k in range(n_blocks):
            blk = sums[k * 3 * CUM_ROWS:(k + 1) * 3 * CUM_ROWS]
            c_blocks.append(blk[:CUM_ROWS] + blk[CUM_ROWS:2 * CUM_ROWS] + blk[2 * CUM_ROWS:] + carry)
            carry = carry + jnp.sum(lf_t[:, k * LANES:(k + 1) * LANES], axis=1, keepdims=True)
        c_ref[0, :, rows] = jnp.concatenate(c_blocks, axis=1)[:FOX_HEADS] * LOG2_E
    carry_ref[...] = carry


def _inproj(h2d, seq, g, wfox, wgla, wsm, bsm, wa, ba, upper):
    t = h2d.shape[0]
    batch = t // seq
    tiles_per_seq = seq // IN_TM
    row = lambda w: pl.BlockSpec((IN_TM, w), lambda i: (i, 0))
    out_shapes = (
        jax.ShapeDtypeStruct((t, FOX_WIDTH), BF16),
        jax.ShapeDtypeStruct((t, FOX_WIDTH), BF16),
        jax.ShapeDtypeStruct((t, FOX_WIDTH), BF16),
        jax.ShapeDtypeStruct((batch, FOX_HEADS, seq), F32),
        jax.ShapeDtypeStruct((t, GLA_KEY_WIDTH), F32),
        jax.ShapeDtypeStruct((t, GLA_KEY_WIDTH), F32),
        jax.ShapeDtypeStruct((t, GLA_VAL_WIDTH), BF16),
        jax.ShapeDtypeStruct((t, GLA_KEY_WIDTH), F32),
    )
    out_specs = (
        row(FOX_WIDTH), row(FOX_WIDTH), row(FOX_WIDTH),
        pl.BlockSpec((1, FOX_HEADS, IN_TM), lambda i: (i // tiles_per_seq, 0, i % tiles_per_seq)),
        row(GLA_KEY_WIDTH), row(GLA_KEY_WIDTH), row(GLA_VAL_WIDTH), row(GLA_KEY_WIDTH),
    )
    return pl.pallas_call(
        functools.partial(_inproj_kernel, tiles_per_seq),
        grid=(t // IN_TM,),
        in_specs=[row(D_MODEL), _const_spec((1, D_MODEL)),
                  _const_spec((3 * FOX_WIDTH, D_MODEL)),
                  _const_spec((2 * GLA_KEY_WIDTH + GLA_VAL_WIDTH, D_MODEL)),
                  _const_spec((SMALL_W, D_MODEL)), _const_spec((1, SMALL_W)),
                  _const_spec((SMALL_W, GLA_KEY_WIDTH)), _const_spec((1, GLA_KEY_WIDTH)),
                  _const_spec((LANES, LANES))],
        out_specs=out_specs,
        out_shape=out_shapes,
        scratch_shapes=[pltpu.VMEM((CUM_ROWS, LANES), F32)],
        compiler_params=_params(1),
        name="inproj",
    )(h2d, g, wfox, wgla, wsm, bsm, wa, ba, upper)


def _fox_kernel(seq, q_ref, k_ref, v_ref, c_ref, o_ref, kaug_ref, qaug_ref, vt_ref):
    lane = lax.broadcasted_iota(jnp.int32, (1, HEAD_PAIR), 1)
    row = lax.broadcasted_iota(jnp.int32, (HEAD_PAIR, 1), 0)
    k_all = k_ref[0]
    q_t = q_ref[0].astype(F32).T
    v_t = v_ref[0].astype(F32).T
    for j in range(2):
        lo_lane, hi_lane = j * FOX_HEAD_DIM, (j + 1) * FOX_HEAD_DIM
        bias0 = (1 - j) * FOX_HEAD_DIM
        parts = [p.astype(F32) for p in _split3(-c_ref[0, 0, j:j + 1, :])]
        bias_t = jnp.zeros((HEAD_PAIR, seq), F32)
        for i, part in enumerate(parts):
            bias_t = jnp.where(row == bias0 + i, part, bias_t)
        own_lane = (lane >= lo_lane) & (lane < hi_lane)
        kaug_ref[j] = jnp.where(own_lane, k_all, bias_t.T.astype(BF16))
        own_row = (row >= lo_lane) & (row < hi_lane)
        one_row = (row >= bias0) & (row < bias0 + len(parts))
        qaug_ref[j] = jnp.where(own_row, q_t, jnp.where(one_row, 1.0, 0.0)).astype(BF16)
        vt_ref[j] = jnp.concatenate(
            [v_t[lo_lane:hi_lane], jnp.ones((FOX_ONES_ROWS, seq), F32)], axis=0).astype(BF16)

    chains = []
    for qi in range(seq // TQ):
        q0 = qi * TQ
        for j in range(2):
            steps, k0 = [], 0
            while k0 + FOX_WIDE <= q0:
                steps.append((k0, FOX_WIDE, False))
                k0 += FOX_WIDE
            if k0 < q0:
                steps.append((k0, q0 - k0, False))
            steps.append((q0, TQ, True))
            chains.append((q0, j, steps))
    items = []
    for g in range(0, len(chains), FOX_GROUP):
        group = chains[g:g + FOX_GROUP]
        for t in range(max(len(c[2]) for c in group)):
            for q0, j, steps in group:
                if t < len(steps):
                    items.append((q0, j) + steps[t])

    def scores(item):
        q0, j, k0, width, _ = item
        return _dot(kaug_ref[j, k0:k0 + width, :], qaug_ref[j, :, q0:q0 + TQ])

    state, out_t = {}, {}
    pending = [scores(item) for item in items[:FOX_AHEAD]]
    for t, item in enumerate(items):
        q0, j, k0, width, diagonal = item
        s_t = pending.pop(0)
        if t + FOX_AHEAD < len(items):
            pending.append(scores(items[t + FOX_AHEAD]))
        m, acc = state.get((q0, j), (jnp.full((1, TQ), -jnp.inf, F32),
                                     jnp.zeros((FOX_HEAD_DIM + FOX_ONES_ROWS, TQ), F32)))
        if diagonal:
            key = lax.broadcasted_iota(jnp.int32, (width, TQ), 0) + k0
            qry = lax.broadcasted_iota(jnp.int32, (width, TQ), 1) + q0
            s_t = jnp.where(key <= qry, s_t, -jnp.inf)
        m_next = jnp.maximum(m, jnp.max(s_t, axis=0, keepdims=True))
        p_t = jnp.exp2(s_t - m_next)
        acc = acc * jnp.exp2(m - m_next) + _dot(vt_ref[j, :, k0:k0 + width], p_t.astype(BF16))
        state[(q0, j)] = (m_next, acc)
        if diagonal:
            denom = acc[FOX_HEAD_DIM:FOX_HEAD_DIM + 8]
            out_t[(q0, j)] = acc[:FOX_HEAD_DIM] / jnp.concatenate(
                [denom] * (FOX_HEAD_DIM // 8), axis=0)
            if (q0, 1 - j) in out_t:
                both = jnp.concatenate([out_t[(q0, 0)], out_t[(q0, 1)]], axis=0)
                o_ref[0, q0:q0 + TQ, :] = both.T.astype(BF16)


def _fox(fq, fk, fv, c):
    batch, seq, _ = fq.shape
    n_pairs = FOX_WIDTH // HEAD_PAIR
    qkv = pl.BlockSpec((1, seq, HEAD_PAIR), lambda b, p: (b, 0, p))
    return pl.pallas_call(
        functools.partial(_fox_kernel, seq),
        grid=(batch, n_pairs),
        in_specs=[qkv, qkv, qkv, pl.BlockSpec((1, 1, 2, seq), lambda b, p: (b, p, 0, 0))],
        out_specs=qkv,
        out_shape=jax.ShapeDtypeStruct((batch, seq, FOX_WIDTH), BF16),
        scratch_shapes=[pltpu.VMEM((2, seq, HEAD_PAIR), BF16),
                        pltpu.VMEM((2, HEAD_PAIR, seq), BF16),
                        pltpu.VMEM((2, FOX_HEAD_DIM + FOX_ONES_ROWS, seq), BF16)],
        compiler_params=_params(2),
        name="fox",
    )(fq, fk, fv, c)


def _gla_kernel(seq, q_ref, k_ref, v_ref, la_ref, ltri_ref, o_ref):
    n_chunks = TT // GLA_CHUNK
    lane = lax.broadcasted_iota(jnp.int32, (1, HEAD_PAIR), 1)
    head_mask = (lane < GLA_DK, lane >= GLA_DK)
    row_chunk = lax.broadcasted_iota(jnp.int32, (TT, 1), 0) // GLA_CHUNK
    rr = lax.broadcasted_iota(jnp.int32, (TT, TT), 0)
    cc = lax.broadcasted_iota(jnp.int32, (TT, TT), 1)
    intra = (rr >= cc) & (rr // GLA_CHUNK == cc // GLA_CHUNK)
    ltri = ltri_ref[...]

    n_pairs = GLA_HEADS // 2
    n_tiles = seq // TT

    def rows_of(t):
        return slice(t * TT, (t + 1) * TT)

    def cumulate(t):
        hi, lo = _split2(la_ref[rows_of(t), :])
        return _dot(ltri, hi) + _dot(ltri, lo)

    def products(t, b_all):
        rows = rows_of(t)
        out = []
        for p in range(n_pairs):
            ksl = slice(p * HEAD_PAIR, (p + 1) * HEAD_PAIR)
            b = b_all[:, ksl]
            last_rows = [b[(i + 1) * GLA_CHUNK - 1:(i + 1) * GLA_CHUNK, :] for i in range(n_chunks)]
            b_last = jnp.concatenate(
                [jnp.broadcast_to(r, (GLA_CHUNK, HEAD_PAIR)) for r in last_rows], axis=0)
            q_in = (q_ref[rows, ksl] * jnp.exp(b)).astype(BF16)
            k = k_ref[rows, ksl]
            k_in = (k * jnp.exp(-b)).astype(BF16)
            k_st = (k * jnp.exp(b_last - b)).astype(BF16)
            kv = None
            attn = []
            for j in range(2):
                h = 2 * p + j
                vj = v_ref[rows, h * GLA_DV:(h + 1) * GLA_DV]
                kcat = jnp.concatenate(
                    [jnp.where((row_chunk == i) & head_mask[j], k_st, jnp.zeros_like(k_st))
                     for i in range(n_chunks)], axis=1)
                contrib = _dot_tn(vj, kcat)
                kv = contrib if kv is None else kv + contrib
                qh = jnp.where(head_mask[j], q_in, jnp.zeros_like(q_in))
                attn.append(_dot_nt(qh, k_in))
            out.append((q_in, [jnp.exp(r) for r in last_rows], kv, attn))
        return out

    state = [jnp.zeros((GLA_DV, HEAD_PAIR), F32) for _ in range(n_pairs)]

    def finish(t, prods):
        rows = rows_of(t)
        for p in range(n_pairs):
            q_in, decays, kv, attn = prods[p]
            o_inter = []
            for i in range(n_chunks):
                st = state[p].astype(BF16)
                st2 = jnp.concatenate(
                    [jnp.where(head_mask[j], st, jnp.zeros_like(st)) for j in range(2)], axis=0)
                o_inter.append(_dot_nt(q_in[i * GLA_CHUNK:(i + 1) * GLA_CHUNK, :], st2))
                state[p] = state[p] * decays[i] + kv[:, i * HEAD_PAIR:(i + 1) * HEAD_PAIR]
            o_intra = []
            for j in range(2):
                h = 2 * p + j
                vj = v_ref[rows, h * GLA_DV:(h + 1) * GLA_DV]
                o_intra.append(_dot(jnp.where(intra, attn[j], 0.0).astype(BF16), vj))
            o_ref[rows, 2 * p * GLA_DV:(2 * p + 2) * GLA_DV] = (
                jnp.concatenate(o_intra, axis=1) + jnp.concatenate(o_inter, axis=0))

    b_alls, prods = {}, {}
    for step in range(n_tiles + 2):
        if step < n_tiles:
            b_alls[step] = cumulate(step)
        if 0 <= step - 1 < n_tiles:
            prods[step - 1] = products(step - 1, b_alls.pop(step - 1))
        if 0 <= step - 2 < n_tiles:
            finish(step - 2, prods.pop(step - 2))


def _gla(gq, gk, gv, la, seq, ltri):
    t = gq.shape[0]
    row = lambda w: pl.BlockSpec((seq, w), lambda b: (b, 0))
    return pl.pallas_call(
        functools.partial(_gla_kernel, seq),
        grid=(t // seq,),
        in_specs=[row(GLA_KEY_WIDTH), row(GLA_KEY_WIDTH), row(GLA_VAL_WIDTH), row(GLA_KEY_WIDTH),
                  _const_spec((TT, TT))],
        out_specs=row(GLA_VAL_WIDTH),
        out_shape=jax.ShapeDtypeStruct((t, GLA_VAL_WIDTH), F32),
        compiler_params=_params(1),
        name="gla",
    )(gq, gk, gv, la, ltri)


def _outproj_kernel(h_ref, ofox_ref, ogla_ref, gpre_ref, wgate_ref, gnorm_ref, wbf_ref, wbg_ref,
                    wout_ref, gpost_ref, o_ref):
    subs = [slice(s * OUT_SUB, (s + 1) * OUT_SUB) for s in range(OUT_TM // OUT_SUB)]
    u = [_rms(h_ref[rows, :], gpre_ref[...]).astype(BF16) for rows in subs]
    gates = [_dot_nt(us, wgate_ref[...]) for us in u]
    branch_fox = [_dot(ofox_ref[rows, :], wbf_ref[...]) for rows in subs]
    branch_gla = []
    for s, rows in enumerate(subs):
        g_r = gates[s][:, :GLA_VAL_WIDTH]
        heads = []
        for hd in range(GLA_HEADS):
            sl = slice(hd * GLA_DV, (hd + 1) * GLA_DV)
            heads.append(_rms(ogla_ref[rows, sl], gnorm_ref[:, sl]))
        o_gla = jnp.concatenate(heads, axis=1)
        o_gla = (o_gla * (g_r * jax.nn.sigmoid(g_r))).astype(BF16)
        branch_gla.append(_dot(o_gla, wbg_ref[...]))
    merged = []
    for s in range(len(subs)):
        gate_fox = gates[s][:, GLA_VAL_WIDTH:GLA_VAL_WIDTH + D_MODEL]
        gate_gla = gates[s][:, GLA_VAL_WIDTH + D_MODEL:]
        y = jax.nn.sigmoid(gate_fox) * branch_fox[s] + jax.nn.sigmoid(gate_gla) * branch_gla[s]
        merged.append(_dot(y.astype(BF16), wout_ref[...]))
    for s, rows in enumerate(subs):
        o_ref[rows, :] = h_ref[rows, :] + _rms(merged[s], gpost_ref[...])


def _outproj(h2d, o_fox, o_gla, g_pre, wgate, gnorm, wbf, wbg, wout, g_post):
    t = h2d.shape[0]
    row = lambda w: pl.BlockSpec((OUT_TM, w), lambda i: (i, 0))
    return pl.pallas_call(
        _outproj_kernel,
        grid=(t // OUT_TM,),
        in_specs=[row(D_MODEL), row(FOX_WIDTH), row(GLA_VAL_WIDTH), _const_spec((1, D_MODEL)),
                  _const_spec((GLA_VAL_WIDTH + 2 * D_MODEL, D_MODEL)),
                  _const_spec((1, GLA_VAL_WIDTH)),
                  _const_spec((FOX_WIDTH, D_MODEL)), _const_spec((GLA_VAL_WIDTH, D_MODEL)),
                  _const_spec((D_MODEL, D_MODEL)), _const_spec((1, D_MODEL))],
        out_specs=row(D_MODEL),
        out_shape=jax.ShapeDtypeStruct((t, D_MODEL), F32),
        compiler_params=_params(1),
        name="outproj",
    )(h2d, o_fox, o_gla, g_pre, wgate, gnorm, wbf, wbg, wout, g_post)


def _chunk_tril(n):
    r = jnp.arange(n)[:, None]
    c = jnp.arange(n)[None, :]
    return (((r // GLA_CHUNK) == (c // GLA_CHUNK)) & (r >= c)).astype(BF16)


def _layer(h2d, batch, seq, ffn1_pre_g, ffn1_w_gate, ffn1_w_up, ffn1_w_down, ffn1_post_g,
           mix_pre_g, w_in, b_forget, w_alpha_up, b_alpha, gla_norm_g, w_branch_fox,
           w_branch_gla, w_out, mix_post_g, ffn2_pre_g, ffn2_w_gate, ffn2_w_up, ffn2_w_down,
           ffn2_post_g):
    vec = lambda g: g.reshape(1, -1)
    bf = lambda w: w.astype(BF16)

    splits = [FOX_WIDTH, FOX_WIDTH, FOX_WIDTH, FOX_HEADS, GLA_KEY_WIDTH, GLA_KEY_WIDTH,
              GLA_VAL_WIDTH, GLA_GATE_RANK, GLA_VAL_WIDTH, D_MODEL, D_MODEL]
    offs = [0]
    for s in splits:
        offs.append(offs[-1] + s)
    w_in_t = w_in.T
    col = lambda a, b: w_in_t[offs[a]:offs[b], :]
    wfox = bf(col(0, 3))
    wgla = bf(col(4, 7))
    pad_cols = SMALL_W - FOX_HEADS - GLA_GATE_RANK
    wsm = bf(jnp.concatenate([col(3, 4), col(7, 8), jnp.zeros((pad_cols, D_MODEL), F32)], axis=0))
    bsm = jnp.concatenate([b_forget, jnp.zeros((SMALL_W - FOX_HEADS,), F32)]).reshape(1, SMALL_W)
    wa = bf(jnp.concatenate([jnp.zeros((FOX_HEADS, GLA_KEY_WIDTH), F32), w_alpha_up,
                             jnp.zeros((pad_cols, GLA_KEY_WIDTH), F32)], axis=0))
    wgate = bf(col(8, 11))
    tok = jnp.arange(LANES)
    upper = (tok[:, None] <= tok[None, :]).astype(BF16)
    ltri = _chunk_tril(TT)

    h1, ffn2_w = _ffn(h2d, vec(ffn1_pre_g), bf(ffn1_w_gate), bf(ffn1_w_up), bf(ffn1_w_down),
                      vec(ffn1_post_g), cast=(ffn2_w_gate, ffn2_w_up, ffn2_w_down))

    fq, fk, fv, c, gq, gk, gv, la = _inproj(h1, seq, vec(mix_pre_g), wfox, wgla, wsm, bsm, wa,
                                            vec(b_alpha), upper)
    shape3 = (batch, seq, FOX_WIDTH)
    o_fox = _fox(fq.reshape(shape3), fk.reshape(shape3), fv.reshape(shape3),
                 c.reshape(batch, FOX_HEADS // 2, 2, seq)).reshape(batch * seq, FOX_WIDTH)
    o_gla = _gla(gq, gk, gv, la, seq, ltri)

    h2 = _outproj(h1, o_fox, o_gla, vec(mix_pre_g), wgate, vec(gla_norm_g), bf(w_branch_fox),
                  bf(w_branch_gla), bf(w_out), vec(mix_post_g))

    return _ffn(h2, vec(ffn2_pre_g), *ffn2_w, vec(ffn2_post_g))[0]


def kernel(x, ffn1_pre_g, ffn1_w_gate, ffn1_w_up, ffn1_w_down, ffn1_post_g, mix_pre_g, w_in,
           b_forget, w_alpha_up, b_alpha, gla_norm_g, w_branch_fox, w_branch_gla, w_out,
           mix_post_g, ffn2_pre_g, ffn2_w_gate, ffn2_w_up, ffn2_w_down, ffn2_post_g):
    batch, seq, d = x.shape
    h = x.reshape(batch * seq, d)
    depth = ffn1_pre_g.shape[0]
    for l in range(depth):
        h = _layer(h, batch, seq, ffn1_pre_g[l], ffn1_w_gate[l], ffn1_w_up[l], ffn1_w_down[l],
                   ffn1_post_g[l], mix_pre_g[l], w_in[l], b_forget[l], w_alpha_up[l], b_alpha[l],
                   gla_norm_g[l], w_branch_fox[l], w_branch_gla[l], w_out[l], mix_post_g[l],
                   ffn2_pre_g[l], ffn2_w_gate[l], ffn2_w_up[l], ffn2_w_down[l], ffn2_post_g[l])
    return h.reshape(batch, seq, d)
```

```python
import functools

import jax
import jax.numpy as jnp
from jax import lax
from jax.experimental import pallas as pl
from jax.experimental.pallas import tpu as pltpu

F32 = jnp.float32
BF16 = jnp.bfloat16

D_MODEL = 1024
D_FF = 2816
FOX_HEADS = 8
FOX_HEAD_DIM = 64
FOX_WIDTH = FOX_HEADS * FOX_HEAD_DIM
GLA_HEADS = 4
GLA_DK = 64
GLA_DV = 128
GLA_KEY_WIDTH = GLA_HEADS * GLA_DK
GLA_VAL_WIDTH = GLA_HEADS * GLA_DV
GLA_GATE_RANK = 16
GLA_TAU = 16.0
GLA_CHUNK = 64
NORM_EPS = 1e-6

LANES = 128
HEAD_PAIR = LANES
VMEM_LIMIT_BYTES = 56 * 1024 * 1024

IN_TM = 1024
IN_SUB = 512
FFN_TM = 1024
OUT_TM = 1024
OUT_SUB = 512
FFN_SUB = 512
TQ = 256
FOX_WIDE = 512
FOX_ONES_ROWS = 16
FOX_PAIRS = 4
FOX_GROUP = 8
FOX_AHEAD = 4
LOG2_E = 1.4426950408889634
TT = 256
SMALL_W = LANES
CUM_ROWS = 16


def _rms(x, g):
    return x * lax.rsqrt(jnp.mean(x * x, axis=-1, keepdims=True) + NORM_EPS) * g


def _log_sigmoid(x):
    return jnp.minimum(x, 0.0) - jnp.log1p(jnp.exp(-jnp.abs(x)))


def _dot(a, b):
    return jnp.dot(a, b, preferred_element_type=F32)


def _dot_nt(a, b):
    return lax.dot_general(a, b, (((1,), (1,)), ((), ())), preferred_element_type=F32)


def _dot_tn(a, b):
    return lax.dot_general(a, b, (((0,), (0,)), ((), ())), preferred_element_type=F32)


def _split2(x):
    hi = x.astype(BF16)
    lo = (x - hi.astype(F32)).astype(BF16)
    return hi, lo


def _split3(x):
    hi = x.astype(BF16)
    r = x - hi.astype(F32)
    mid = r.astype(BF16)
    lo = (r - mid.astype(F32)).astype(BF16)
    return hi, mid, lo


def _const_spec(shape):
    return pl.BlockSpec(shape, lambda *_: (0,) * len(shape), pipeline_mode=pl.Buffered(1))


def _params(n_axes):
    return pltpu.CompilerParams(dimension_semantics=("arbitrary",) * n_axes,
                                vmem_limit_bytes=VMEM_LIMIT_BYTES)


def _ffn_kernel(n_cast, x_ref, gpre_ref, wg_ref, wu_ref, wd_ref, gpost_ref, *rest):
    cast_in, o_ref, cast_out = rest[:n_cast], rest[n_cast], rest[n_cast + 1:]
    subs = [slice(s * FFN_SUB, (s + 1) * FFN_SUB) for s in range(FFN_TM // FFN_SUB)]
    xn = [_rms(x_ref[rows, :], gpre_ref[...]).astype(BF16) for rows in subs]
    hid = []
    for s in range(len(subs)):
        g = _dot(xn[s], wg_ref[...])
        u = _dot(xn[s], wu_ref[...])
        hid.append((g * jax.nn.sigmoid(g) * u).astype(BF16))
    acc = [_dot(h, wd_ref[...]) for h in hid]
    for s, rows in enumerate(subs):
        o_ref[rows, :] = x_ref[rows, :] + 0.5 * _rms(acc[s], gpost_ref[...])
    for src, dst in zip(cast_in, cast_out):
        dst[...] = src[...].astype(BF16)


def _cast_spec(shape, steps):
    rows, cols = shape
    span = 1 if (rows // steps) % 16 == 0 else 2
    assert rows % steps == 0 and (span * rows // steps) % 16 == 0
    return pl.BlockSpec((span * rows // steps, cols), lambda i: (i // span, 0))


def _ffn(x2d, g_pre, w_gate, w_up, w_down, g_post, cast=()):
    t = x2d.shape[0]
    steps = t // FFN_TM
    row = pl.BlockSpec((FFN_TM, D_MODEL), lambda i: (i, 0))
    cast_specs = [_cast_spec(w.shape, steps) for w in cast]
    out = pl.pallas_call(
        functools.partial(_ffn_kernel, len(cast)),
        grid=(steps,),
        in_specs=[row, _const_spec((1, D_MODEL)), _const_spec((D_MODEL, D_FF)),
                  _const_spec((D_MODEL, D_FF)), _const_spec((D_FF, D_MODEL)),
                  _const_spec((1, D_MODEL))] + cast_specs,
        out_specs=[row] + cast_specs,
        out_shape=[jax.ShapeDtypeStruct((t, D_MODEL), F32)]
        + [jax.ShapeDtypeStruct(w.shape, BF16) for w in cast],
        compiler_params=_params(1),
        name="ffn",
    )(x2d, g_pre, w_gate, w_up, w_down, g_post, *cast)
    return out[0], tuple(out[1:])


def _inproj_kernel(tiles_per_seq, h_ref, g_ref, wfox_ref, wgla_ref, wsm_ref, bsm_ref, wa_ref,
                   ba_ref, upper_ref, fq_ref, fk_ref, fv_ref, c_ref, gq_ref, gk_ref, gv_ref,
                   la_ref, carry_ref):
    @pl.when(pl.program_id(0) % tiles_per_seq == 0)
    def _():
        carry_ref[...] = jnp.zeros_like(carry_ref)

    subs = [slice(s * IN_SUB, (s + 1) * IN_SUB) for s in range(IN_TM // IN_SUB)]
    scale = FOX_HEAD_DIM ** -0.5 * LOG2_E
    u = [_rms(h_ref[rows, :], g_ref[...]).astype(BF16) for rows in subs]

    zs = [_dot_nt(us, wsm_ref[...]) for us in u]

    for s, rows in enumerate(subs):
        zf = _dot_nt(u[s], wfox_ref[...])
        fq_ref[rows, :] = (zf[:, :FOX_WIDTH] * scale).astype(BF16)
        fk_ref[rows, :] = zf[:, FOX_WIDTH:2 * FOX_WIDTH].astype(BF16)
        fv_ref[rows, :] = zf[:, 2 * FOX_WIDTH:].astype(BF16)

    for s, rows in enumerate(subs):
        a = _dot(zs[s].astype(BF16), wa_ref[...]) + ba_ref[...]
        la_ref[rows, :] = _log_sigmoid(a) * (1.0 / GLA_TAU)

    for s, rows in enumerate(subs):
        zg = _dot_nt(u[s], wgla_ref[...])
        gq_ref[rows, :] = zg[:, :GLA_KEY_WIDTH] * (GLA_DK ** -0.5)
        gk_ref[rows, :] = zg[:, GLA_KEY_WIDTH:2 * GLA_KEY_WIDTH]
        gv_ref[rows, :] = zg[:, 2 * GLA_KEY_WIDTH:].astype(BF16)

    upper = upper_ref[...]
    carry = carry_ref[...]
    n_blocks = IN_SUB // LANES
    for s, rows in enumerate(subs):
        lf_t = _log_sigmoid(zs[s] + bsm_ref[...]).T[:CUM_ROWS]
        parts = jnp.concatenate(_split3(lf_t), axis=0)
        stacked = jnp.concatenate(
            [parts[:, k * LANES:(k + 1) * LANES] for k in range(n_blocks)], axis=0)
        sums = _dot(stacked, upper)
        c_blocks = []
        for k in range(n_blocks):
            blk = sums[k * 3 * CUM_ROWS:(k + 1) * 3 * CUM_ROWS]
            c_blocks.append(blk[:CUM_ROWS] + blk[CUM_ROWS:2 * CUM_ROWS] + blk[2 * CUM_ROWS:] + carry)
            carry = carry + jnp.sum(lf_t[:, k * LANES:(k + 1) * LANES], axis=1, keepdims=True)
        c_ref[0, :, rows] = jnp.concatenate(c_blocks, axis=1)[:FOX_HEADS] * LOG2_E
    carry_ref[...] = carry


def _inproj(h2d, seq, g, wfox, wgla, wsm, bsm, wa, ba, upper):
    t = h2d.shape[0]
    batch = t // seq
    tiles_per_seq = seq // IN_TM
    row = lambda w: pl.BlockSpec((IN_TM, w), lambda i: (i, 0))
    out_shapes = (
        jax.ShapeDtypeStruct((t, FOX_WIDTH), BF16),
        jax.ShapeDtypeStruct((t, FOX_WIDTH), BF16),
        jax.ShapeDtypeStruct((t, FOX_WIDTH), BF16),
        jax.ShapeDtypeStruct((batch, FOX_HEADS, seq), F32),
        jax.ShapeDtypeStruct((t, GLA_KEY_WIDTH), F32),
        jax.ShapeDtypeStruct((t, GLA_KEY_WIDTH), F32),
        jax.ShapeDtypeStruct((t, GLA_VAL_WIDTH), BF16),
        jax.ShapeDtypeStruct((t, GLA_KEY_WIDTH), F32),
    )
    out_specs = (
        row(FOX_WIDTH), row(FOX_WIDTH), row(FOX_WIDTH),
        pl.BlockSpec((1, FOX_HEADS, IN_TM), lambda i: (i // tiles_per_seq, 0, i % tiles_per_seq)),
        row(GLA_KEY_WIDTH), row(GLA_KEY_WIDTH), row(GLA_VAL_WIDTH), row(GLA_KEY_WIDTH),
    )
    return pl.pallas_call(
        functools.partial(_inproj_kernel, tiles_per_seq),
        grid=(t // IN_TM,),
        in_specs=[row(D_MODEL), _const_spec((1, D_MODEL)),
                  _const_spec((3 * FOX_WIDTH, D_MODEL)),
                  _const_spec((2 * GLA_KEY_WIDTH + GLA_VAL_WIDTH, D_MODEL)),
                  _const_spec((SMALL_W, D_MODEL)), _const_spec((1, SMALL_W)),
                  _const_spec((SMALL_W, GLA_KEY_WIDTH)), _const_spec((1, GLA_KEY_WIDTH)),
                  _const_spec((LANES, LANES))],
        out_specs=out_specs,
        out_shape=out_shapes,
        scratch_shapes=[pltpu.VMEM((CUM_ROWS, LANES), F32)],
        compiler_params=_params(1),
        name="inproj",
    )(h2d, g, wfox, wgla, wsm, bsm, wa, ba, upper)


def _fox_kernel(seq, q_ref, k_ref, v_ref, c_ref, o_ref, kaug_ref, qaug_ref, vt_ref):
    lane = lax.broadcasted_iota(jnp.int32, (1, HEAD_PAIR), 1)
    row = lax.broadcasted_iota(jnp.int32, (HEAD_PAIR, 1), 0)
    n_heads = 2 * FOX_PAIRS
    for pair in range(FOX_PAIRS):
        pair_lanes = slice(pair * HEAD_PAIR, (pair + 1) * HEAD_PAIR)
        k_all = k_ref[0, :, pair_lanes]
        q_t = q_ref[0, :, pair_lanes].astype(F32).T
        v_t = v_ref[0, :, pair_lanes].astype(F32).T
        for j in range(2):
            h = 2 * pair + j
            lo_lane, hi_lane = j * FOX_HEAD_DIM, (j + 1) * FOX_HEAD_DIM
            bias0 = (1 - j) * FOX_HEAD_DIM
            parts = [p.astype(F32) for p in _split3(-c_ref[0, pair, j:j + 1, :])]
            bias_t = jnp.zeros((HEAD_PAIR, seq), F32)
            for i, part in enumerate(parts):
                bias_t = jnp.where(row == bias0 + i, part, bias_t)
            own_lane = (lane >= lo_lane) & (lane < hi_lane)
            kaug_ref[h] = jnp.where(own_lane, k_all, bias_t.T.astype(BF16))
            own_row = (row >= lo_lane) & (row < hi_lane)
            one_row = (row >= bias0) & (row < bias0 + len(parts))
            qaug_ref[h] = jnp.where(own_row, q_t, jnp.where(one_row, 1.0, 0.0)).astype(BF16)
            vt_ref[h] = jnp.concatenate(
                [v_t[lo_lane:hi_lane], jnp.ones((FOX_ONES_ROWS, seq), F32)], axis=0).astype(BF16)

    chains = []
    for qi in range(seq // TQ):
        q0 = qi * TQ
        for h in range(n_heads):
            steps, k0 = [], 0
            while k0 + FOX_WIDE <= q0:
                steps.append((k0, FOX_WIDE, False))
                k0 += FOX_WIDE
            if k0 < q0:
                steps.append((k0, q0 - k0, False))
            steps.append((q0, TQ, True))
            chains.append((q0, h, steps))
    items = []
    for g in range(0, len(chains), FOX_GROUP):
        group = chains[g:g + FOX_GROUP]
        for t in range(max(len(c[2]) for c in group)):
            for q0, h, steps in group:
                if t < len(steps):
                    items.append((q0, h) + steps[t])

    def scores(item):
        q0, h, k0, width, _ = item
        return _dot(kaug_ref[h, k0:k0 + width, :], qaug_ref[h, :, q0:q0 + TQ])

    state, out_t = {}, {}
    pending = [scores(item) for item in items[:FOX_AHEAD]]
    for t, item in enumerate(items):
        q0, h, k0, width, diagonal = item
        s_t = pending.pop(0)
        if t + FOX_AHEAD < len(items):
            pending.append(scores(items[t + FOX_AHEAD]))
        m, acc = state.get((q0, h), (jnp.full((1, TQ), -jnp.inf, F32),
                                     jnp.zeros((FOX_HEAD_DIM + FOX_ONES_ROWS, TQ), F32)))
        if diagonal:
            key = lax.broadcasted_iota(jnp.int32, (width, TQ), 0) + k0
            qry = lax.broadcasted_iota(jnp.int32, (width, TQ), 1) + q0
            s_t = jnp.where(key <= qry, s_t, -jnp.inf)
        m_next = jnp.maximum(m, jnp.max(s_t, axis=0, keepdims=True))
        p_t = jnp.exp2(s_t - m_next)
        acc = acc * jnp.exp2(m - m_next) + _dot(vt_ref[h, :, k0:k0 + width], p_t.astype(BF16))
        state[(q0, h)] = (m_next, acc)
        if diagonal:
            denom = acc[FOX_HEAD_DIM:FOX_HEAD_DIM + 8]
            out_t[(q0, h)] = acc[:FOX_HEAD_DIM] / jnp.concatenate(
                [denom] * (FOX_HEAD_DIM // 8), axis=0)
            if (q0, h ^ 1) in out_t:
                pair = h // 2
                both = jnp.concatenate([out_t[(q0, 2 * pair)], out_t[(q0, 2 * pair + 1)]], axis=0)
                o_ref[0, q0:q0 + TQ, pair * HEAD_PAIR:(pair + 1) * HEAD_PAIR] = both.T.astype(BF16)


def _fox(fq, fk, fv, c):
    batch, seq, _ = fq.shape
    n_pairs = FOX_WIDTH // HEAD_PAIR
    qkv = pl.BlockSpec((1, seq, FOX_PAIRS * HEAD_PAIR), lambda b, p: (b, 0, p))
    n_heads = 2 * FOX_PAIRS
    return pl.pallas_call(
        functools.partial(_fox_kernel, seq),
        grid=(batch, n_pairs // FOX_PAIRS),
        in_specs=[qkv, qkv, qkv,
                  pl.BlockSpec((1, FOX_PAIRS, 2, seq), lambda b, p: (b, p, 0, 0))],
        out_specs=qkv,
        out_shape=jax.ShapeDtypeStruct((batch, seq, FOX_WIDTH), BF16),
        scratch_shapes=[pltpu.VMEM((n_heads, seq, HEAD_PAIR), BF16),
                        pltpu.VMEM((n_heads, HEAD_PAIR, seq), BF16),
                        pltpu.VMEM((n_heads, FOX_HEAD_DIM + FOX_ONES_ROWS, seq), BF16)],
        compiler_params=_params(2),
        name="fox",
    )(fq, fk, fv, c)


def _gla_kernel(seq, q_ref, k_ref, v_ref, la_ref, ltri_ref, o_ref):
    n_chunks = TT // GLA_CHUNK
    lane = lax.broadcasted_iota(jnp.int32, (1, HEAD_PAIR), 1)
    head_mask = (lane < GLA_DK, lane >= GLA_DK)
    row_chunk = lax.broadcasted_iota(jnp.int32, (TT, 1), 0) // GLA_CHUNK
    rr = lax.broadcasted_iota(jnp.int32, (TT, TT), 0)
    cc = lax.broadcasted_iota(jnp.int32, (TT, TT), 1)
    intra = (rr >= cc) & (rr // GLA_CHUNK == cc // GLA_CHUNK)
    ltri = ltri_ref[...]

    n_pairs = GLA_HEADS // 2
    n_tiles = seq // TT

    def rows_of(t):
        return slice(t * TT, (t + 1) * TT)

    def cumulate(t):
        hi, lo = _split2(la_ref[rows_of(t), :])
        return _dot(ltri, hi) + _dot(ltri, lo)

    def products(t, b_all):
        rows = rows_of(t)
        out = []
        for p in range(n_pairs):
            ksl = slice(p * HEAD_PAIR, (p + 1) * HEAD_PAIR)
            b = b_all[:, ksl]
            last_rows = [b[(i + 1) * GLA_CHUNK - 1:(i + 1) * GLA_CHUNK, :] for i in range(n_chunks)]
            b_last = jnp.concatenate(
                [jnp.broadcast_to(r, (GLA_CHUNK, HEAD_PAIR)) for r in last_rows], axis=0)
            q_in = (q_ref[rows, ksl] * jnp.exp(b)).astype(BF16)
            k = k_ref[rows, ksl]
            k_in = (k * jnp.exp(-b)).astype(BF16)
            k_st = (k * jnp.exp(b_last - b)).astype(BF16)
            kv = None
            attn = []
            for j in range(2):
                h = 2 * p + j
                vj = v_ref[rows, h * GLA_DV:(h + 1) * GLA_DV]
                kcat = jnp.concatenate(
                    [jnp.where((row_chunk == i) & head_mask[j], k_st, jnp.zeros_like(k_st))
                     for i in range(n_chunks)], axis=1)
                contrib = _dot_tn(vj, kcat)
                kv = contrib if kv is None else kv + contrib
                qh = jnp.where(head_mask[j], q_in, jnp.zeros_like(q_in))
                attn.append(_dot_nt(qh, k_in))
            out.append((q_in, [jnp.exp(r) for r in last_rows], kv, attn))
        return out

    state = [jnp.zeros((GLA_DV, HEAD_PAIR), F32) for _ in range(n_pairs)]

    def finish(t, prods):
        rows = rows_of(t)
        for p in range(n_pairs):
            q_in, decays, kv, attn = prods[p]
            o_inter = []
            for i in range(n_chunks):
                st = state[p].astype(BF16)
                st2 = jnp.concatenate(
                    [jnp.where(head_mask[j], st, jnp.zeros_like(st)) for j in range(2)], axis=0)
                o_inter.append(_dot_nt(q_in[i * GLA_CHUNK:(i + 1) * GLA_CHUNK, :], st2))
                state[p] = state[p] * decays[i] + kv[:, i * HEAD_PAIR:(i + 1) * HEAD_PAIR]
            o_intra = []
            for j in range(2):
                h = 2 * p + j
                vj = v_ref[rows, h * GLA_DV:(h + 1) * GLA_DV]
                o_intra.append(_dot(jnp.where(intra, attn[j], 0.0).astype(BF16), vj))
            o_ref[rows, 2 * p * GLA_DV:(2 * p + 2) * GLA_DV] = (
                jnp.concatenate(o_intra, axis=1) + jnp.concatenate(o_inter, axis=0))

    b_alls, prods = {}, {}
    for step in range(n_tiles + 2):
        if step < n_tiles:
            b_alls[step] = cumulate(step)
        if 0 <= step - 1 < n_tiles:
            prods[step - 1] = products(step - 1, b_alls.pop(step - 1))
        if 0 <= step - 2 < n_tiles:
            finish(step - 2, prods.pop(step - 2))


def _gla(gq, gk, gv, la, seq, ltri):
    t = gq.shape[0]
    row = lambda w: pl.BlockSpec((seq, w), lambda b: (b, 0))
    return pl.pallas_call(
        functools.partial(_gla_kernel, seq),
        grid=(t // seq,),
        in_specs=[row(GLA_KEY_WIDTH), row(GLA_KEY_WIDTH), row(GLA_VAL_WIDTH), row(GLA_KEY_WIDTH),
                  _const_spec((TT, TT))],
        out_specs=row(GLA_VAL_WIDTH),
        out_shape=jax.ShapeDtypeStruct((t, GLA_VAL_WIDTH), F32),
        compiler_params=_params(1),
        name="gla",
    )(gq, gk, gv, la, ltri)


def _outproj_kernel(h_ref, ofox_ref, ogla_ref, gpre_ref, wgate_ref, gnorm_ref, wbf_ref, wbg_ref,
                    wout_ref, gpost_ref, o_ref):
    subs = [slice(s * OUT_SUB, (s + 1) * OUT_SUB) for s in range(OUT_TM // OUT_SUB)]
    u = [_rms(h_ref[rows, :], gpre_ref[...]).astype(BF16) for rows in subs]
    gates = [_dot_nt(us, wgate_ref[...]) for us in u]
    branch_fox = [_dot(ofox_ref[rows, :], wbf_ref[...]) for rows in subs]
    branch_gla = []
    for s, rows in enumerate(subs):
        g_r = gates[s][:, :GLA_VAL_WIDTH]
        heads = []
        for hd in range(GLA_HEADS):
            sl = slice(hd * GLA_DV, (hd + 1) * GLA_DV)
            heads.append(_rms(ogla_ref[rows, sl], gnorm_ref[:, sl]))
        o_gla = jnp.concatenate(heads, axis=1)
        o_gla = (o_gla * (g_r * jax.nn.sigmoid(g_r))).astype(BF16)
        branch_gla.append(_dot(o_gla, wbg_ref[...]))
    merged = []
    for s in range(len(subs)):
        gate_fox = gates[s][:, GLA_VAL_WIDTH:GLA_VAL_WIDTH + D_MODEL]
        gate_gla = gates[s][:, GLA_VAL_WIDTH + D_MODEL:]
        y = jax.nn.sigmoid(gate_fox) * branch_fox[s] + jax.nn.sigmoid(gate_gla) * branch_gla[s]
        merged.append(_dot(y.astype(BF16), wout_ref[...]))
    for s, rows in enumerate(subs):
        o_ref[rows, :] = h_ref[rows, :] + _rms(merged[s], gpost_ref[...])


def _outproj(h2d, o_fox, o_gla, g_pre, wgate, gnorm, wbf, wbg, wout, g_post):
    t = h2d.shape[0]
    row = lambda w: pl.BlockSpec((OUT_TM, w), lambda i: (i, 0))
    return pl.pallas_call(
        _outproj_kernel,
        grid=(t // OUT_TM,),
        in_specs=[row(D_MODEL), row(FOX_WIDTH), row(GLA_VAL_WIDTH), _const_spec((1, D_MODEL)),
                  _const_spec((GLA_VAL_WIDTH + 2 * D_MODEL, D_MODEL)),
                  _const_spec((1, GLA_VAL_WIDTH)),
                  _const_spec((FOX_WIDTH, D_MODEL)), _const_spec((GLA_VAL_WIDTH, D_MODEL)),
                  _const_spec((D_MODEL, D_MODEL)), _const_spec((1, D_MODEL))],
        out_specs=row(D_MODEL),
        out_shape=jax.ShapeDtypeStruct((t, D_MODEL), F32),
        compiler_params=_params(1),
        name="outproj",
    )(h2d, o_fox, o_gla, g_pre, wgate, gnorm, wbf, wbg, wout, g_post)


def _chunk_tril(n):
    r = jnp.arange(n)[:, None]
    c = jnp.arange(n)[None, :]
    return (((r // GLA_CHUNK) == (c // GLA_CHUNK)) & (r >= c)).astype(BF16)


def _layer(h2d, batch, seq, ffn1_pre_g, ffn1_w_gate, ffn1_w_up, ffn1_w_down, ffn1_post_g,
           mix_pre_g, w_in, b_forget, w_alpha_up, b_alpha, gla_norm_g, w_branch_fox,
           w_branch_gla, w_out, mix_post_g, ffn2_pre_g, ffn2_w_gate, ffn2_w_up, ffn2_w_down,
           ffn2_post_g):
    vec = lambda g: g.reshape(1, -1)
    bf = lambda w: w.astype(BF16)

    splits = [FOX_WIDTH, FOX_WIDTH, FOX_WIDTH, FOX_HEADS, GLA_KEY_WIDTH, GLA_KEY_WIDTH,
              GLA_VAL_WIDTH, GLA_GATE_RANK, GLA_VAL_WIDTH, D_MODEL, D_MODEL]
    offs = [0]
    for s in splits:
        offs.append(offs[-1] + s)
    w_in_t = w_in.T
    col = lambda a, b: w_in_t[offs[a]:offs[b], :]
    wfox = bf(col(0, 3))
    wgla = bf(col(4, 7))
    pad_cols = SMALL_W - FOX_HEADS - GLA_GATE_RANK
    wsm = bf(jnp.concatenate([col(3, 4), col(7, 8), jnp.zeros((pad_cols, D_MODEL), F32)], axis=0))
    bsm = jnp.concatenate([b_forget, jnp.zeros((SMALL_W - FOX_HEADS,), F32)]).reshape(1, SMALL_W)
    wa = bf(jnp.concatenate([jnp.zeros((FOX_HEADS, GLA_KEY_WIDTH), F32), w_alpha_up,
                             jnp.zeros((pad_cols, GLA_KEY_WIDTH), F32)], axis=0))
    wgate = bf(col(8, 11))
    tok = jnp.arange(LANES)
    upper = (tok[:, None] <= tok[None, :]).astype(BF16)
    ltri = _chunk_tril(TT)

    h1, ffn2_w = _ffn(h2d, vec(ffn1_pre_g), bf(ffn1_w_gate), bf(ffn1_w_up), bf(ffn1_w_down),
                      vec(ffn1_post_g), cast=(ffn2_w_gate, ffn2_w_up, ffn2_w_down))

    fq, fk, fv, c, gq, gk, gv, la = _inproj(h1, seq, vec(mix_pre_g), wfox, wgla, wsm, bsm, wa,
                                            vec(b_alpha), upper)
    shape3 = (batch, seq, FOX_WIDTH)
    o_fox = _fox(fq.reshape(shape3), fk.reshape(shape3), fv.reshape(shape3),
                 c.reshape(batch, FOX_HEADS // 2, 2, seq)).reshape(batch * seq, FOX_WIDTH)
    o_gla = _gla(gq, gk, gv, la, seq, ltri)

    h2 = _outproj(h1, o_fox, o_gla, vec(mix_pre_g), wgate, vec(gla_norm_g), bf(w_branch_fox),
                  bf(w_branch_gla), bf(w_out), vec(mix_post_g))

    return _ffn(h2, vec(ffn2_pre_g), *ffn2_w, vec(ffn2_post_g))[0]


def kernel(x, ffn1_pre_g, ffn1_w_gate, ffn1_w_up, ffn1_w_down, ffn1_post_g, mix_pre_g, w_in,
           b_forget, w_alpha_up, b_alpha, gla_norm_g, w_branch_fox, w_branch_gla, w_out,
           mix_post_g, ffn2_pre_g, ffn2_w_gate, ffn2_w_up, ffn2_w_down, ffn2_post_g):
    batch, seq, d = x.shape
    h = x.reshape(batch * seq, d)
    depth = ffn1_pre_g.shape[0]
    for l in range(depth):
        h = _layer(h, batch, seq, ffn1_pre_g[l], ffn1_w_gate[l], ffn1_w_up[l], ffn1_w_down[l],
                   ffn1_post_g[l], mix_pre_g[l], w_in[l], b_forget[l], w_alpha_up[l], b_alpha[l],
                   gla_norm_g[l], w_branch_fox[l], w_branch_gla[l], w_out[l], mix_post_g[l],
                   ffn2_pre_g[l], ffn2_w_gate[l], ffn2_w_up[l], ffn2_w_down[l], ffn2_post_g[l])
    return h.reshape(batch, seq, d)
```

```python
import functools

import jax
import jax.numpy as jnp
from jax import lax
from jax.experimental import pallas as pl
from jax.experimental.pallas import tpu as pltpu

F32 = jnp.float32
BF16 = jnp.bfloat16

D_MODEL = 1024
D_FF = 2816
FOX_HEADS = 8
FOX_HEAD_DIM = 64
FOX_WIDTH = FOX_HEADS * FOX_HEAD_DIM
GLA_HEADS = 4
GLA_DK = 64
GLA_DV = 128
GLA_KEY_WIDTH = GLA_HEADS * GLA_DK
GLA_VAL_WIDTH = GLA_HEADS * GLA_DV
GLA_GATE_RANK = 16
GLA_TAU = 16.0
GLA_CHUNK = 64
NORM_EPS = 1e-6

LANES = 128
HEAD_PAIR = LANES
VMEM_LIMIT_BYTES = 56 * 1024 * 1024

IN_TM = 1024
IN_SUB = 512
FFN_TM = 1024
FF_CHUNK = 256
OUT_TM = 1024
OUT_SUB = 512
FFN_SUB = 512
TQ = 256
FOX_WIDE = 512
FOX_ONES_ROWS = 16
FOX_PAIRS = 4
FOX_GROUP = 8
FOX_AHEAD = 4
LOG2_E = 1.4426950408889634
TT = 256
SMALL_W = LANES
CUM_ROWS = 16


def _rms(x, g):
    return x * lax.rsqrt(jnp.mean(x * x, axis=-1, keepdims=True) + NORM_EPS) * g


def _log_sigmoid(x):
    return jnp.minimum(x, 0.0) - jnp.log1p(jnp.exp(-jnp.abs(x)))


def _dot(a, b):
    return jnp.dot(a, b, preferred_element_type=F32)


def _dot_nt(a, b):
    return lax.dot_general(a, b, (((1,), (1,)), ((), ())), preferred_element_type=F32)


def _dot_tn(a, b):
    return lax.dot_general(a, b, (((0,), (0,)), ((), ())), preferred_element_type=F32)


def _split2(x):
    hi = x.astype(BF16)
    lo = (x - hi.astype(F32)).astype(BF16)
    return hi, lo


def _split3(x):
    hi = x.astype(BF16)
    r = x - hi.astype(F32)
    mid = r.astype(BF16)
    lo = (r - mid.astype(F32)).astype(BF16)
    return hi, mid, lo


def _const_spec(shape):
    return pl.BlockSpec(shape, lambda *_: (0,) * len(shape), pipeline_mode=pl.Buffered(1))


def _params(n_axes):
    return pltpu.CompilerParams(dimension_semantics=("arbitrary",) * n_axes,
                                vmem_limit_bytes=VMEM_LIMIT_BYTES)


def _ffn_kernel(n_cast, x_ref, gpre_ref, wg_ref, wu_ref, wd_ref, gpost_ref, *rest):
    cast_in, o_ref, cast_out = rest[:n_cast], rest[n_cast], rest[n_cast + 1:]
    subs = [slice(s * FFN_SUB, (s + 1) * FFN_SUB) for s in range(FFN_TM // FFN_SUB)]
    xn = [_rms(x_ref[rows, :], gpre_ref[...]).astype(BF16) for rows in subs]
    acc = [None] * len(subs)
    for c in range(D_FF // FF_CHUNK):
        cols = slice(c * FF_CHUNK, (c + 1) * FF_CHUNK)
        for s in range(len(subs)):
            g = _dot(xn[s], wg_ref[:, cols])
            u = _dot(xn[s], wu_ref[:, cols])
            h = (g * jax.nn.sigmoid(g) * u).astype(BF16)
            part = _dot(h, wd_ref[cols, :])
            acc[s] = part if acc[s] is None else acc[s] + part
    for s, rows in enumerate(subs):
        o_ref[rows, :] = x_ref[rows, :] + 0.5 * _rms(acc[s], gpost_ref[...])
    for src, dst in zip(cast_in, cast_out):
        dst[...] = src[...].astype(BF16)


def _cast_spec(shape, steps):
    rows, cols = shape
    span = 1 if (rows // steps) % 16 == 0 else 2
    assert rows % steps == 0 and (span * rows // steps) % 16 == 0
    return pl.BlockSpec((span * rows // steps, cols), lambda i: (i // span, 0))


def _ffn(x2d, g_pre, w_gate, w_up, w_down, g_post, cast=()):
    t = x2d.shape[0]
    steps = t // FFN_TM
    row = pl.BlockSpec((FFN_TM, D_MODEL), lambda i: (i, 0))
    cast_specs = [_cast_spec(w.shape, steps) for w in cast]
    out = pl.pallas_call(
        functools.partial(_ffn_kernel, len(cast)),
        grid=(steps,),
        in_specs=[row, _const_spec((1, D_MODEL)), _const_spec((D_MODEL, D_FF)),
                  _const_spec((D_MODEL, D_FF)), _const_spec((D_FF, D_MODEL)),
                  _const_spec((1, D_MODEL))] + cast_specs,
        out_specs=[row] + cast_specs,
        out_shape=[jax.ShapeDtypeStruct((t, D_MODEL), F32)]
        + [jax.ShapeDtypeStruct(w.shape, BF16) for w in cast],
        compiler_params=_params(1),
        name="ffn",
    )(x2d, g_pre, w_gate, w_up, w_down, g_post, *cast)
    return out[0], tuple(out[1:])


def _inproj_kernel(tiles_per_seq, h_ref, g_ref, wfox_ref, wgla_ref, wsm_ref, bsm_ref, wa_ref,
                   ba_ref, upper_ref, fq_ref, fk_ref, fv_ref, c_ref, gq_ref, gk_ref, gv_ref,
                   la_ref, carry_ref):
    @pl.when(pl.program_id(0) % tiles_per_seq == 0)
    def _():
        carry_ref[...] = jnp.zeros_like(carry_ref)

    subs = [slice(s * IN_SUB, (s + 1) * IN_SUB) for s in range(IN_TM // IN_SUB)]
    scale = FOX_HEAD_DIM ** -0.5 * LOG2_E
    u = [_rms(h_ref[rows, :], g_ref[...]).astype(BF16) for rows in subs]

    zs = [_dot_nt(us, wsm_ref[...]) for us in u]

    for s, rows in enumerate(subs):
        zf = _dot_nt(u[s], wfox_ref[...])
        fq_ref[rows, :] = (zf[:, :FOX_WIDTH] * scale).astype(BF16)
        fk_ref[rows, :] = zf[:, FOX_WIDTH:2 * FOX_WIDTH].astype(BF16)
        fv_ref[rows, :] = zf[:, 2 * FOX_WIDTH:].astype(BF16)

    for s, rows in enumerate(subs):
        a = _dot(zs[s].astype(BF16), wa_ref[...]) + ba_ref[...]
        la_ref[rows, :] = _log_sigmoid(a) * (1.0 / GLA_TAU)

    for s, rows in enumerate(subs):
        zg = _dot_nt(u[s], wgla_ref[...])
        gq_ref[rows, :] = zg[:, :GLA_KEY_WIDTH] * (GLA_DK ** -0.5)
        gk_ref[rows, :] = zg[:, GLA_KEY_WIDTH:2 * GLA_KEY_WIDTH]
        gv_ref[rows, :] = zg[:, 2 * GLA_KEY_WIDTH:].astype(BF16)

    upper = upper_ref[...]
    carry = carry_ref[...]
    n_blocks = IN_SUB // LANES
    for s, rows in enumerate(subs):
        lf_t = _log_sigmoid(zs[s] + bsm_ref[...]).T[:CUM_ROWS]
        parts = jnp.concatenate(_split3(lf_t), axis=0)
        stacked = jnp.concatenate(
            [parts[:, k * LANES:(k + 1) * LANES] for k in range(n_blocks)], axis=0)
        sums = _dot(stacked, upper)
        c_blocks = []
        for k in range(n_blocks):
            blk = sums[k * 3 * CUM_ROWS:(k + 1) * 3 * CUM_ROWS]
            c_blocks.append(blk[:CUM_ROWS] + blk[CUM_ROWS:2 * CUM_ROWS] + blk[2 * CUM_ROWS:] + carry)
            carry = carry + jnp.sum(lf_t[:, k * LANES:(k + 1) * LANES], axis=1, keepdims=True)
        c_ref[0, :, rows] = jnp.concatenate(c_blocks, axis=1)[:FOX_HEADS] * LOG2_E
    carry_ref[...] = carry


def _inproj(h2d, seq, g, wfox, wgla, wsm, bsm, wa, ba, upper):
    t = h2d.shape[0]
    batch = t // seq
    tiles_per_seq = seq // IN_TM
    row = lambda w: pl.BlockSpec((IN_TM, w), lambda i: (i, 0))
    out_shapes = (
        jax.ShapeDtypeStruct((t, FOX_WIDTH), BF16),
        jax.ShapeDtypeStruct((t, FOX_WIDTH), BF16),
        jax.ShapeDtypeStruct((t, FOX_WIDTH), BF16),
        jax.ShapeDtypeStruct((batch, FOX_HEADS, seq), F32),
        jax.ShapeDtypeStruct((t, GLA_KEY_WIDTH), F32),
        jax.ShapeDtypeStruct((t, GLA_KEY_WIDTH), F32),
        jax.ShapeDtypeStruct((t, GLA_VAL_WIDTH), BF16),
        jax.ShapeDtypeStruct((t, GLA_KEY_WIDTH), F32),
    )
    out_specs = (
        row(FOX_WIDTH), row(FOX_WIDTH), row(FOX_WIDTH),
        pl.BlockSpec((1, FOX_HEADS, IN_TM), lambda i: (i // tiles_per_seq, 0, i % tiles_per_seq)),
        row(GLA_KEY_WIDTH), row(GLA_KEY_WIDTH), row(GLA_VAL_WIDTH), row(GLA_KEY_WIDTH),
    )
    return pl.pallas_call(
        functools.partial(_inproj_kernel, tiles_per_seq),
        grid=(t // IN_TM,),
        in_specs=[row(D_MODEL), _const_spec((1, D_MODEL)),
                  _const_spec((3 * FOX_WIDTH, D_MODEL)),
                  _const_spec((2 * GLA_KEY_WIDTH + GLA_VAL_WIDTH, D_MODEL)),
                  _const_spec((SMALL_W, D_MODEL)), _const_spec((1, SMALL_W)),
                  _const_spec((SMALL_W, GLA_KEY_WIDTH)), _const_spec((1, GLA_KEY_WIDTH)),
                  _const_spec((LANES, LANES))],
        out_specs=out_specs,
        out_shape=out_shapes,
        scratch_shapes=[pltpu.VMEM((CUM_ROWS, LANES), F32)],
        compiler_params=_params(1),
        name="inproj",
    )(h2d, g, wfox, wgla, wsm, bsm, wa, ba, upper)


def _fox_kernel(seq, q_ref, k_ref, v_ref, c_ref, o_ref, kaug_ref, qaug_ref, vt_ref):
    lane = lax.broadcasted_iota(jnp.int32, (1, HEAD_PAIR), 1)
    row = lax.broadcasted_iota(jnp.int32, (HEAD_PAIR, 1), 0)
    n_heads = 2 * FOX_PAIRS
    for pair in range(FOX_PAIRS):
        pair_lanes = slice(pair * HEAD_PAIR, (pair + 1) * HEAD_PAIR)
        k_all = k_ref[0, :, pair_lanes]
        q_t = q_ref[0, :, pair_lanes].astype(F32).T
        v_t = v_ref[0, :, pair_lanes].astype(F32).T
        for j in range(2):
            h = 2 * pair + j
            lo_lane, hi_lane = j * FOX_HEAD_DIM, (j + 1) * FOX_HEAD_DIM
            bias0 = (1 - j) * FOX_HEAD_DIM
            parts = [p.astype(F32) for p in _split3(-c_ref[0, pair, j:j + 1, :])]
            bias_t = jnp.zeros((HEAD_PAIR, seq), F32)
            for i, part in enumerate(parts):
                bias_t = jnp.where(row == bias0 + i, part, bias_t)
            own_lane = (lane >= lo_lane) & (lane < hi_lane)
            kaug_ref[h] = jnp.where(own_lane, k_all, bias_t.T.astype(BF16))
            own_row = (row >= lo_lane) & (row < hi_lane)
            one_row = (row >= bias0) & (row < bias0 + len(parts))
            qaug_ref[h] = jnp.where(own_row, q_t, jnp.where(one_row, 1.0, 0.0)).astype(BF16)
            vt_ref[h] = jnp.concatenate(
                [v_t[lo_lane:hi_lane], jnp.ones((FOX_ONES_ROWS, seq), F32)], axis=0).astype(BF16)

    chains = []
    for qi in reversed(range(seq // TQ)):
        q0 = qi * TQ
        for h in range(n_heads):
            steps, k0 = [], 0
            while k0 + FOX_WIDE <= q0:
                steps.append((k0, FOX_WIDE, False))
                k0 += FOX_WIDE
            if k0 < q0:
                steps.append((k0, q0 - k0, False))
            steps.append((q0, TQ, True))
            chains.append((q0, h, steps))
    items = []
    for g in range(0, len(chains), FOX_GROUP):
        group = chains[g:g + FOX_GROUP]
        for t in range(max(len(c[2]) for c in group)):
            for q0, h, steps in group:
                if t < len(steps):
                    items.append((q0, h) + steps[t])

    def scores(item):
        q0, h, k0, width, _ = item
        return _dot(kaug_ref[h, k0:k0 + width, :], qaug_ref[h, :, q0:q0 + TQ])

    state, out_t = {}, {}
    pending = [scores(item) for item in items[:FOX_AHEAD]]
    for t, item in enumerate(items):
        q0, h, k0, width, diagonal = item
        s_t = pending.pop(0)
        if t + FOX_AHEAD < len(items):
            pending.append(scores(items[t + FOX_AHEAD]))
        m, acc = state.get((q0, h), (jnp.full((1, TQ), -jnp.inf, F32),
                                     jnp.zeros((FOX_HEAD_DIM + FOX_ONES_ROWS, TQ), F32)))
        if diagonal:
            key = lax.broadcasted_iota(jnp.int32, (width, TQ), 0) + k0
            qry = lax.broadcasted_iota(jnp.int32, (width, TQ), 1) + q0
            s_t = jnp.where(key <= qry, s_t, -jnp.inf)
        m_next = jnp.maximum(m, jnp.max(s_t, axis=0, keepdims=True))
        p_t = jnp.exp2(s_t - m_next)
        acc = acc * jnp.exp2(m - m_next) + _dot(vt_ref[h, :, k0:k0 + width], p_t.astype(BF16))
        state[(q0, h)] = (m_next, acc)
        if diagonal:
            denom = acc[FOX_HEAD_DIM:FOX_HEAD_DIM + 8]
            out_t[(q0, h)] = acc[:FOX_HEAD_DIM] / jnp.concatenate(
                [denom] * (FOX_HEAD_DIM // 8), axis=0)
            if (q0, h ^ 1) in out_t:
                pair = h // 2
                both = jnp.concatenate([out_t[(q0, 2 * pair)], out_t[(q0, 2 * pair + 1)]], axis=0)
                o_ref[0, q0:q0 + TQ, pair * HEAD_PAIR:(pair + 1) * HEAD_PAIR] = both.T.astype(BF16)


def _fox(fq, fk, fv, c):
    batch, seq, _ = fq.shape
    n_pairs = FOX_WIDTH // HEAD_PAIR
    qkv = pl.BlockSpec((1, seq, FOX_PAIRS * HEAD_PAIR), lambda b, p: (b, 0, p))
    n_heads = 2 * FOX_PAIRS
    return pl.pallas_call(
        functools.partial(_fox_kernel, seq),
        grid=(batch, n_pairs // FOX_PAIRS),
        in_specs=[qkv, qkv, qkv,
                  pl.BlockSpec((1, FOX_PAIRS, 2, seq), lambda b, p: (b, p, 0, 0))],
        out_specs=qkv,
        out_shape=jax.ShapeDtypeStruct((batch, seq, FOX_WIDTH), BF16),
        scratch_shapes=[pltpu.VMEM((n_heads, seq, HEAD_PAIR), BF16),
                        pltpu.VMEM((n_heads, HEAD_PAIR, seq), BF16),
                        pltpu.VMEM((n_heads, FOX_HEAD_DIM + FOX_ONES_ROWS, seq), BF16)],
        compiler_params=_params(2),
        name="fox",
    )(fq, fk, fv, c)


def _gla_kernel(seq, q_ref, k_ref, v_ref, la_ref, ltri_ref, o_ref):
    n_chunks = TT // GLA_CHUNK
    lane = lax.broadcasted_iota(jnp.int32, (1, HEAD_PAIR), 1)
    head_mask = (lane < GLA_DK, lane >= GLA_DK)
    row_chunk = lax.broadcasted_iota(jnp.int32, (TT, 1), 0) // GLA_CHUNK
    rr = lax.broadcasted_iota(jnp.int32, (TT, TT), 0)
    cc = lax.broadcasted_iota(jnp.int32, (TT, TT), 1)
    intra = (rr >= cc) & (rr // GLA_CHUNK == cc // GLA_CHUNK)
    ltri = ltri_ref[...]

    n_pairs = GLA_HEADS // 2
    n_tiles = seq // TT

    def rows_of(t):
        return slice(t * TT, (t + 1) * TT)

    def cumulate(t):
        hi, lo = _split2(la_ref[rows_of(t), :])
        return _dot(ltri, hi) + _dot(ltri, lo)

    def products(t, b_all):
        rows = rows_of(t)
        out = []
        for p in range(n_pairs):
            ksl = slice(p * HEAD_PAIR, (p + 1) * HEAD_PAIR)
            b = b_all[:, ksl]
            last_rows = [b[(i + 1) * GLA_CHUNK - 1:(i + 1) * GLA_CHUNK, :] for i in range(n_chunks)]
            b_last = jnp.concatenate(
                [jnp.broadcast_to(r, (GLA_CHUNK, HEAD_PAIR)) for r in last_rows], axis=0)
            q_in = (q_ref[rows, ksl] * jnp.exp(b)).astype(BF16)
            k = k_ref[rows, ksl]
            k_in = (k * jnp.exp(-b)).astype(BF16)
            k_st = (k * jnp.exp(b_last - b)).astype(BF16)
            kv = None
            attn = []
            for j in range(2):
                h = 2 * p + j
                vj = v_ref[rows, h * GLA_DV:(h + 1) * GLA_DV]
                kcat = jnp.concatenate(
                    [jnp.where((row_chunk == i) & head_mask[j], k_st, jnp.zeros_like(k_st))
                     for i in range(n_chunks)], axis=1)
                contrib = _dot_tn(vj, kcat)
                kv = contrib if kv is None else kv + contrib
                qh = jnp.where(head_mask[j], q_in, jnp.zeros_like(q_in))
                attn.append(_dot_nt(qh, k_in))
            out.append((q_in, [jnp.exp(r) for r in last_rows], kv, attn))
        return out

    state = [jnp.zeros((GLA_DV, HEAD_PAIR), F32) for _ in range(n_pairs)]

    def finish(t, prods):
        rows = rows_of(t)
        for p in range(n_pairs):
            q_in, decays, kv, attn = prods[p]
            o_inter = []
            for i in range(n_chunks):
                st = state[p].astype(BF16)
                st2 = jnp.concatenate(
                    [jnp.where(head_mask[j], st, jnp.zeros_like(st)) for j in range(2)], axis=0)
                o_inter.append(_dot_nt(q_in[i * GLA_CHUNK:(i + 1) * GLA_CHUNK, :], st2))
                state[p] = state[p] * decays[i] + kv[:, i * HEAD_PAIR:(i + 1) * HEAD_PAIR]
            o_intra = []
            for j in range(2):
                h = 2 * p + j
                vj = v_ref[rows, h * GLA_DV:(h + 1) * GLA_DV]
                o_intra.append(_dot(jnp.where(intra, attn[j], 0.0).astype(BF16), vj))
            o_ref[rows, 2 * p * GLA_DV:(2 * p + 2) * GLA_DV] = (
                jnp.concatenate(o_intra, axis=1) + jnp.concatenate(o_inter, axis=0))

    b_alls, prods = {}, {}
    for step in range(n_tiles + 2):
        if step < n_tiles:
            b_alls[step] = cumulate(step)
        if 0 <= step - 1 < n_tiles:
            prods[step - 1] = products(step - 1, b_alls.pop(step - 1))
        if 0 <= step - 2 < n_tiles:
            finish(step - 2, prods.pop(step - 2))


def _gla(gq, gk, gv, la, seq, ltri):
    t = gq.shape[0]
    row = lambda w: pl.BlockSpec((seq, w), lambda b: (b, 0))
    return pl.pallas_call(
        functools.partial(_gla_kernel, seq),
        grid=(t // seq,),
        in_specs=[row(GLA_KEY_WIDTH), row(GLA_KEY_WIDTH), row(GLA_VAL_WIDTH), row(GLA_KEY_WIDTH),
                  _const_spec((TT, TT))],
        out_specs=row(GLA_VAL_WIDTH),
        out_shape=jax.ShapeDtypeStruct((t, GLA_VAL_WIDTH), F32),
        compiler_params=_params(1),
        name="gla",
    )(gq, gk, gv, la, ltri)


def _outproj_kernel(h_ref, ofox_ref, ogla_ref, gpre_ref, wgate_ref, gnorm_ref, wbf_ref, wbg_ref,
                    wout_ref, gpost_ref, o_ref):
    subs = [slice(s * OUT_SUB, (s + 1) * OUT_SUB) for s in range(OUT_TM // OUT_SUB)]
    u = [_rms(h_ref[rows, :], gpre_ref[...]).astype(BF16) for rows in subs]
    gates = [_dot_nt(us, wgate_ref[...]) for us in u]
    branch_fox = [_dot(ofox_ref[rows, :], wbf_ref[...]) for rows in subs]
    branch_gla = []
    for s, rows in enumerate(subs):
        g_r = gates[s][:, :GLA_VAL_WIDTH]
        heads = []
        for hd in range(GLA_HEADS):
            sl = slice(hd * GLA_DV, (hd + 1) * GLA_DV)
            heads.append(_rms(ogla_ref[rows, sl], gnorm_ref[:, sl]))
        o_gla = jnp.concatenate(heads, axis=1)
        o_gla = (o_gla * (g_r * jax.nn.sigmoid(g_r))).astype(BF16)
        branch_gla.append(_dot(o_gla, wbg_ref[...]))
    merged = []
    for s in range(len(subs)):
        gate_fox = gates[s][:, GLA_VAL_WIDTH:GLA_VAL_WIDTH + D_MODEL]
        gate_gla = gates[s][:, GLA_VAL_WIDTH + D_MODEL:]
        y = jax.nn.sigmoid(gate_fox) * branch_fox[s] + jax.nn.sigmoid(gate_gla) * branch_gla[s]
        merged.append(_dot(y.astype(BF16), wout_ref[...]))
    for s, rows in enumerate(subs):
        o_ref[rows, :] = h_ref[rows, :] + _rms(merged[s], gpost_ref[...])


def _outproj(h2d, o_fox, o_gla, g_pre, wgate, gnorm, wbf, wbg, wout, g_post):
    t = h2d.shape[0]
    row = lambda w: pl.BlockSpec((OUT_TM, w), lambda i: (i, 0))
    return pl.pallas_call(
        _outproj_kernel,
        grid=(t // OUT_TM,),
        in_specs=[row(D_MODEL), row(FOX_WIDTH), row(GLA_VAL_WIDTH), _const_spec((1, D_MODEL)),
                  _const_spec((GLA_VAL_WIDTH + 2 * D_MODEL, D_MODEL)),
                  _const_spec((1, GLA_VAL_WIDTH)),
                  _const_spec((FOX_WIDTH, D_MODEL)), _const_spec((GLA_VAL_WIDTH, D_MODEL)),
                  _const_spec((D_MODEL, D_MODEL)), _const_spec((1, D_MODEL))],
        out_specs=row(D_MODEL),
        out_shape=jax.ShapeDtypeStruct((t, D_MODEL), F32),
        compiler_params=_params(1),
        name="outproj",
    )(h2d, o_fox, o_gla, g_pre, wgate, gnorm, wbf, wbg, wout, g_post)


def _chunk_tril(n):
    r = jnp.arange(n)[:, None]
    c = jnp.arange(n)[None, :]
    return (((r // GLA_CHUNK) == (c // GLA_CHUNK)) & (r >= c)).astype(BF16)


def _layer(h2d, batch, seq, ffn1_pre_g, ffn1_w_gate, ffn1_w_up, ffn1_w_down, ffn1_post_g,
           mix_pre_g, w_in, b_forget, w_alpha_up, b_alpha, gla_norm_g, w_branch_fox,
           w_branch_gla, w_out, mix_post_g, ffn2_pre_g, ffn2_w_gate, ffn2_w_up, ffn2_w_down,
           ffn2_post_g):
    vec = lambda g: g.reshape(1, -1)
    bf = lambda w: w.astype(BF16)

    splits = [FOX_WIDTH, FOX_WIDTH, FOX_WIDTH, FOX_HEADS, GLA_KEY_WIDTH, GLA_KEY_WIDTH,
              GLA_VAL_WIDTH, GLA_GATE_RANK, GLA_VAL_WIDTH, D_MODEL, D_MODEL]
    offs = [0]
    for s in splits:
        offs.append(offs[-1] + s)
    w_in_t = w_in.T
    col = lambda a, b: w_in_t[offs[a]:offs[b], :]
    wfox = bf(col(0, 3))
    wgla = bf(col(4, 7))
    pad_cols = SMALL_W - FOX_HEADS - GLA_GATE_RANK
    wsm = bf(jnp.concatenate([col(3, 4), col(7, 8), jnp.zeros((pad_cols, D_MODEL), F32)], axis=0))
    bsm = jnp.concatenate([b_forget, jnp.zeros((SMALL_W - FOX_HEADS,), F32)]).reshape(1, SMALL_W)
    wa = bf(jnp.concatenate([jnp.zeros((FOX_HEADS, GLA_KEY_WIDTH), F32), w_alpha_up,
                             jnp.zeros((pad_cols, GLA_KEY_WIDTH), F32)], axis=0))
    wgate = bf(col(8, 11))
    tok = jnp.arange(LANES)
    upper = (tok[:, None] <= tok[None, :]).astype(BF16)
    ltri = _chunk_tril(TT)

    h1, ffn2_w = _ffn(h2d, vec(ffn1_pre_g), bf(ffn1_w_gate), bf(ffn1_w_up), bf(ffn1_w_down),
                      vec(ffn1_post_g), cast=(ffn2_w_gate, ffn2_w_up, ffn2_w_down))

    fq, fk, fv, c, gq, gk, gv, la = _inproj(h1, seq, vec(mix_pre_g), wfox, wgla, wsm, bsm, wa,
                                            vec(b_alpha), upper)
    shape3 = (batch, seq, FOX_WIDTH)
    o_fox = _fox(fq.reshape(shape3), fk.reshape(shape3), fv.reshape(shape3),
                 c.reshape(batch, FOX_HEADS // 2, 2, seq)).reshape(batch * seq, FOX_WIDTH)
    o_gla = _gla(gq, gk, gv, la, seq, ltri)

    h2 = _outproj(h1, o_fox, o_gla, vec(mix_pre_g), wgate, vec(gla_norm_g), bf(w_branch_fox),
                  bf(w_branch_gla), bf(w_out), vec(mix_post_g))

    return _ffn(h2, vec(ffn2_pre_g), *ffn2_w, vec(ffn2_post_g))[0]


def kernel(x, ffn1_pre_g, ffn1_w_gate, ffn1_w_up, ffn1_w_down, ffn1_post_g, mix_pre_g, w_in,
           b_forget, w_alpha_up, b_alpha, gla_norm_g, w_branch_fox, w_branch_gla, w_out,
           mix_post_g, ffn2_pre_g, ffn2_w_gate, ffn2_w_up, ffn2_w_down, ffn2_post_g):
    batch, seq, d = x.shape
    h = x.reshape(batch * seq, d)
    depth = ffn1_pre_g.shape[0]
    for l in range(depth):
        h = _layer(h, batch, seq, ffn1_pre_g[l], ffn1_w_gate[l], ffn1_w_up[l], ffn1_w_down[l],
                   ffn1_post_g[l], mix_pre_g[l], w_in[l], b_forget[l], w_alpha_up[l], b_alpha[l],
                   gla_norm_g[l], w_branch_fox[l], w_branch_gla[l], w_out[l], mix_post_g[l],
                   ffn2_pre_g[l], ffn2_w_gate[l], ffn2_w_up[l], ffn2_w_down[l], ffn2_post_g[l])
    return h.reshape(batch, seq, d)
```

```python
import functools

import jax
import jax.numpy as jnp
from jax import lax
from jax.experimental import pallas as pl
from jax.experimental.pallas import tpu as pltpu

F32 = jnp.float32
BF16 = jnp.bfloat16

D_MODEL = 1024
D_FF = 2816
FOX_HEADS = 8
FOX_HEAD_DIM = 64
FOX_WIDTH = FOX_HEADS * FOX_HEAD_DIM
GLA_HEADS = 4
GLA_DK = 64
GLA_DV = 128
GLA_KEY_WIDTH = GLA_HEADS * GLA_DK
GLA_VAL_WIDTH = GLA_HEADS * GLA_DV
GLA_GATE_RANK = 16
GLA_TAU = 16.0
GLA_CHUNK = 64
NORM_EPS = 1e-6

LANES = 128
HEAD_PAIR = LANES
VMEM_LIMIT_BYTES = 56 * 1024 * 1024

IN_TM = 1024
IN_SUB = 512
FFN_TM = 1024
FF_CHUNK = 256
OUT_TM = 1024
OUT_SUB = 512
FFN_SUB = 512
TQ = 256
FOX_WIDE = 512
FOX_ONES_ROWS = 16
FOX_PAIRS = 4
FOX_GROUP = 8
FOX_AHEAD = 4
LOG2_E = 1.4426950408889634
TT = 256
SMALL_W = LANES
CUM_ROWS = 16


def _rms(x, g):
    return x * lax.rsqrt(jnp.mean(x * x, axis=-1, keepdims=True) + NORM_EPS) * g


def _log_sigmoid(x):
    return jnp.minimum(x, 0.0) - jnp.log1p(jnp.exp(-jnp.abs(x)))


def _dot(a, b):
    return jnp.dot(a, b, preferred_element_type=F32)


def _dot_nt(a, b):
    return lax.dot_general(a, b, (((1,), (1,)), ((), ())), preferred_element_type=F32)


def _dot_tn(a, b):
    return lax.dot_general(a, b, (((0,), (0,)), ((), ())), preferred_element_type=F32)


def _split2(x):
    hi = x.astype(BF16)
    lo = (x - hi.astype(F32)).astype(BF16)
    return hi, lo


def _split3(x):
    hi = x.astype(BF16)
    r = x - hi.astype(F32)
    mid = r.astype(BF16)
    lo = (r - mid.astype(F32)).astype(BF16)
    return hi, mid, lo


def _const_spec(shape):
    return pl.BlockSpec(shape, lambda *_: (0,) * len(shape), pipeline_mode=pl.Buffered(1))


def _params(n_axes):
    return pltpu.CompilerParams(dimension_semantics=("arbitrary",) * n_axes,
                                vmem_limit_bytes=VMEM_LIMIT_BYTES)


def _ffn_kernel(n_cast, x_ref, gpre_ref, wg_ref, wu_ref, wd_ref, gpost_ref, *rest):
    cast_in, o_ref, cast_out = rest[:n_cast], rest[n_cast], rest[n_cast + 1:]
    subs = [slice(s * FFN_SUB, (s + 1) * FFN_SUB) for s in range(FFN_TM // FFN_SUB)]
    xn = [_rms(x_ref[rows, :], gpre_ref[...]).astype(BF16) for rows in subs]
    acc = [None] * len(subs)
    for c in range(D_FF // FF_CHUNK):
        cols = slice(c * FF_CHUNK, (c + 1) * FF_CHUNK)
        for s in range(len(subs)):
            g = _dot(xn[s], wg_ref[:, cols])
            u = _dot(xn[s], wu_ref[:, cols])
            h = (g * jax.nn.sigmoid(g) * u).astype(BF16)
            part = _dot(h, wd_ref[cols, :])
            acc[s] = part if acc[s] is None else acc[s] + part
    for s, rows in enumerate(subs):
        o_ref[rows, :] = x_ref[rows, :] + 0.5 * _rms(acc[s], gpost_ref[...])
    for src, dst in zip(cast_in, cast_out):
        dst[...] = src[...].astype(BF16)


def _cast_spec(shape, steps):
    rows, cols = shape
    span = 1 if (rows // steps) % 16 == 0 else 2
    assert rows % steps == 0 and (span * rows // steps) % 16 == 0
    return pl.BlockSpec((span * rows // steps, cols), lambda i: (i // span, 0))


def _ffn(x2d, g_pre, w_gate, w_up, w_down, g_post, cast=()):
    t = x2d.shape[0]
    steps = t // FFN_TM
    row = pl.BlockSpec((FFN_TM, D_MODEL), lambda i: (i, 0))
    cast_specs = [_cast_spec(w.shape, steps) for w in cast]
    out = pl.pallas_call(
        functools.partial(_ffn_kernel, len(cast)),
        grid=(steps,),
        in_specs=[row, _const_spec((1, D_MODEL)), _const_spec((D_MODEL, D_FF)),
                  _const_spec((D_MODEL, D_FF)), _const_spec((D_FF, D_MODEL)),
                  _const_spec((1, D_MODEL))] + cast_specs,
        out_specs=[row] + cast_specs,
        out_shape=[jax.ShapeDtypeStruct((t, D_MODEL), F32)]
        + [jax.ShapeDtypeStruct(w.shape, BF16) for w in cast],
        compiler_params=_params(1),
        name="ffn",
    )(x2d, g_pre, w_gate, w_up, w_down, g_post, *cast)
    return out[0], tuple(out[1:])


def _inproj_kernel(tiles_per_seq, h_ref, g_ref, wfox_ref, wgla_ref, wsm_ref, bsm_ref, wa_ref,
                   ba_ref, upper_ref, fq_ref, fk_ref, fv_ref, c_ref, gq_ref, gk_ref, gv_ref,
                   la_ref, carry_ref):
    @pl.when(pl.program_id(0) % tiles_per_seq == 0)
    def _():
        carry_ref[...] = jnp.zeros_like(carry_ref)

    subs = [slice(s * IN_SUB, (s + 1) * IN_SUB) for s in range(IN_TM // IN_SUB)]
    scale = FOX_HEAD_DIM ** -0.5 * LOG2_E
    u = [_rms(h_ref[rows, :], g_ref[...]).astype(BF16) for rows in subs]

    zs = [_dot_nt(us, wsm_ref[...]) for us in u]

    for s, rows in enumerate(subs):
        zf = _dot_nt(u[s], wfox_ref[...])
        fq_ref[rows, :] = (zf[:, :FOX_WIDTH] * scale).astype(BF16)
        fk_ref[rows, :] = zf[:, FOX_WIDTH:2 * FOX_WIDTH].astype(BF16)
        fv_ref[rows, :] = zf[:, 2 * FOX_WIDTH:].astype(BF16)

    for s, rows in enumerate(subs):
        a = _dot(zs[s].astype(BF16), wa_ref[...]) + ba_ref[...]
        la_ref[rows, :] = _log_sigmoid(a) * (1.0 / GLA_TAU)

    for s, rows in enumerate(subs):
        zg = _dot_nt(u[s], wgla_ref[...])
        gq_ref[rows, :] = zg[:, :GLA_KEY_WIDTH] * (GLA_DK ** -0.5)
        gk_ref[rows, :] = zg[:, GLA_KEY_WIDTH:2 * GLA_KEY_WIDTH]
        gv_ref[rows, :] = zg[:, 2 * GLA_KEY_WIDTH:].astype(BF16)

    upper = upper_ref[...]
    carry = carry_ref[...]
    n_blocks = IN_SUB // LANES
    for s, rows in enumerate(subs):
        lf_t = _log_sigmoid(zs[s] + bsm_ref[...]).T[:CUM_ROWS]
        parts = jnp.concatenate(_split3(lf_t), axis=0)
        stacked = jnp.concatenate(
            [parts[:, k * LANES:(k + 1) * LANES] for k in range(n_blocks)], axis=0)
        sums = _dot(stacked, upper)
        c_blocks = []
        for k in range(n_blocks):
            blk = sums[k * 3 * CUM_ROWS:(k + 1) * 3 * CUM_ROWS]
            c_blocks.append(blk[:CUM_ROWS] + blk[CUM_ROWS:2 * CUM_ROWS] + blk[2 * CUM_ROWS:] + carry)
            carry = carry + jnp.sum(lf_t[:, k * LANES:(k + 1) * LANES], axis=1, keepdims=True)
        c_ref[0, :, rows] = jnp.concatenate(c_blocks, axis=1)[:FOX_HEADS] * LOG2_E
    carry_ref[...] = carry


def _inproj(h2d, seq, g, wfox, wgla, wsm, bsm, wa, ba, upper):
    t = h2d.shape[0]
    batch = t // seq
    tiles_per_seq = seq // IN_TM
    row = lambda w: pl.BlockSpec((IN_TM, w), lambda i: (i, 0))
    out_shapes = (
        jax.ShapeDtypeStruct((t, FOX_WIDTH), BF16),
        jax.ShapeDtypeStruct((t, FOX_WIDTH), BF16),
        jax.ShapeDtypeStruct((t, FOX_WIDTH), BF16),
        jax.ShapeDtypeStruct((batch, FOX_HEADS, seq), F32),
        jax.ShapeDtypeStruct((t, GLA_KEY_WIDTH), F32),
        jax.ShapeDtypeStruct((t, GLA_KEY_WIDTH), F32),
        jax.ShapeDtypeStruct((t, GLA_VAL_WIDTH), BF16),
        jax.ShapeDtypeStruct((t, GLA_KEY_WIDTH), F32),
    )
    out_specs = (
        row(FOX_WIDTH), row(FOX_WIDTH), row(FOX_WIDTH),
        pl.BlockSpec((1, FOX_HEADS, IN_TM), lambda i: (i // tiles_per_seq, 0, i % tiles_per_seq)),
        row(GLA_KEY_WIDTH), row(GLA_KEY_WIDTH), row(GLA_VAL_WIDTH), row(GLA_KEY_WIDTH),
    )
    return pl.pallas_call(
        functools.partial(_inproj_kernel, tiles_per_seq),
        grid=(t // IN_TM,),
        in_specs=[row(D_MODEL), _const_spec((1, D_MODEL)),
                  _const_spec((3 * FOX_WIDTH, D_MODEL)),
                  _const_spec((2 * GLA_KEY_WIDTH + GLA_VAL_WIDTH, D_MODEL)),
                  _const_spec((SMALL_W, D_MODEL)), _const_spec((1, SMALL_W)),
                  _const_spec((SMALL_W, GLA_KEY_WIDTH)), _const_spec((1, GLA_KEY_WIDTH)),
                  _const_spec((LANES, LANES))],
        out_specs=out_specs,
        out_shape=out_shapes,
        scratch_shapes=[pltpu.VMEM((CUM_ROWS, LANES), F32)],
        compiler_params=_params(1),
        name="inproj",
    )(h2d, g, wfox, wgla, wsm, bsm, wa, ba, upper)


def _fox_kernel(seq, q_ref, k_ref, v_ref, c_ref, o_ref, kaug_ref, qaug_ref, vt_ref):
    lane = lax.broadcasted_iota(jnp.int32, (1, HEAD_PAIR), 1)
    row = lax.broadcasted_iota(jnp.int32, (HEAD_PAIR, 1), 0)
    n_heads = 2 * FOX_PAIRS
    for pair in range(FOX_PAIRS):
        pair_lanes = slice(pair * HEAD_PAIR, (pair + 1) * HEAD_PAIR)
        k_all = k_ref[0, :, pair_lanes]
        q_t = q_ref[0, :, pair_lanes].astype(F32).T
        v_t = v_ref[0, :, pair_lanes].astype(F32).T
        for j in range(2):
            h = 2 * pair + j
            lo_lane, hi_lane = j * FOX_HEAD_DIM, (j + 1) * FOX_HEAD_DIM
            bias0 = (1 - j) * FOX_HEAD_DIM
            parts = [p.astype(F32) for p in _split3(-c_ref[0, pair, j:j + 1, :])]
            bias_t = jnp.zeros((HEAD_PAIR, seq), F32)
            for i, part in enumerate(parts):
                bias_t = jnp.where(row == bias0 + i, part, bias_t)
            own_lane = (lane >= lo_lane) & (lane < hi_lane)
            kaug_ref[h] = jnp.where(own_lane, k_all, bias_t.T.astype(BF16))
            own_row = (row >= lo_lane) & (row < hi_lane)
            one_row = (row >= bias0) & (row < bias0 + len(parts))
            qaug_ref[h] = jnp.where(own_row, q_t, jnp.where(one_row, 1.0, 0.0)).astype(BF16)
            vt_ref[h] = jnp.concatenate(
                [v_t[lo_lane:hi_lane], jnp.ones((FOX_ONES_ROWS, seq), F32)], axis=0).astype(BF16)

    chains = []
    for qi in reversed(range(seq // TQ)):
        q0 = qi * TQ
        for h in range(n_heads):
            steps, k0 = [], 0
            while k0 + FOX_WIDE <= q0:
                steps.append((k0, FOX_WIDE, False))
                k0 += FOX_WIDE
            if k0 < q0:
                steps.append((k0, q0 - k0, False))
            steps.append((q0, TQ, True))
            chains.append((q0, h, steps))
    items = []
    for g in range(0, len(chains), FOX_GROUP):
        group = chains[g:g + FOX_GROUP]
        for t in range(max(len(c[2]) for c in group)):
            for q0, h, steps in group:
                if t < len(steps):
                    items.append((q0, h) + steps[t])

    def scores(item):
        q0, h, k0, width, _ = item
        return _dot(kaug_ref[h, k0:k0 + width, :], qaug_ref[h, :, q0:q0 + TQ])

    state, out_t = {}, {}
    pending = [scores(item) for item in items[:FOX_AHEAD]]
    for t, item in enumerate(items):
        q0, h, k0, width, diagonal = item
        s_t = pending.pop(0)
        if t + FOX_AHEAD < len(items):
            pending.append(scores(items[t + FOX_AHEAD]))
        m, acc = state.get((q0, h), (jnp.full((1, TQ), -jnp.inf, F32),
                                     jnp.zeros((FOX_HEAD_DIM + FOX_ONES_ROWS, TQ), F32)))
        if diagonal:
            key = lax.broadcasted_iota(jnp.int32, (width, TQ), 0) + k0
            qry = lax.broadcasted_iota(jnp.int32, (width, TQ), 1) + q0
            s_t = jnp.where(key <= qry, s_t, -jnp.inf)
        m_next = jnp.maximum(m, jnp.max(s_t, axis=0, keepdims=True))
        p_t = jnp.exp2(s_t - m_next)
        acc = acc * jnp.exp2(m - m_next) + _dot(vt_ref[h, :, k0:k0 + width], p_t.astype(BF16))
        state[(q0, h)] = (m_next, acc)
        if diagonal:
            denom = acc[FOX_HEAD_DIM:FOX_HEAD_DIM + 8]
            out_t[(q0, h)] = acc[:FOX_HEAD_DIM] / jnp.concatenate(
                [denom] * (FOX_HEAD_DIM // 8), axis=0)
            if (q0, h ^ 1) in out_t:
                pair = h // 2
                both = jnp.concatenate([out_t[(q0, 2 * pair)], out_t[(q0, 2 * pair + 1)]], axis=0)
                o_ref[0, q0:q0 + TQ, pair * HEAD_PAIR:(pair + 1) * HEAD_PAIR] = both.T.astype(BF16)


def _gla_kernel(seq, q_ref, k_ref, v_ref, la_ref, ltri_ref, o_ref):
    n_chunks = TT // GLA_CHUNK
    lane = lax.broadcasted_iota(jnp.int32, (1, HEAD_PAIR), 1)
    head_mask = (lane < GLA_DK, lane >= GLA_DK)
    row_chunk = lax.broadcasted_iota(jnp.int32, (TT, 1), 0) // GLA_CHUNK
    rr = lax.broadcasted_iota(jnp.int32, (TT, TT), 0)
    cc = lax.broadcasted_iota(jnp.int32, (TT, TT), 1)
    intra = (rr >= cc) & (rr // GLA_CHUNK == cc // GLA_CHUNK)
    ltri = ltri_ref[...]

    n_pairs = GLA_HEADS // 2
    n_tiles = seq // TT

    def rows_of(t):
        return slice(t * TT, (t + 1) * TT)

    def cumulate(t):
        hi, lo = _split2(la_ref[rows_of(t), :])
        return _dot(ltri, hi) + _dot(ltri, lo)

    def products(t, b_all):
        rows = rows_of(t)
        out = []
        for p in range(n_pairs):
            ksl = slice(p * HEAD_PAIR, (p + 1) * HEAD_PAIR)
            b = b_all[:, ksl]
            last_rows = [b[(i + 1) * GLA_CHUNK - 1:(i + 1) * GLA_CHUNK, :] for i in range(n_chunks)]
            b_last = jnp.concatenate(
                [jnp.broadcast_to(r, (GLA_CHUNK, HEAD_PAIR)) for r in last_rows], axis=0)
            q_in = (q_ref[rows, ksl] * jnp.exp(b)).astype(BF16)
            k = k_ref[rows, ksl]
            k_in = (k * jnp.exp(-b)).astype(BF16)
            k_st = (k * jnp.exp(b_last - b)).astype(BF16)
            kv = None
            attn = []
            for j in range(2):
                h = 2 * p + j
                vj = v_ref[rows, h * GLA_DV:(h + 1) * GLA_DV]
                kcat = jnp.concatenate(
                    [jnp.where((row_chunk == i) & head_mask[j], k_st, jnp.zeros_like(k_st))
                     for i in range(n_chunks)], axis=1)
                contrib = _dot_tn(vj, kcat)
                kv = contrib if kv is None else kv + contrib
                qh = jnp.where(head_mask[j], q_in, jnp.zeros_like(q_in))
                attn.append(_dot_nt(qh, k_in))
            out.append((q_in, [jnp.exp(r) for r in last_rows], kv, attn))
        return out

    state = [jnp.zeros((GLA_DV, HEAD_PAIR), F32) for _ in range(n_pairs)]

    def finish(t, prods):
        rows = rows_of(t)
        for p in range(n_pairs):
            q_in, decays, kv, attn = prods[p]
            o_inter = []
            for i in range(n_chunks):
                st = state[p].astype(BF16)
                st2 = jnp.concatenate(
                    [jnp.where(head_mask[j], st, jnp.zeros_like(st)) for j in range(2)], axis=0)
                o_inter.append(_dot_nt(q_in[i * GLA_CHUNK:(i + 1) * GLA_CHUNK, :], st2))
                state[p] = state[p] * decays[i] + kv[:, i * HEAD_PAIR:(i + 1) * HEAD_PAIR]
            o_intra = []
            for j in range(2):
                h = 2 * p + j
                vj = v_ref[rows, h * GLA_DV:(h + 1) * GLA_DV]
                o_intra.append(_dot(jnp.where(intra, attn[j], 0.0).astype(BF16), vj))
            o_ref[rows, 2 * p * GLA_DV:(2 * p + 2) * GLA_DV] = (
                jnp.concatenate(o_intra, axis=1) + jnp.concatenate(o_inter, axis=0))

    b_alls, prods = {}, {}
    for step in range(n_tiles + 2):
        if step < n_tiles:
            b_alls[step] = cumulate(step)
        if 0 <= step - 1 < n_tiles:
            prods[step - 1] = products(step - 1, b_alls.pop(step - 1))
        if 0 <= step - 2 < n_tiles:
            finish(step - 2, prods.pop(step - 2))


def _mixers_kernel(seq, fq_ref, fk_ref, fv_ref, c_ref, gq_ref, gk_ref, gv_ref, la_ref, ltri_ref,
                   ofox_ref, ogla_ref, kaug_ref, qaug_ref, vt_ref):
    _fox_kernel(seq, fq_ref, fk_ref, fv_ref, c_ref, ofox_ref, kaug_ref, qaug_ref, vt_ref)
    _gla_kernel(seq, gq_ref, gk_ref, gv_ref, la_ref, ltri_ref, ogla_ref)


def _mixers(fq, fk, fv, c, gq, gk, gv, la, ltri):
    batch, seq, _ = fq.shape
    assert FOX_PAIRS * HEAD_PAIR == FOX_WIDTH
    n_heads = 2 * FOX_PAIRS
    qkv = pl.BlockSpec((1, seq, FOX_WIDTH), lambda b: (b, 0, 0))
    row = lambda w: pl.BlockSpec((seq, w), lambda b: (b, 0))
    return pl.pallas_call(
        functools.partial(_mixers_kernel, seq),
        grid=(batch,),
        in_specs=[qkv, qkv, qkv, pl.BlockSpec((1, FOX_PAIRS, 2, seq), lambda b: (b, 0, 0, 0)),
                  row(GLA_KEY_WIDTH), row(GLA_KEY_WIDTH), row(GLA_VAL_WIDTH), row(GLA_KEY_WIDTH),
                  _const_spec((TT, TT))],
        out_specs=[qkv, row(GLA_VAL_WIDTH)],
        out_shape=[jax.ShapeDtypeStruct((batch, seq, FOX_WIDTH), BF16),
                   jax.ShapeDtypeStruct((batch * seq, GLA_VAL_WIDTH), F32)],
        scratch_shapes=[pltpu.VMEM((n_heads, seq, HEAD_PAIR), BF16),
                        pltpu.VMEM((n_heads, HEAD_PAIR, seq), BF16),
                        pltpu.VMEM((n_heads, FOX_HEAD_DIM + FOX_ONES_ROWS, seq), BF16)],
        compiler_params=_params(1),
        name="mixers",
    )(fq, fk, fv, c, gq, gk, gv, la, ltri)


def _outproj_kernel(h_ref, ofox_ref, ogla_ref, gpre_ref, wgate_ref, gnorm_ref, wbf_ref, wbg_ref,
                    wout_ref, gpost_ref, o_ref):
    subs = [slice(s * OUT_SUB, (s + 1) * OUT_SUB) for s in range(OUT_TM // OUT_SUB)]
    u = [_rms(h_ref[rows, :], gpre_ref[...]).astype(BF16) for rows in subs]
    gates = [_dot_nt(us, wgate_ref[...]) for us in u]
    branch_fox = [_dot(ofox_ref[rows, :], wbf_ref[...]) for rows in subs]
    branch_gla = []
    for s, rows in enumerate(subs):
        g_r = gates[s][:, :GLA_VAL_WIDTH]
        heads = []
        for hd in range(GLA_HEADS):
            sl = slice(hd * GLA_DV, (hd + 1) * GLA_DV)
            heads.append(_rms(ogla_ref[rows, sl], gnorm_ref[:, sl]))
        o_gla = jnp.concatenate(heads, axis=1)
        o_gla = (o_gla * (g_r * jax.nn.sigmoid(g_r))).astype(BF16)
        branch_gla.append(_dot(o_gla, wbg_ref[...]))
    merged = []
    for s in range(len(subs)):
        gate_fox = gates[s][:, GLA_VAL_WIDTH:GLA_VAL_WIDTH + D_MODEL]
        gate_gla = gates[s][:, GLA_VAL_WIDTH + D_MODEL:]
        y = jax.nn.sigmoid(gate_fox) * branch_fox[s] + jax.nn.sigmoid(gate_gla) * branch_gla[s]
        merged.append(_dot(y.astype(BF16), wout_ref[...]))
    for s, rows in enumerate(subs):
        o_ref[rows, :] = h_ref[rows, :] + _rms(merged[s], gpost_ref[...])


def _outproj(h2d, o_fox, o_gla, g_pre, wgate, gnorm, wbf, wbg, wout, g_post):
    t = h2d.shape[0]
    row = lambda w: pl.BlockSpec((OUT_TM, w), lambda i: (i, 0))
    return pl.pallas_call(
        _outproj_kernel,
        grid=(t // OUT_TM,),
        in_specs=[row(D_MODEL), row(FOX_WIDTH), row(GLA_VAL_WIDTH), _const_spec((1, D_MODEL)),
                  _const_spec((GLA_VAL_WIDTH + 2 * D_MODEL, D_MODEL)),
                  _const_spec((1, GLA_VAL_WIDTH)),
                  _const_spec((FOX_WIDTH, D_MODEL)), _const_spec((GLA_VAL_WIDTH, D_MODEL)),
                  _const_spec((D_MODEL, D_MODEL)), _const_spec((1, D_MODEL))],
        out_specs=row(D_MODEL),
        out_shape=jax.ShapeDtypeStruct((t, D_MODEL), F32),
        compiler_params=_params(1),
        name="outproj",
    )(h2d, o_fox, o_gla, g_pre, wgate, gnorm, wbf, wbg, wout, g_post)


def _chunk_tril(n):
    r = jnp.arange(n)[:, None]
    c = jnp.arange(n)[None, :]
    return (((r // GLA_CHUNK) == (c // GLA_CHUNK)) & (r >= c)).astype(BF16)


def _layer(h2d, batch, seq, ffn1_pre_g, ffn1_w_gate, ffn1_w_up, ffn1_w_down, ffn1_post_g,
           mix_pre_g, w_in, b_forget, w_alpha_up, b_alpha, gla_norm_g, w_branch_fox,
           w_branch_gla, w_out, mix_post_g, ffn2_pre_g, ffn2_w_gate, ffn2_w_up, ffn2_w_down,
           ffn2_post_g):
    vec = lambda g: g.reshape(1, -1)
    bf = lambda w: w.astype(BF16)

    splits = [FOX_WIDTH, FOX_WIDTH, FOX_WIDTH, FOX_HEADS, GLA_KEY_WIDTH, GLA_KEY_WIDTH,
              GLA_VAL_WIDTH, GLA_GATE_RANK, GLA_VAL_WIDTH, D_MODEL, D_MODEL]
    offs = [0]
    for s in splits:
        offs.append(offs[-1] + s)
    w_in_t = w_in.T
    col = lambda a, b: w_in_t[offs[a]:offs[b], :]
    wfox = bf(col(0, 3))
    wgla = bf(col(4, 7))
    pad_cols = SMALL_W - FOX_HEADS - GLA_GATE_RANK
    wsm = bf(jnp.concatenate([col(3, 4), col(7, 8), jnp.zeros((pad_cols, D_MODEL), F32)], axis=0))
    bsm = jnp.concatenate([b_forget, jnp.zeros((SMALL_W - FOX_HEADS,), F32)]).reshape(1, SMALL_W)
    wa = bf(jnp.concatenate([jnp.zeros((FOX_HEADS, GLA_KEY_WIDTH), F32), w_alpha_up,
                             jnp.zeros((pad_cols, GLA_KEY_WIDTH), F32)], axis=0))
    wgate = bf(col(8, 11))
    tok = jnp.arange(LANES)
    upper = (tok[:, None] <= tok[None, :]).astype(BF16)
    ltri = _chunk_tril(TT)

    h1, ffn2_w = _ffn(h2d, vec(ffn1_pre_g), bf(ffn1_w_gate), bf(ffn1_w_up), bf(ffn1_w_down),
                      vec(ffn1_post_g), cast=(ffn2_w_gate, ffn2_w_up, ffn2_w_down))

    fq, fk, fv, c, gq, gk, gv, la = _inproj(h1, seq, vec(mix_pre_g), wfox, wgla, wsm, bsm, wa,
                                            vec(b_alpha), upper)
    shape3 = (batch, seq, FOX_WIDTH)
    o_fox, o_gla = _mixers(fq.reshape(shape3), fk.reshape(shape3), fv.reshape(shape3),
                           c.reshape(batch, FOX_HEADS // 2, 2, seq), gq, gk, gv, la, ltri)
    o_fox = o_fox.reshape(batch * seq, FOX_WIDTH)

    h2 = _outproj(h1, o_fox, o_gla, vec(mix_pre_g), wgate, vec(gla_norm_g), bf(w_branch_fox),
                  bf(w_branch_gla), bf(w_out), vec(mix_post_g))

    return _ffn(h2, vec(ffn2_pre_g), *ffn2_w, vec(ffn2_post_g))[0]


def kernel(x, ffn1_pre_g, ffn1_w_gate, ffn1_w_up, ffn1_w_down, ffn1_post_g, mix_pre_g, w_in,
           b_forget, w_alpha_up, b_alpha, gla_norm_g, w_branch_fox, w_branch_gla, w_out,
           mix_post_g, ffn2_pre_g, ffn2_w_gate, ffn2_w_up, ffn2_w_down, ffn2_post_g):
    batch, seq, d = x.shape
    h = x.reshape(batch * seq, d)
    depth = ffn1_pre_g.shape[0]
    for l in range(depth):
        h = _layer(h, batch, seq, ffn1_pre_g[l], ffn1_w_gate[l], ffn1_w_up[l], ffn1_w_down[l],
                   ffn1_post_g[l], mix_pre_g[l], w_in[l], b_forget[l], w_alpha_up[l], b_alpha[l],
                   gla_norm_g[l], w_branch_fox[l], w_branch_gla[l], w_out[l], mix_post_g[l],
                   ffn2_pre_g[l], ffn2_w_gate[l], ffn2_w_up[l], ffn2_w_down[l], ffn2_post_g[l])
    return h.reshape(batch, seq, d)
```

```python
import functools

import jax
import jax.numpy as jnp
from jax import lax
from jax.experimental import pallas as pl
from jax.experimental.pallas import tpu as pltpu

F32 = jnp.float32
BF16 = jnp.bfloat16

D_MODEL = 1024
D_FF = 2816
FOX_HEADS = 8
FOX_HEAD_DIM = 64
FOX_WIDTH = FOX_HEADS * FOX_HEAD_DIM
GLA_HEADS = 4
GLA_DK = 64
GLA_DV = 128
GLA_KEY_WIDTH = GLA_HEADS * GLA_DK
GLA_VAL_WIDTH = GLA_HEADS * GLA_DV
GLA_GATE_RANK = 16
GLA_TAU = 16.0
GLA_CHUNK = 64
NORM_EPS = 1e-6

LANES = 128
HEAD_PAIR = LANES
VMEM_LIMIT_BYTES = 56 * 1024 * 1024

IN_TM = 1024
IN_SUB = 512
FFN_TM = 1024
FF_CHUNK = 256
OUT_TM = 1024
OUT_SUB = 512
FFN_SUB = 512
TQ = 256
FOX_WIDE = 512
FOX_ONES_ROWS = 16
FOX_PAIRS = 4
FOX_GROUP = 8
FOX_AHEAD = 4
LOG2_E = 1.4426950408889634
TT = 256
SMALL_W = LANES
CUM_ROWS = 16


def _rms(x, g):
    return x * lax.rsqrt(jnp.mean(x * x, axis=-1, keepdims=True) + NORM_EPS) * g


def _log_sigmoid(x):
    return jnp.minimum(x, 0.0) - jnp.log(1.0 + jnp.exp(-jnp.abs(x)))


def _dot(a, b):
    return jnp.dot(a, b, preferred_element_type=F32)


def _dot_nt(a, b):
    return lax.dot_general(a, b, (((1,), (1,)), ((), ())), preferred_element_type=F32)


def _dot_tn(a, b):
    return lax.dot_general(a, b, (((0,), (0,)), ((), ())), preferred_element_type=F32)


def _split2(x):
    hi = x.astype(BF16)
    lo = (x - hi.astype(F32)).astype(BF16)
    return hi, lo


def _split3(x):
    hi = x.astype(BF16)
    r = x - hi.astype(F32)
    mid = r.astype(BF16)
    lo = (r - mid.astype(F32)).astype(BF16)
    return hi, mid, lo


def _const_spec(shape):
    return pl.BlockSpec(shape, lambda *_: (0,) * len(shape), pipeline_mode=pl.Buffered(1))


def _params(n_axes):
    return pltpu.CompilerParams(dimension_semantics=("arbitrary",) * n_axes,
                                vmem_limit_bytes=VMEM_LIMIT_BYTES)


def _ffn_kernel(n_cast, x_ref, gpre_ref, wg_ref, wu_ref, wd_ref, gpost_ref, *rest):
    cast_in, o_ref, cast_out = rest[:n_cast], rest[n_cast], rest[n_cast + 1:]
    subs = [slice(s * FFN_SUB, (s + 1) * FFN_SUB) for s in range(FFN_TM // FFN_SUB)]
    xn = [_rms(x_ref[rows, :], gpre_ref[...]).astype(BF16) for rows in subs]
    acc = [None] * len(subs)
    for c in range(D_FF // FF_CHUNK):
        cols = slice(c * FF_CHUNK, (c + 1) * FF_CHUNK)
        for s in range(len(subs)):
            g = _dot(xn[s], wg_ref[:, cols])
            u = _dot(xn[s], wu_ref[:, cols])
            h = (g * jax.nn.sigmoid(g) * u).astype(BF16)
            part = _dot(h, wd_ref[cols, :])
            acc[s] = part if acc[s] is None else acc[s] + part
    for s, rows in enumerate(subs):
        o_ref[rows, :] = x_ref[rows, :] + 0.5 * _rms(acc[s], gpost_ref[...])
    for src, dst in zip(cast_in, cast_out):
        dst[...] = src[...].astype(BF16)


def _cast_spec(shape, steps):
    rows, cols = shape
    span = 1 if (rows // steps) % 16 == 0 else 2
    assert rows % steps == 0 and (span * rows // steps) % 16 == 0
    return pl.BlockSpec((span * rows // steps, cols), lambda i: (i // span, 0))


def _ffn(x2d, g_pre, w_gate, w_up, w_down, g_post, cast=()):
    t = x2d.shape[0]
    steps = t // FFN_TM
    row = pl.BlockSpec((FFN_TM, D_MODEL), lambda i: (i, 0))
    cast_specs = [_cast_spec(w.shape, steps) for w in cast]
    out = pl.pallas_call(
        functools.partial(_ffn_kernel, len(cast)),
        grid=(steps,),
        in_specs=[row, _const_spec((1, D_MODEL)), _const_spec((D_MODEL, D_FF)),
                  _const_spec((D_MODEL, D_FF)), _const_spec((D_FF, D_MODEL)),
                  _const_spec((1, D_MODEL))] + cast_specs,
        out_specs=[row] + cast_specs,
        out_shape=[jax.ShapeDtypeStruct((t, D_MODEL), F32)]
        + [jax.ShapeDtypeStruct(w.shape, BF16) for w in cast],
        compiler_params=_params(1),
        name="ffn",
    )(x2d, g_pre, w_gate, w_up, w_down, g_post, *cast)
    return out[0], tuple(out[1:])


def _inproj_kernel(tiles_per_seq, h_ref, g_ref, wfox_ref, wgla_ref, wsm_ref, bsm_ref, wa_ref,
                   ba_ref, upper_ref, fq_ref, fk_ref, fv_ref, c_ref, gq_ref, gk_ref, gv_ref,
                   la_ref, carry_ref):
    @pl.when(pl.program_id(0) % tiles_per_seq == 0)
    def _():
        carry_ref[...] = jnp.zeros_like(carry_ref)

    subs = [slice(s * IN_SUB, (s + 1) * IN_SUB) for s in range(IN_TM // IN_SUB)]
    scale = FOX_HEAD_DIM ** -0.5 * LOG2_E
    u = [_rms(h_ref[rows, :], g_ref[...]).astype(BF16) for rows in subs]

    zs = [_dot_nt(us, wsm_ref[...]) for us in u]

    for s, rows in enumerate(subs):
        zf = _dot_nt(u[s], wfox_ref[...])
        fq_ref[rows, :] = (zf[:, :FOX_WIDTH] * scale).astype(BF16)
        fk_ref[rows, :] = zf[:, FOX_WIDTH:2 * FOX_WIDTH].astype(BF16)
        fv_ref[rows, :] = zf[:, 2 * FOX_WIDTH:].astype(BF16)

    for s, rows in enumerate(subs):
        a = _dot(zs[s].astype(BF16), wa_ref[...]) + ba_ref[...]
        la_ref[rows, :] = _log_sigmoid(a) * (1.0 / GLA_TAU)

    for s, rows in enumerate(subs):
        zg = _dot_nt(u[s], wgla_ref[...])
        gq_ref[rows, :] = zg[:, :GLA_KEY_WIDTH] * (GLA_DK ** -0.5)
        gk_ref[rows, :] = zg[:, GLA_KEY_WIDTH:2 * GLA_KEY_WIDTH]
        gv_ref[rows, :] = zg[:, 2 * GLA_KEY_WIDTH:].astype(BF16)

    upper = upper_ref[...]
    carry = carry_ref[...]
    n_blocks = IN_SUB // LANES
    for s, rows in enumerate(subs):
        lf_t = _log_sigmoid(zs[s] + bsm_ref[...]).T[:CUM_ROWS]
        parts = jnp.concatenate(_split3(lf_t), axis=0)
        stacked = jnp.concatenate(
            [parts[:, k * LANES:(k + 1) * LANES] for k in range(n_blocks)], axis=0)
        sums = _dot(stacked, upper)
        c_blocks = []
        for k in range(n_blocks):
            blk = sums[k * 3 * CUM_ROWS:(k + 1) * 3 * CUM_ROWS]
            c_blocks.append(blk[:CUM_ROWS] + blk[CUM_ROWS:2 * CUM_ROWS] + blk[2 * CUM_ROWS:] + carry)
            carry = carry + jnp.sum(lf_t[:, k * LANES:(k + 1) * LANES], axis=1, keepdims=True)
        c_ref[0, :, rows] = jnp.concatenate(c_blocks, axis=1)[:FOX_HEADS] * LOG2_E
    carry_ref[...] = carry


def _inproj(h2d, seq, g, wfox, wgla, wsm, bsm, wa, ba, upper):
    t = h2d.shape[0]
    batch = t // seq
    tiles_per_seq = seq // IN_TM
    row = lambda w: pl.BlockSpec((IN_TM, w), lambda i: (i, 0))
    out_shapes = (
        jax.ShapeDtypeStruct((t, FOX_WIDTH), BF16),
        jax.ShapeDtypeStruct((t, FOX_WIDTH), BF16),
        jax.ShapeDtypeStruct((t, FOX_WIDTH), BF16),
        jax.ShapeDtypeStruct((batch, FOX_HEADS, seq), F32),
        jax.ShapeDtypeStruct((t, GLA_KEY_WIDTH), F32),
        jax.ShapeDtypeStruct((t, GLA_KEY_WIDTH), F32),
        jax.ShapeDtypeStruct((t, GLA_VAL_WIDTH), BF16),
        jax.ShapeDtypeStruct((t, GLA_KEY_WIDTH), F32),
    )
    out_specs = (
        row(FOX_WIDTH), row(FOX_WIDTH), row(FOX_WIDTH),
        pl.BlockSpec((1, FOX_HEADS, IN_TM), lambda i: (i // tiles_per_seq, 0, i % tiles_per_seq)),
        row(GLA_KEY_WIDTH), row(GLA_KEY_WIDTH), row(GLA_VAL_WIDTH), row(GLA_KEY_WIDTH),
    )
    return pl.pallas_call(
        functools.partial(_inproj_kernel, tiles_per_seq),
        grid=(t // IN_TM,),
        in_specs=[row(D_MODEL), _const_spec((1, D_MODEL)),
                  _const_spec((3 * FOX_WIDTH, D_MODEL)),
                  _const_spec((2 * GLA_KEY_WIDTH + GLA_VAL_WIDTH, D_MODEL)),
                  _const_spec((SMALL_W, D_MODEL)), _const_spec((1, SMALL_W)),
                  _const_spec((SMALL_W, GLA_KEY_WIDTH)), _const_spec((1, GLA_KEY_WIDTH)),
                  _const_spec((LANES, LANES))],
        out_specs=out_specs,
        out_shape=out_shapes,
        scratch_shapes=[pltpu.VMEM((CUM_ROWS, LANES), F32)],
        compiler_params=_params(1),
        name="inproj",
    )(h2d, g, wfox, wgla, wsm, bsm, wa, ba, upper)


def _fox_kernel(seq, q_ref, k_ref, v_ref, c_ref, o_ref, kaug_ref, qaug_ref, vt_ref):
    lane = lax.broadcasted_iota(jnp.int32, (1, HEAD_PAIR), 1)
    row = lax.broadcasted_iota(jnp.int32, (HEAD_PAIR, 1), 0)
    n_heads = 2 * FOX_PAIRS
    for pair in range(FOX_PAIRS):
        pair_lanes = slice(pair * HEAD_PAIR, (pair + 1) * HEAD_PAIR)
        k_all = k_ref[0, :, pair_lanes]
        q_t = q_ref[0, :, pair_lanes].astype(F32).T
        v_t = v_ref[0, :, pair_lanes].astype(F32).T
        for j in range(2):
            h = 2 * pair + j
            lo_lane, hi_lane = j * FOX_HEAD_DIM, (j + 1) * FOX_HEAD_DIM
            bias0 = (1 - j) * FOX_HEAD_DIM
            parts = [p.astype(F32) for p in _split3(-c_ref[0, pair, j:j + 1, :])]
            bias_t = jnp.zeros((HEAD_PAIR, seq), F32)
            for i, part in enumerate(parts):
                bias_t = jnp.where(row == bias0 + i, part, bias_t)
            own_lane = (lane >= lo_lane) & (lane < hi_lane)
            kaug_ref[h] = jnp.where(own_lane, k_all, bias_t.T.astype(BF16))
            own_row = (row >= lo_lane) & (row < hi_lane)
            one_row = (row >= bias0) & (row < bias0 + len(parts))
            qaug_ref[h] = jnp.where(own_row, q_t, jnp.where(one_row, 1.0, 0.0)).astype(BF16)
            vt_ref[h] = jnp.concatenate(
                [v_t[lo_lane:hi_lane], jnp.ones((FOX_ONES_ROWS, seq), F32)], axis=0).astype(BF16)

    chains = []
    for qi in reversed(range(seq // TQ)):
        q0 = qi * TQ
        for h in range(n_heads):
            steps, k0 = [], 0
            while k0 + FOX_WIDE <= q0:
                steps.append((k0, FOX_WIDE, False))
                k0 += FOX_WIDE
            if k0 < q0:
                steps.append((k0, q0 - k0, False))
            steps.append((q0, TQ, True))
            chains.append((q0, h, steps))
    items = []
    for g in range(0, len(chains), FOX_GROUP):
        group = chains[g:g + FOX_GROUP]
        for t in range(max(len(c[2]) for c in group)):
            for q0, h, steps in group:
                if t < len(steps):
                    items.append((q0, h) + steps[t])

    def scores(item):
        q0, h, k0, width, _ = item
        return _dot(kaug_ref[h, k0:k0 + width, :], qaug_ref[h, :, q0:q0 + TQ])

    state, out_t = {}, {}
    pending = [scores(item) for item in items[:FOX_AHEAD]]
    for t, item in enumerate(items):
        q0, h, k0, width, diagonal = item
        s_t = pending.pop(0)
        if t + FOX_AHEAD < len(items):
            pending.append(scores(items[t + FOX_AHEAD]))
        m, acc = state.get((q0, h), (jnp.full((1, TQ), -jnp.inf, F32),
                                     jnp.zeros((FOX_HEAD_DIM + FOX_ONES_ROWS, TQ), F32)))
        if diagonal:
            key = lax.broadcasted_iota(jnp.int32, (width, TQ), 0) + k0
            qry = lax.broadcasted_iota(jnp.int32, (width, TQ), 1) + q0
            s_t = jnp.where(key <= qry, s_t, -jnp.inf)
        m_next = jnp.maximum(m, jnp.max(s_t, axis=0, keepdims=True))
        p_t = jnp.exp2(s_t - m_next)
        acc = acc * jnp.exp2(m - m_next) + _dot(vt_ref[h, :, k0:k0 + width], p_t.astype(BF16))
        state[(q0, h)] = (m_next, acc)
        if diagonal:
            denom = acc[FOX_HEAD_DIM:FOX_HEAD_DIM + 8]
            out_t[(q0, h)] = acc[:FOX_HEAD_DIM] / jnp.concatenate(
                [denom] * (FOX_HEAD_DIM // 8), axis=0)
            if (q0, h ^ 1) in out_t:
                pair = h // 2
                both = jnp.concatenate([out_t[(q0, 2 * pair)], out_t[(q0, 2 * pair + 1)]], axis=0)
                o_ref[0, q0:q0 + TQ, pair * HEAD_PAIR:(pair + 1) * HEAD_PAIR] = both.T.astype(BF16)


def _gla_kernel(seq, q_ref, k_ref, v_ref, la_ref, ltri_ref, o_ref):
    n_chunks = TT // GLA_CHUNK
    lane = lax.broadcasted_iota(jnp.int32, (1, HEAD_PAIR), 1)
    head_mask = (lane < GLA_DK, lane >= GLA_DK)
    row_chunk = lax.broadcasted_iota(jnp.int32, (TT, 1), 0) // GLA_CHUNK
    rr = lax.broadcasted_iota(jnp.int32, (TT, TT), 0)
    cc = lax.broadcasted_iota(jnp.int32, (TT, TT), 1)
    intra = (rr >= cc) & (rr // GLA_CHUNK == cc // GLA_CHUNK)
    ltri = ltri_ref[...]

    n_pairs = GLA_HEADS // 2
    n_tiles = seq // TT

    def rows_of(t):
        return slice(t * TT, (t + 1) * TT)

    def cumulate(t):
        hi, lo = _split2(la_ref[rows_of(t), :])
        return _dot(ltri, hi) + _dot(ltri, lo)

    def products(t, b_all):
        rows = rows_of(t)
        out = []
        for p in range(n_pairs):
            ksl = slice(p * HEAD_PAIR, (p + 1) * HEAD_PAIR)
            b = b_all[:, ksl]
            last_rows = [b[(i + 1) * GLA_CHUNK - 1:(i + 1) * GLA_CHUNK, :] for i in range(n_chunks)]
            b_last = jnp.concatenate(
                [jnp.broadcast_to(r, (GLA_CHUNK, HEAD_PAIR)) for r in last_rows], axis=0)
            q_in = (q_ref[rows, ksl] * jnp.exp(b)).astype(BF16)
            k = k_ref[rows, ksl]
            k_in = (k * jnp.exp(-b)).astype(BF16)
            k_st = (k * jnp.exp(b_last - b)).astype(BF16)
            kv = None
            attn = []
            for j in range(2):
                h = 2 * p + j
                vj = v_ref[rows, h * GLA_DV:(h + 1) * GLA_DV]
                kcat = jnp.concatenate(
                    [jnp.where((row_chunk == i) & head_mask[j], k_st, jnp.zeros_like(k_st))
                     for i in range(n_chunks)], axis=1)
                contrib = _dot_tn(vj, kcat)
                kv = contrib if kv is None else kv + contrib
                qh = jnp.where(head_mask[j], q_in, jnp.zeros_like(q_in))
                attn.append(_dot_nt(qh, k_in))
            out.append((q_in, [jnp.exp(r) for r in last_rows], kv, attn))
        return out

    state = [jnp.zeros((GLA_DV, HEAD_PAIR), F32) for _ in range(n_pairs)]

    def finish(t, prods):
        rows = rows_of(t)
        for p in range(n_pairs):
            q_in, decays, kv, attn = prods[p]
            o_inter = []
            for i in range(n_chunks):
                st = state[p].astype(BF16)
                st2 = jnp.concatenate(
                    [jnp.where(head_mask[j], st, jnp.zeros_like(st)) for j in range(2)], axis=0)
                o_inter.append(_dot_nt(q_in[i * GLA_CHUNK:(i + 1) * GLA_CHUNK, :], st2))
                state[p] = state[p] * decays[i] + kv[:, i * HEAD_PAIR:(i + 1) * HEAD_PAIR]
            o_intra = []
            for j in range(2):
                h = 2 * p + j
                vj = v_ref[rows, h * GLA_DV:(h + 1) * GLA_DV]
                o_intra.append(_dot(jnp.where(intra, attn[j], 0.0).astype(BF16), vj))
            o_ref[rows, 2 * p * GLA_DV:(2 * p + 2) * GLA_DV] = (
                jnp.concatenate(o_intra, axis=1) + jnp.concatenate(o_inter, axis=0))

    b_alls, prods = {}, {}
    for step in range(n_tiles + 2):
        if step < n_tiles:
            b_alls[step] = cumulate(step)
        if 0 <= step - 1 < n_tiles:
            prods[step - 1] = products(step - 1, b_alls.pop(step - 1))
        if 0 <= step - 2 < n_tiles:
            finish(step - 2, prods.pop(step - 2))


def _mixers_kernel(seq, fq_ref, fk_ref, fv_ref, c_ref, gq_ref, gk_ref, gv_ref, la_ref, ltri_ref,
                   ofox_ref, ogla_ref, kaug_ref, qaug_ref, vt_ref):
    _fox_kernel(seq, fq_ref, fk_ref, fv_ref, c_ref, ofox_ref, kaug_ref, qaug_ref, vt_ref)
    _gla_kernel(seq, gq_ref, gk_ref, gv_ref, la_ref, ltri_ref, ogla_ref)


def _mixers(fq, fk, fv, c, gq, gk, gv, la, ltri):
    batch, seq, _ = fq.shape
    assert FOX_PAIRS * HEAD_PAIR == FOX_WIDTH
    n_heads = 2 * FOX_PAIRS
    qkv = pl.BlockSpec((1, seq, FOX_WIDTH), lambda b: (b, 0, 0))
    row = lambda w: pl.BlockSpec((seq, w), lambda b: (b, 0))
    return pl.pallas_call(
        functools.partial(_mixers_kernel, seq),
        grid=(batch,),
        in_specs=[qkv, qkv, qkv, pl.BlockSpec((1, FOX_PAIRS, 2, seq), lambda b: (b, 0, 0, 0)),
                  row(GLA_KEY_WIDTH), row(GLA_KEY_WIDTH), row(GLA_VAL_WIDTH), row(GLA_KEY_WIDTH),
                  _const_spec((TT, TT))],
        out_specs=[qkv, row(GLA_VAL_WIDTH)],
        out_shape=[jax.ShapeDtypeStruct((batch, seq, FOX_WIDTH), BF16),
                   jax.ShapeDtypeStruct((batch * seq, GLA_VAL_WIDTH), F32)],
        scratch_shapes=[pltpu.VMEM((n_heads, seq, HEAD_PAIR), BF16),
                        pltpu.VMEM((n_heads, HEAD_PAIR, seq), BF16),
                        pltpu.VMEM((n_heads, FOX_HEAD_DIM + FOX_ONES_ROWS, seq), BF16)],
        compiler_params=_params(1),
        name="mixers",
    )(fq, fk, fv, c, gq, gk, gv, la, ltri)


def _outproj_kernel(h_ref, ofox_ref, ogla_ref, gpre_ref, wgate_ref, gnorm_ref, wbf_ref, wbg_ref,
                    wout_ref, gpost_ref, o_ref):
    subs = [slice(s * OUT_SUB, (s + 1) * OUT_SUB) for s in range(OUT_TM // OUT_SUB)]
    u = [_rms(h_ref[rows, :], gpre_ref[...]).astype(BF16) for rows in subs]
    gates = [_dot_nt(us, wgate_ref[...]) for us in u]
    branch_fox = [_dot(ofox_ref[rows, :], wbf_ref[...]) for rows in subs]
    branch_gla = []
    for s, rows in enumerate(subs):
        g_r = gates[s][:, :GLA_VAL_WIDTH]
        heads = []
        for hd in range(GLA_HEADS):
            sl = slice(hd * GLA_DV, (hd + 1) * GLA_DV)
            heads.append(_rms(ogla_ref[rows, sl], gnorm_ref[:, sl]))
        o_gla = jnp.concatenate(heads, axis=1)
        o_gla = (o_gla * (g_r * jax.nn.sigmoid(g_r))).astype(BF16)
        branch_gla.append(_dot(o_gla, wbg_ref[...]))
    merged = []
    for s in range(len(subs)):
        gate_fox = gates[s][:, GLA_VAL_WIDTH:GLA_VAL_WIDTH + D_MODEL]
        gate_gla = gates[s][:, GLA_VAL_WIDTH + D_MODEL:]
        y = jax.nn.sigmoid(gate_fox) * branch_fox[s] + jax.nn.sigmoid(gate_gla) * branch_gla[s]
        merged.append(_dot(y.astype(BF16), wout_ref[...]))
    for s, rows in enumerate(subs):
        o_ref[rows, :] = h_ref[rows, :] + _rms(merged[s], gpost_ref[...])


def _outproj(h2d, o_fox, o_gla, g_pre, wgate, gnorm, wbf, wbg, wout, g_post):
    t = h2d.shape[0]
    row = lambda w: pl.BlockSpec((OUT_TM, w), lambda i: (i, 0))
    return pl.pallas_call(
        _outproj_kernel,
        grid=(t // OUT_TM,),
        in_specs=[row(D_MODEL), row(FOX_WIDTH), row(GLA_VAL_WIDTH), _const_spec((1, D_MODEL)),
                  _const_spec((GLA_VAL_WIDTH + 2 * D_MODEL, D_MODEL)),
                  _const_spec((1, GLA_VAL_WIDTH)),
                  _const_spec((FOX_WIDTH, D_MODEL)), _const_spec((GLA_VAL_WIDTH, D_MODEL)),
                  _const_spec((D_MODEL, D_MODEL)), _const_spec((1, D_MODEL))],
        out_specs=row(D_MODEL),
        out_shape=jax.ShapeDtypeStruct((t, D_MODEL), F32),
        compiler_params=_params(1),
        name="outproj",
    )(h2d, o_fox, o_gla, g_pre, wgate, gnorm, wbf, wbg, wout, g_post)


def _chunk_tril(n):
    r = jnp.arange(n)[:, None]
    c = jnp.arange(n)[None, :]
    return (((r // GLA_CHUNK) == (c // GLA_CHUNK)) & (r >= c)).astype(BF16)


def _layer(h2d, batch, seq, ffn1_pre_g, ffn1_w_gate, ffn1_w_up, ffn1_w_down, ffn1_post_g,
           mix_pre_g, w_in, b_forget, w_alpha_up, b_alpha, gla_norm_g, w_branch_fox,
           w_branch_gla, w_out, mix_post_g, ffn2_pre_g, ffn2_w_gate, ffn2_w_up, ffn2_w_down,
           ffn2_post_g):
    vec = lambda g: g.reshape(1, -1)
    bf = lambda w: w.astype(BF16)

    splits = [FOX_WIDTH, FOX_WIDTH, FOX_WIDTH, FOX_HEADS, GLA_KEY_WIDTH, GLA_KEY_WIDTH,
              GLA_VAL_WIDTH, GLA_GATE_RANK, GLA_VAL_WIDTH, D_MODEL, D_MODEL]
    offs = [0]
    for s in splits:
        offs.append(offs[-1] + s)
    w_in_t = w_in.T
    col = lambda a, b: w_in_t[offs[a]:offs[b], :]
    wfox = bf(col(0, 3))
    wgla = bf(col(4, 7))
    pad_cols = SMALL_W - FOX_HEADS - GLA_GATE_RANK
    wsm = bf(jnp.concatenate([col(3, 4), col(7, 8), jnp.zeros((pad_cols, D_MODEL), F32)], axis=0))
    bsm = jnp.concatenate([b_forget, jnp.zeros((SMALL_W - FOX_HEADS,), F32)]).reshape(1, SMALL_W)
    wa = bf(jnp.concatenate([jnp.zeros((FOX_HEADS, GLA_KEY_WIDTH), F32), w_alpha_up,
                             jnp.zeros((pad_cols, GLA_KEY_WIDTH), F32)], axis=0))
    wgate = bf(col(8, 11))
    tok = jnp.arange(LANES)
    upper = (tok[:, None] <= tok[None, :]).astype(BF16)
    ltri = _chunk_tril(TT)

    h1, ffn2_w = _ffn(h2d, vec(ffn1_pre_g), bf(ffn1_w_gate), bf(ffn1_w_up), bf(ffn1_w_down),
                      vec(ffn1_post_g), cast=(ffn2_w_gate, ffn2_w_up, ffn2_w_down))

    fq, fk, fv, c, gq, gk, gv, la = _inproj(h1, seq, vec(mix_pre_g), wfox, wgla, wsm, bsm, wa,
                                            vec(b_alpha), upper)
    shape3 = (batch, seq, FOX_WIDTH)
    o_fox, o_gla = _mixers(fq.reshape(shape3), fk.reshape(shape3), fv.reshape(shape3),
                           c.reshape(batch, FOX_HEADS // 2, 2, seq), gq, gk, gv, la, ltri)
    o_fox = o_fox.reshape(batch * seq, FOX_WIDTH)

    h2 = _outproj(h1, o_fox, o_gla, vec(mix_pre_g), wgate, vec(gla_norm_g), bf(w_branch_fox),
                  bf(w_branch_gla), bf(w_out), vec(mix_post_g))

    return _ffn(h2, vec(ffn2_pre_g), *ffn2_w, vec(ffn2_post_g))[0]


def kernel(x, ffn1_pre_g, ffn1_w_gate, ffn1_w_up, ffn1_w_down, ffn1_post_g, mix_pre_g, w_in,
           b_forget, w_alpha_up, b_alpha, gla_norm_g, w_branch_fox, w_branch_gla, w_out,
           mix_post_g, ffn2_pre_g, ffn2_w_gate, ffn2_w_up, ffn2_w_down, ffn2_post_g):
    batch, seq, d = x.shape
    h = x.reshape(batch * seq, d)
    depth = ffn1_pre_g.shape[0]
    for l in range(depth):
        h = _layer(h, batch, seq, ffn1_pre_g[l], ffn1_w_gate[l], ffn1_w_up[l], ffn1_w_down[l],
                   ffn1_post_g[l], mix_pre_g[l], w_in[l], b_forget[l], w_alpha_up[l], b_alpha[l],
                   gla_norm_g[l], w_branch_fox[l], w_branch_gla[l], w_out[l], mix_post_g[l],
                   ffn2_pre_g[l], ffn2_w_gate[l], ffn2_w_up[l], ffn2_w_down[l], ffn2_post_g[l])
    return h.reshape(batch, seq, d)
```

```python
import functools

import jax
import jax.numpy as jnp
from jax import lax
from jax.experimental import pallas as pl
from jax.experimental.pallas import tpu as pltpu

F32 = jnp.float32
BF16 = jnp.bfloat16

D_MODEL = 1024
D_FF = 2816
FOX_HEADS = 8
FOX_HEAD_DIM = 64
FOX_WIDTH = FOX_HEADS * FOX_HEAD_DIM
GLA_HEADS = 4
GLA_DK = 64
GLA_DV = 128
GLA_KEY_WIDTH = GLA_HEADS * GLA_DK
GLA_VAL_WIDTH = GLA_HEADS * GLA_DV
GLA_GATE_RANK = 16
GLA_TAU = 16.0
GLA_CHUNK = 64
NORM_EPS = 1e-6

LANES = 128
HEAD_PAIR = LANES
VMEM_LIMIT_BYTES = 56 * 1024 * 1024

IN_TM = 1024
IN_SUB = 512
FFN_TM = 1024
FF_CHUNK = 256
OUT_TM = 1024
OUT_SUB = 512
FFN_SUB = 512
TQ = 256
FOX_WIDE = 512
FOX_ONES_ROWS = 16
FOX_PAIRS = 4
FOX_GROUP = 8
FOX_AHEAD = 4
LOG2_E = 1.4426950408889634
TT = 256
SMALL_W = LANES
CUM_ROWS = 16


def _rms(x, g):
    return x * lax.rsqrt(jnp.mean(x * x, axis=-1, keepdims=True) + NORM_EPS) * g


def _log_sigmoid(x):
    return jnp.minimum(x, 0.0) - jnp.log(1.0 + jnp.exp(-jnp.abs(x)))


def _dot(a, b):
    return jnp.dot(a, b, preferred_element_type=F32)


def _dot_nt(a, b):
    return lax.dot_general(a, b, (((1,), (1,)), ((), ())), preferred_element_type=F32)


def _dot_tn(a, b):
    return lax.dot_general(a, b, (((0,), (0,)), ((), ())), preferred_element_type=F32)


def _split2(x):
    hi = x.astype(BF16)
    lo = (x - hi.astype(F32)).astype(BF16)
    return hi, lo


def _split3(x):
    hi = x.astype(BF16)
    r = x - hi.astype(F32)
    mid = r.astype(BF16)
    lo = (r - mid.astype(F32)).astype(BF16)
    return hi, mid, lo


def _const_spec(shape):
    return pl.BlockSpec(shape, lambda *_: (0,) * len(shape), pipeline_mode=pl.Buffered(1))


def _params(n_axes):
    return pltpu.CompilerParams(dimension_semantics=("arbitrary",) * n_axes,
                                vmem_limit_bytes=VMEM_LIMIT_BYTES)


def _ffn_kernel(n_cast, x_ref, gpre_ref, wg_ref, wu_ref, wd_ref, gpost_ref, *rest):
    cast_in, o_ref, cast_out = rest[:n_cast], rest[n_cast], rest[n_cast + 1:]
    subs = [slice(s * FFN_SUB, (s + 1) * FFN_SUB) for s in range(FFN_TM // FFN_SUB)]
    xn = [_rms(x_ref[rows, :], gpre_ref[...]).astype(BF16) for rows in subs]
    acc = [None] * len(subs)
    for c in range(D_FF // FF_CHUNK):
        cols = slice(c * FF_CHUNK, (c + 1) * FF_CHUNK)
        for s in range(len(subs)):
            g = _dot(xn[s], wg_ref[:, cols])
            u = _dot(xn[s], wu_ref[:, cols])
            h = (g * jax.nn.sigmoid(g) * u).astype(BF16)
            part = _dot(h, wd_ref[cols, :])
            acc[s] = part if acc[s] is None else acc[s] + part
    for s, rows in enumerate(subs):
        o_ref[rows, :] = x_ref[rows, :] + 0.5 * _rms(acc[s], gpost_ref[...])
    for src, dst in zip(cast_in, cast_out):
        dst[...] = src[...].astype(BF16)


def _cast_spec(shape, steps):
    rows, cols = shape
    span = 1 if (rows // steps) % 16 == 0 else 2
    assert rows % steps == 0 and (span * rows // steps) % 16 == 0
    return pl.BlockSpec((span * rows // steps, cols), lambda i: (i // span, 0))


def _ffn(x2d, g_pre, w_gate, w_up, w_down, g_post, cast=()):
    t = x2d.shape[0]
    steps = t // FFN_TM
    row = pl.BlockSpec((FFN_TM, D_MODEL), lambda i: (i, 0))
    cast_specs = [_cast_spec(w.shape, steps) for w in cast]
    out = pl.pallas_call(
        functools.partial(_ffn_kernel, len(cast)),
        grid=(steps,),
        in_specs=[row, _const_spec((1, D_MODEL)), _const_spec((D_MODEL, D_FF)),
                  _const_spec((D_MODEL, D_FF)), _const_spec((D_FF, D_MODEL)),
                  _const_spec((1, D_MODEL))] + cast_specs,
        out_specs=[row] + cast_specs,
        out_shape=[jax.ShapeDtypeStruct((t, D_MODEL), F32)]
        + [jax.ShapeDtypeStruct(w.shape, BF16) for w in cast],
        compiler_params=_params(1),
        name="ffn",
    )(x2d, g_pre, w_gate, w_up, w_down, g_post, *cast)
    return out[0], tuple(out[1:])


def _inproj_kernel(tiles_per_seq, h_ref, g_ref, wfox_ref, wgla_ref, wsm_ref, bsm_ref, wa_ref,
                   ba_ref, upper_ref, fq_ref, fk_ref, fv_ref, c_ref, gq_ref, gk_ref, gv_ref,
                   la_ref, carry_ref):
    @pl.when(pl.program_id(0) % tiles_per_seq == 0)
    def _():
        carry_ref[...] = jnp.zeros_like(carry_ref)

    subs = [slice(s * IN_SUB, (s + 1) * IN_SUB) for s in range(IN_TM // IN_SUB)]
    scale = FOX_HEAD_DIM ** -0.5 * LOG2_E
    u = [_rms(h_ref[rows, :], g_ref[...]).astype(BF16) for rows in subs]

    zs = [_dot_nt(us, wsm_ref[...]) for us in u]

    for s, rows in enumerate(subs):
        zf = _dot_nt(u[s], wfox_ref[...])
        fq_ref[rows, :] = (zf[:, :FOX_WIDTH] * scale).astype(BF16)
        fk_ref[rows, :] = zf[:, FOX_WIDTH:2 * FOX_WIDTH].astype(BF16)
        fv_ref[rows, :] = zf[:, 2 * FOX_WIDTH:].astype(BF16)

    for s, rows in enumerate(subs):
        a = _dot(zs[s].astype(BF16), wa_ref[...]) + ba_ref[...]
        la_ref[rows, :] = _log_sigmoid(a) * (1.0 / GLA_TAU)

    for s, rows in enumerate(subs):
        zg = _dot_nt(u[s], wgla_ref[...])
        gq_ref[rows, :] = zg[:, :GLA_KEY_WIDTH] * (GLA_DK ** -0.5)
        gk_ref[rows, :] = zg[:, GLA_KEY_WIDTH:2 * GLA_KEY_WIDTH]
        gv_ref[rows, :] = zg[:, 2 * GLA_KEY_WIDTH:].astype(BF16)

    upper = upper_ref[...]
    carry = carry_ref[...]
    n_blocks = IN_SUB // LANES
    for s, rows in enumerate(subs):
        lf_t = _log_sigmoid(zs[s] + bsm_ref[...]).T[:CUM_ROWS]
        parts = jnp.concatenate(_split3(lf_t), axis=0)
        stacked = jnp.concatenate(
            [parts[:, k * LANES:(k + 1) * LANES] for k in range(n_blocks)], axis=0)
        sums = _dot(stacked, upper)
        c_blocks = []
        for k in range(n_blocks):
            blk = sums[k * 3 * CUM_ROWS:(k + 1) * 3 * CUM_ROWS]
            c_blocks.append(blk[:CUM_ROWS] + blk[CUM_ROWS:2 * CUM_ROWS] + blk[2 * CUM_ROWS:] + carry)
            carry = carry + jnp.sum(lf_t[:, k * LANES:(k + 1) * LANES], axis=1, keepdims=True)
        c_ref[0, :, rows] = jnp.concatenate(c_blocks, axis=1)[:FOX_HEADS] * LOG2_E
    carry_ref[...] = carry


def _inproj(h2d, seq, g, wfox, wgla, wsm, bsm, wa, ba, upper):
    t = h2d.shape[0]
    batch = t // seq
    tiles_per_seq = seq // IN_TM
    row = lambda w: pl.BlockSpec((IN_TM, w), lambda i: (i, 0))
    out_shapes = (
        jax.ShapeDtypeStruct((t, FOX_WIDTH), BF16),
        jax.ShapeDtypeStruct((t, FOX_WIDTH), BF16),
        jax.ShapeDtypeStruct((t, FOX_WIDTH), BF16),
        jax.ShapeDtypeStruct((batch, FOX_HEADS, seq), F32),
        jax.ShapeDtypeStruct((t, GLA_KEY_WIDTH), F32),
        jax.ShapeDtypeStruct((t, GLA_KEY_WIDTH), F32),
        jax.ShapeDtypeStruct((t, GLA_VAL_WIDTH), BF16),
        jax.ShapeDtypeStruct((t, GLA_KEY_WIDTH), F32),
    )
    out_specs = (
        row(FOX_WIDTH), row(FOX_WIDTH), row(FOX_WIDTH),
        pl.BlockSpec((1, FOX_HEADS, IN_TM), lambda i: (i // tiles_per_seq, 0, i % tiles_per_seq)),
        row(GLA_KEY_WIDTH), row(GLA_KEY_WIDTH), row(GLA_VAL_WIDTH), row(GLA_KEY_WIDTH),
    )
    return pl.pallas_call(
        functools.partial(_inproj_kernel, tiles_per_seq),
        grid=(t // IN_TM,),
        in_specs=[row(D_MODEL), _const_spec((1, D_MODEL)),
                  _const_spec((3 * FOX_WIDTH, D_MODEL)),
                  _const_spec((2 * GLA_KEY_WIDTH + GLA_VAL_WIDTH, D_MODEL)),
                  _const_spec((SMALL_W, D_MODEL)), _const_spec((1, SMALL_W)),
                  _const_spec((SMALL_W, GLA_KEY_WIDTH)), _const_spec((1, GLA_KEY_WIDTH)),
                  _const_spec((LANES, LANES))],
        out_specs=out_specs,
        out_shape=out_shapes,
        scratch_shapes=[pltpu.VMEM((CUM_ROWS, LANES), F32)],
        compiler_params=_params(1),
        name="inproj",
    )(h2d, g, wfox, wgla, wsm, bsm, wa, ba, upper)


def _fox_kernel(seq, q_ref, k_ref, v_ref, c_ref, o_ref, kaug_ref, qaug_ref, vt_ref):
    lane = lax.broadcasted_iota(jnp.int32, (1, HEAD_PAIR), 1)
    row = lax.broadcasted_iota(jnp.int32, (HEAD_PAIR, 1), 0)
    n_heads = 2 * FOX_PAIRS
    for pair in range(FOX_PAIRS):
        pair_lanes = slice(pair * HEAD_PAIR, (pair + 1) * HEAD_PAIR)
        k_all = k_ref[0, :, pair_lanes]
        q_t = q_ref[0, :, pair_lanes].astype(F32).T
        v_t = v_ref[0, :, pair_lanes].astype(F32).T
        for j in range(2):
            h = 2 * pair + j
            lo_lane, hi_lane = j * FOX_HEAD_DIM, (j + 1) * FOX_HEAD_DIM
            bias0 = (1 - j) * FOX_HEAD_DIM
            parts = [p.astype(F32) for p in _split3(-c_ref[0, pair, j:j + 1, :])]
            bias_t = jnp.zeros((HEAD_PAIR, seq), F32)
            for i, part in enumerate(parts):
                bias_t = jnp.where(row == bias0 + i, part, bias_t)
            own_lane = (lane >= lo_lane) & (lane < hi_lane)
            kaug_ref[h] = jnp.where(own_lane, k_all, bias_t.T.astype(BF16))
            own_row = (row >= lo_lane) & (row < hi_lane)
            one_row = (row >= bias0) & (row < bias0 + len(parts))
            qaug_ref[h] = jnp.where(own_row, q_t, jnp.where(one_row, 1.0, 0.0)).astype(BF16)
            vt_ref[h] = jnp.concatenate(
                [v_t[lo_lane:hi_lane], jnp.ones((FOX_ONES_ROWS, seq), F32)], axis=0).astype(BF16)

    chains = []
    for qi in reversed(range(seq // TQ)):
        q0 = qi * TQ
        for h in range(n_heads):
            steps, k0 = [], 0
            while k0 + FOX_WIDE <= q0:
                steps.append((k0, FOX_WIDE, False))
                k0 += FOX_WIDE
            if k0 < q0:
                steps.append((k0, q0 - k0, False))
            steps.append((q0, TQ, True))
            chains.append((q0, h, steps))
    items = []
    for g in range(0, len(chains), FOX_GROUP):
        group = chains[g:g + FOX_GROUP]
        for t in range(max(len(c[2]) for c in group)):
            for q0, h, steps in group:
                if t < len(steps):
                    items.append((q0, h) + steps[t])

    def scores(item):
        q0, h, k0, width, _ = item
        return _dot(kaug_ref[h, k0:k0 + width, :], qaug_ref[h, :, q0:q0 + TQ])

    state, out_t = {}, {}
    pending = [scores(item) for item in items[:FOX_AHEAD]]
    for t, item in enumerate(items):
        q0, h, k0, width, diagonal = item
        s_t = pending.pop(0)
        if t + FOX_AHEAD < len(items):
            pending.append(scores(items[t + FOX_AHEAD]))
        m, acc = state.get((q0, h), (jnp.full((1, TQ), -jnp.inf, F32),
                                     jnp.zeros((FOX_HEAD_DIM + FOX_ONES_ROWS, TQ), F32)))
        if diagonal:
            key = lax.broadcasted_iota(jnp.int32, (width, TQ), 0) + k0
            qry = lax.broadcasted_iota(jnp.int32, (width, TQ), 1) + q0
            s_t = jnp.where(key <= qry, s_t, -jnp.inf)
        m_next = jnp.maximum(m, jnp.max(s_t, axis=0, keepdims=True))
        p_t = jnp.exp2(s_t - m_next)
        acc = acc * jnp.exp2(m - m_next) + _dot(vt_ref[h, :, k0:k0 + width], p_t.astype(BF16))
        state[(q0, h)] = (m_next, acc)
        if diagonal:
            denom = acc[FOX_HEAD_DIM:FOX_HEAD_DIM + 8]
            out_t[(q0, h)] = acc[:FOX_HEAD_DIM] / jnp.concatenate(
                [denom] * (FOX_HEAD_DIM // 8), axis=0)
            if (q0, h ^ 1) in out_t:
                pair = h // 2
                both = jnp.concatenate([out_t[(q0, 2 * pair)], out_t[(q0, 2 * pair + 1)]], axis=0)
                o_ref[0, q0:q0 + TQ, pair * HEAD_PAIR:(pair + 1) * HEAD_PAIR] = both.T.astype(BF16)


def _gla_kernel(seq, q_ref, k_ref, v_ref, la_ref, ltri_ref, gnorm_ref, o_ref):
    n_chunks = TT // GLA_CHUNK
    lane = lax.broadcasted_iota(jnp.int32, (1, HEAD_PAIR), 1)
    head_mask = (lane < GLA_DK, lane >= GLA_DK)
    row_chunk = lax.broadcasted_iota(jnp.int32, (TT, 1), 0) // GLA_CHUNK
    rr = lax.broadcasted_iota(jnp.int32, (TT, TT), 0)
    cc = lax.broadcasted_iota(jnp.int32, (TT, TT), 1)
    intra = (rr >= cc) & (rr // GLA_CHUNK == cc // GLA_CHUNK)
    ltri = ltri_ref[...]

    n_pairs = GLA_HEADS // 2
    n_tiles = seq // TT

    def rows_of(t):
        return slice(t * TT, (t + 1) * TT)

    def cumulate(t):
        hi, lo = _split2(la_ref[rows_of(t), :])
        return _dot(ltri, hi) + _dot(ltri, lo)

    def products(t, b_all):
        rows = rows_of(t)
        out = []
        for p in range(n_pairs):
            ksl = slice(p * HEAD_PAIR, (p + 1) * HEAD_PAIR)
            b = b_all[:, ksl]
            last_rows = [b[(i + 1) * GLA_CHUNK - 1:(i + 1) * GLA_CHUNK, :] for i in range(n_chunks)]
            b_last = jnp.concatenate(
                [jnp.broadcast_to(r, (GLA_CHUNK, HEAD_PAIR)) for r in last_rows], axis=0)
            q_in = (q_ref[rows, ksl] * jnp.exp(b)).astype(BF16)
            k = k_ref[rows, ksl]
            k_in = (k * jnp.exp(-b)).astype(BF16)
            k_st = (k * jnp.exp(b_last - b)).astype(BF16)
            kv = None
            attn = []
            for j in range(2):
                h = 2 * p + j
                vj = v_ref[rows, h * GLA_DV:(h + 1) * GLA_DV]
                kcat = jnp.concatenate(
                    [jnp.where((row_chunk == i) & head_mask[j], k_st, jnp.zeros_like(k_st))
                     for i in range(n_chunks)], axis=1)
                contrib = _dot_tn(vj, kcat)
                kv = contrib if kv is None else kv + contrib
                qh = jnp.where(head_mask[j], q_in, jnp.zeros_like(q_in))
                attn.append(_dot_nt(qh, k_in))
            out.append((q_in, [jnp.exp(r) for r in last_rows], kv, attn))
        return out

    state = [jnp.zeros((GLA_DV, HEAD_PAIR), F32) for _ in range(n_pairs)]

    def finish(t, prods):
        rows = rows_of(t)
        for p in range(n_pairs):
            q_in, decays, kv, attn = prods[p]
            o_inter = []
            for i in range(n_chunks):
                st = state[p].astype(BF16)
                st2 = jnp.concatenate(
                    [jnp.where(head_mask[j], st, jnp.zeros_like(st)) for j in range(2)], axis=0)
                o_inter.append(_dot_nt(q_in[i * GLA_CHUNK:(i + 1) * GLA_CHUNK, :], st2))
                state[p] = state[p] * decays[i] + kv[:, i * HEAD_PAIR:(i + 1) * HEAD_PAIR]
            o_intra = []
            for j in range(2):
                h = 2 * p + j
                vj = v_ref[rows, h * GLA_DV:(h + 1) * GLA_DV]
                o_intra.append(_dot(jnp.where(intra, attn[j], 0.0).astype(BF16), vj))
            o_pair = jnp.concatenate(o_intra, axis=1) + jnp.concatenate(o_inter, axis=0)
            for j in range(2):
                cols = slice((2 * p + j) * GLA_DV, (2 * p + j + 1) * GLA_DV)
                o_ref[rows, cols] = _rms(o_pair[:, j * GLA_DV:(j + 1) * GLA_DV], gnorm_ref[:, cols])

    b_alls, prods = {}, {}
    for step in range(n_tiles + 2):
        if step < n_tiles:
            b_alls[step] = cumulate(step)
        if 0 <= step - 1 < n_tiles:
            prods[step - 1] = products(step - 1, b_alls.pop(step - 1))
        if 0 <= step - 2 < n_tiles:
            finish(step - 2, prods.pop(step - 2))


def _mixers_kernel(seq, fq_ref, fk_ref, fv_ref, c_ref, gq_ref, gk_ref, gv_ref, la_ref, ltri_ref,
                   gnorm_ref, ofox_ref, ogla_ref, kaug_ref, qaug_ref, vt_ref):
    _fox_kernel(seq, fq_ref, fk_ref, fv_ref, c_ref, ofox_ref, kaug_ref, qaug_ref, vt_ref)
    _gla_kernel(seq, gq_ref, gk_ref, gv_ref, la_ref, ltri_ref, gnorm_ref, ogla_ref)


def _mixers(fq, fk, fv, c, gq, gk, gv, la, ltri, gnorm):
    batch, seq, _ = fq.shape
    assert FOX_PAIRS * HEAD_PAIR == FOX_WIDTH
    n_heads = 2 * FOX_PAIRS
    qkv = pl.BlockSpec((1, seq, FOX_WIDTH), lambda b: (b, 0, 0))
    row = lambda w: pl.BlockSpec((seq, w), lambda b: (b, 0))
    return pl.pallas_call(
        functools.partial(_mixers_kernel, seq),
        grid=(batch,),
        in_specs=[qkv, qkv, qkv, pl.BlockSpec((1, FOX_PAIRS, 2, seq), lambda b: (b, 0, 0, 0)),
                  row(GLA_KEY_WIDTH), row(GLA_KEY_WIDTH), row(GLA_VAL_WIDTH), row(GLA_KEY_WIDTH),
                  _const_spec((TT, TT)), _const_spec((1, GLA_VAL_WIDTH))],
        out_specs=[qkv, row(GLA_VAL_WIDTH)],
        out_shape=[jax.ShapeDtypeStruct((batch, seq, FOX_WIDTH), BF16),
                   jax.ShapeDtypeStruct((batch * seq, GLA_VAL_WIDTH), F32)],
        scratch_shapes=[pltpu.VMEM((n_heads, seq, HEAD_PAIR), BF16),
                        pltpu.VMEM((n_heads, HEAD_PAIR, seq), BF16),
                        pltpu.VMEM((n_heads, FOX_HEAD_DIM + FOX_ONES_ROWS, seq), BF16)],
        compiler_params=_params(1),
        name="mixers",
    )(fq, fk, fv, c, gq, gk, gv, la, ltri, gnorm)


def _outproj_kernel(h_ref, ofox_ref, ogla_ref, gpre_ref, wgate_ref, wbf_ref, wbg_ref,
                    wout_ref, gpost_ref, o_ref):
    subs = [slice(s * OUT_SUB, (s + 1) * OUT_SUB) for s in range(OUT_TM // OUT_SUB)]
    u = [_rms(h_ref[rows, :], gpre_ref[...]).astype(BF16) for rows in subs]
    gates = [_dot_nt(us, wgate_ref[...]) for us in u]
    branch_fox = [_dot(ofox_ref[rows, :], wbf_ref[...]) for rows in subs]
    branch_gla = []
    for s, rows in enumerate(subs):
        g_r = gates[s][:, :GLA_VAL_WIDTH]
        o_gla = (ogla_ref[rows, :] * (g_r * jax.nn.sigmoid(g_r))).astype(BF16)
        branch_gla.append(_dot(o_gla, wbg_ref[...]))
    merged = []
    for s in range(len(subs)):
        gate_fox = gates[s][:, GLA_VAL_WIDTH:GLA_VAL_WIDTH + D_MODEL]
        gate_gla = gates[s][:, GLA_VAL_WIDTH + D_MODEL:]
        y = jax.nn.sigmoid(gate_fox) * branch_fox[s] + jax.nn.sigmoid(gate_gla) * branch_gla[s]
        merged.append(_dot(y.astype(BF16), wout_ref[...]))
    for s, rows in enumerate(subs):
        o_ref[rows, :] = h_ref[rows, :] + _rms(merged[s], gpost_ref[...])


def _outproj(h2d, o_fox, o_gla, g_pre, wgate, wbf, wbg, wout, g_post):
    t = h2d.shape[0]
    row = lambda w: pl.BlockSpec((OUT_TM, w), lambda i: (i, 0))
    return pl.pallas_call(
        _outproj_kernel,
        grid=(t // OUT_TM,),
        in_specs=[row(D_MODEL), row(FOX_WIDTH), row(GLA_VAL_WIDTH), _const_spec((1, D_MODEL)),
                  _const_spec((GLA_VAL_WIDTH + 2 * D_MODEL, D_MODEL)),
                  _const_spec((FOX_WIDTH, D_MODEL)), _const_spec((GLA_VAL_WIDTH, D_MODEL)),
                  _const_spec((D_MODEL, D_MODEL)), _const_spec((1, D_MODEL))],
        out_specs=row(D_MODEL),
        out_shape=jax.ShapeDtypeStruct((t, D_MODEL), F32),
        compiler_params=_params(1),
        name="outproj",
    )(h2d, o_fox, o_gla, g_pre, wgate, wbf, wbg, wout, g_post)


def _chunk_tril(n):
    r = jnp.arange(n)[:, None]
    c = jnp.arange(n)[None, :]
    return (((r // GLA_CHUNK) == (c // GLA_CHUNK)) & (r >= c)).astype(BF16)


def _layer(h2d, batch, seq, ffn1_pre_g, ffn1_w_gate, ffn1_w_up, ffn1_w_down, ffn1_post_g,
           mix_pre_g, w_in, b_forget, w_alpha_up, b_alpha, gla_norm_g, w_branch_fox,
           w_branch_gla, w_out, mix_post_g, ffn2_pre_g, ffn2_w_gate, ffn2_w_up, ffn2_w_down,
           ffn2_post_g):
    vec = lambda g: g.reshape(1, -1)
    bf = lambda w: w.astype(BF16)

    splits = [FOX_WIDTH, FOX_WIDTH, FOX_WIDTH, FOX_HEADS, GLA_KEY_WIDTH, GLA_KEY_WIDTH,
              GLA_VAL_WIDTH, GLA_GATE_RANK, GLA_VAL_WIDTH, D_MODEL, D_MODEL]
    offs = [0]
    for s in splits:
        offs.append(offs[-1] + s)
    w_in_t = w_in.T
    col = lambda a, b: w_in_t[offs[a]:offs[b], :]
    wfox = bf(col(0, 3))
    wgla = bf(col(4, 7))
    pad_cols = SMALL_W - FOX_HEADS - GLA_GATE_RANK
    wsm = bf(jnp.concatenate([col(3, 4), col(7, 8), jnp.zeros((pad_cols, D_MODEL), F32)], axis=0))
    bsm = jnp.concatenate([b_forget, jnp.zeros((SMALL_W - FOX_HEADS,), F32)]).reshape(1, SMALL_W)
    wa = bf(jnp.concatenate([jnp.zeros((FOX_HEADS, GLA_KEY_WIDTH), F32), w_alpha_up,
                             jnp.zeros((pad_cols, GLA_KEY_WIDTH), F32)], axis=0))
    wgate = bf(col(8, 11))
    tok = jnp.arange(LANES)
    upper = (tok[:, None] <= tok[None, :]).astype(BF16)
    ltri = _chunk_tril(TT)

    h1, ffn2_w = _ffn(h2d, vec(ffn1_pre_g), bf(ffn1_w_gate), bf(ffn1_w_up), bf(ffn1_w_down),
                      vec(ffn1_post_g), cast=(ffn2_w_gate, ffn2_w_up, ffn2_w_down))

    fq, fk, fv, c, gq, gk, gv, la = _inproj(h1, seq, vec(mix_pre_g), wfox, wgla, wsm, bsm, wa,
                                            vec(b_alpha), upper)
    shape3 = (batch, seq, FOX_WIDTH)
    o_fox, o_gla = _mixers(fq.reshape(shape3), fk.reshape(shape3), fv.reshape(shape3),
                           c.reshape(batch, FOX_HEADS // 2, 2, seq), gq, gk, gv, la, ltri,
                           vec(gla_norm_g))
    o_fox = o_fox.reshape(batch * seq, FOX_WIDTH)

    h2 = _outproj(h1, o_fox, o_gla, vec(mix_pre_g), wgate, bf(w_branch_fox),
                  bf(w_branch_gla), bf(w_out), vec(mix_post_g))

    return _ffn(h2, vec(ffn2_pre_g), *ffn2_w, vec(ffn2_post_g))[0]


def kernel(x, ffn1_pre_g, ffn1_w_gate, ffn1_w_up, ffn1_w_down, ffn1_post_g, mix_pre_g, w_in,
           b_forget, w_alpha_up, b_alpha, gla_norm_g, w_branch_fox, w_branch_gla, w_out,
           mix_post_g, ffn2_pre_g, ffn2_w_gate, ffn2_w_up, ffn2_w_down, ffn2_post_g):
    batch, seq, d = x.shape
    h = x.reshape(batch * seq, d)
    depth = ffn1_pre_g.shape[0]
    for l in range(depth):
        h = _layer(h, batch, seq, ffn1_pre_g[l], ffn1_w_gate[l], ffn1_w_up[l], ffn1_w_down[l],
                   ffn1_post_g[l], mix_pre_g[l], w_in[l], b_forget[l], w_alpha_up[l], b_alpha[l],
                   gla_norm_g[l], w_branch_fox[l], w_branch_gla[l], w_out[l], mix_post_g[l],
                   ffn2_pre_g[l], ffn2_w_gate[l], ffn2_w_up[l], ffn2_w_down[l], ffn2_post_g[l])
    return h.reshape(batch, seq, d)
```

```python
import functools

import jax
import jax.numpy as jnp
from jax import lax
from jax.experimental import pallas as pl
from jax.experimental.pallas import tpu as pltpu

F32 = jnp.float32
BF16 = jnp.bfloat16

D_MODEL = 1024
D_FF = 2816
FOX_HEADS = 8
FOX_HEAD_DIM = 64
FOX_WIDTH = FOX_HEADS * FOX_HEAD_DIM
GLA_HEADS = 4
GLA_DK = 64
GLA_DV = 128
GLA_KEY_WIDTH = GLA_HEADS * GLA_DK
GLA_VAL_WIDTH = GLA_HEADS * GLA_DV
GLA_GATE_RANK = 16
GLA_TAU = 16.0
GLA_CHUNK = 64
NORM_EPS = 1e-6

LANES = 128
HEAD_PAIR = LANES
VMEM_LIMIT_BYTES = 56 * 1024 * 1024

IN_TM = 1024
IN_SUB = 512
FFN_TM = 1024
FF_CHUNK = 256
OUT_TM = 1024
OUT_SUB = 512
FFN_SUB = 512
TQ = 256
FOX_WIDE = 512
FOX_ONES_ROWS = 16
FOX_PAIRS = 4
FOX_GROUP = 8
FOX_AHEAD = 3
LOG2_E = 1.4426950408889634
TT = 256
SMALL_W = LANES
CUM_ROWS = 16


def _rms(x, g):
    return x * lax.rsqrt(jnp.mean(x * x, axis=-1, keepdims=True) + NORM_EPS) * g


def _log_sigmoid(x):
    return jnp.minimum(x, 0.0) - jnp.log(1.0 + jnp.exp(-jnp.abs(x)))


def _dot(a, b):
    return jnp.dot(a, b, preferred_element_type=F32)


def _dot_nt(a, b):
    return lax.dot_general(a, b, (((1,), (1,)), ((), ())), preferred_element_type=F32)


def _dot_tn(a, b):
    return lax.dot_general(a, b, (((0,), (0,)), ((), ())), preferred_element_type=F32)


def _split2(x):
    hi = x.astype(BF16)
    lo = (x - hi.astype(F32)).astype(BF16)
    return hi, lo


def _split3(x):
    hi = x.astype(BF16)
    r = x - hi.astype(F32)
    mid = r.astype(BF16)
    lo = (r - mid.astype(F32)).astype(BF16)
    return hi, mid, lo


def _const_spec(shape):
    return pl.BlockSpec(shape, lambda *_: (0,) * len(shape), pipeline_mode=pl.Buffered(1))


def _params(n_axes):
    return pltpu.CompilerParams(dimension_semantics=("arbitrary",) * n_axes,
                                vmem_limit_bytes=VMEM_LIMIT_BYTES)


def _ffn_kernel(n_cast, x_ref, gpre_ref, wg_ref, wu_ref, wd_ref, gpost_ref, *rest):
    cast_in, o_ref, cast_out = rest[:n_cast], rest[n_cast], rest[n_cast + 1:]
    subs = [slice(s * FFN_SUB, (s + 1) * FFN_SUB) for s in range(FFN_TM // FFN_SUB)]
    xn = [_rms(x_ref[rows, :], gpre_ref[...]).astype(BF16) for rows in subs]
    acc = [None] * len(subs)
    for c in range(D_FF // FF_CHUNK):
        cols = slice(c * FF_CHUNK, (c + 1) * FF_CHUNK)
        for s in range(len(subs)):
            g = _dot(xn[s], wg_ref[:, cols])
            u = _dot(xn[s], wu_ref[:, cols])
            h = (g * jax.nn.sigmoid(g) * u).astype(BF16)
            part = _dot(h, wd_ref[cols, :])
            acc[s] = part if acc[s] is None else acc[s] + part
    for s, rows in enumerate(subs):
        o_ref[rows, :] = x_ref[rows, :] + 0.5 * _rms(acc[s], gpost_ref[...])
    for src, dst in zip(cast_in, cast_out):
        dst[...] = src[...].astype(BF16)


def _cast_spec(shape, steps):
    rows, cols = shape
    span = 1 if (rows // steps) % 16 == 0 else 2
    assert rows % steps == 0 and (span * rows // steps) % 16 == 0
    return pl.BlockSpec((span * rows // steps, cols), lambda i: (i // span, 0))


def _ffn(x2d, g_pre, w_gate, w_up, w_down, g_post, cast=()):
    t = x2d.shape[0]
    steps = t // FFN_TM
    row = pl.BlockSpec((FFN_TM, D_MODEL), lambda i: (i, 0))
    cast_specs = [_cast_spec(w.shape, steps) for w in cast]
    out = pl.pallas_call(
        functools.partial(_ffn_kernel, len(cast)),
        grid=(steps,),
        in_specs=[row, _const_spec((1, D_MODEL)), _const_spec((D_MODEL, D_FF)),
                  _const_spec((D_MODEL, D_FF)), _const_spec((D_FF, D_MODEL)),
                  _const_spec((1, D_MODEL))] + cast_specs,
        out_specs=[row] + cast_specs,
        out_shape=[jax.ShapeDtypeStruct((t, D_MODEL), F32)]
        + [jax.ShapeDtypeStruct(w.shape, BF16) for w in cast],
        compiler_params=_params(1),
        name="ffn",
    )(x2d, g_pre, w_gate, w_up, w_down, g_post, *cast)
    return out[0], tuple(out[1:])


def _inproj_kernel(tiles_per_seq, h_ref, g_ref, wfox_ref, wgla_ref, wsm_ref, bsm_ref, wa_ref,
                   ba_ref, upper_ref, fq_ref, fk_ref, fv_ref, c_ref, gq_ref, gk_ref, gv_ref,
                   la_ref, carry_ref):
    @pl.when(pl.program_id(0) % tiles_per_seq == 0)
    def _():
        carry_ref[...] = jnp.zeros_like(carry_ref)

    subs = [slice(s * IN_SUB, (s + 1) * IN_SUB) for s in range(IN_TM // IN_SUB)]
    scale = FOX_HEAD_DIM ** -0.5 * LOG2_E
    u = [_rms(h_ref[rows, :], g_ref[...]).astype(BF16) for rows in subs]

    zs = [_dot_nt(us, wsm_ref[...]) for us in u]

    for s, rows in enumerate(subs):
        zf = _dot_nt(u[s], wfox_ref[...])
        fq_ref[rows, :] = (zf[:, :FOX_WIDTH] * scale).astype(BF16)
        fk_ref[rows, :] = zf[:, FOX_WIDTH:2 * FOX_WIDTH].astype(BF16)
        fv_ref[rows, :] = zf[:, 2 * FOX_WIDTH:].astype(BF16)

    for s, rows in enumerate(subs):
        a = _dot(zs[s].astype(BF16), wa_ref[...]) + ba_ref[...]
        la_ref[rows, :] = _log_sigmoid(a) * (1.0 / GLA_TAU)

    for s, rows in enumerate(subs):
        zg = _dot_nt(u[s], wgla_ref[...])
        gq_ref[rows, :] = zg[:, :GLA_KEY_WIDTH] * (GLA_DK ** -0.5)
        gk_ref[rows, :] = zg[:, GLA_KEY_WIDTH:2 * GLA_KEY_WIDTH]
        gv_ref[rows, :] = zg[:, 2 * GLA_KEY_WIDTH:].astype(BF16)

    upper = upper_ref[...]
    carry = carry_ref[...]
    n_blocks = IN_SUB // LANES
    for s, rows in enumerate(subs):
        lf_t = _log_sigmoid(zs[s] + bsm_ref[...]).T[:CUM_ROWS]
        parts = jnp.concatenate(_split3(lf_t), axis=0)
        stacked = jnp.concatenate(
            [parts[:, k * LANES:(k + 1) * LANES] for k in range(n_blocks)], axis=0)
        sums = _dot(stacked, upper)
        c_blocks = []
        for k in range(n_blocks):
            blk = sums[k * 3 * CUM_ROWS:(k + 1) * 3 * CUM_ROWS]
            c_blocks.append(blk[:CUM_ROWS] + blk[CUM_ROWS:2 * CUM_ROWS] + blk[2 * CUM_ROWS:] + carry)
            carry = carry + jnp.sum(lf_t[:, k * LANES:(k + 1) * LANES], axis=1, keepdims=True)
        c_ref[0, :, rows] = jnp.concatenate(c_blocks, axis=1)[:FOX_HEADS] * LOG2_E
    carry_ref[...] = carry


def _inproj(h2d, seq, g, wfox, wgla, wsm, bsm, wa, ba, upper):
    t = h2d.shape[0]
    batch = t // seq
    tiles_per_seq = seq // IN_TM
    row = lambda w: pl.BlockSpec((IN_TM, w), lambda i: (i, 0))
    out_shapes = (
        jax.ShapeDtypeStruct((t, FOX_WIDTH), BF16),
        jax.ShapeDtypeStruct((t, FOX_WIDTH), BF16),
        jax.ShapeDtypeStruct((t, FOX_WIDTH), BF16),
        jax.ShapeDtypeStruct((batch, FOX_HEADS, seq), F32),
        jax.ShapeDtypeStruct((t, GLA_KEY_WIDTH), F32),
        jax.ShapeDtypeStruct((t, GLA_KEY_WIDTH), F32),
        jax.ShapeDtypeStruct((t, GLA_VAL_WIDTH), BF16),
        jax.ShapeDtypeStruct((t, GLA_KEY_WIDTH), F32),
    )
    out_specs = (
        row(FOX_WIDTH), row(FOX_WIDTH), row(FOX_WIDTH),
        pl.BlockSpec((1, FOX_HEADS, IN_TM), lambda i: (i // tiles_per_seq, 0, i % tiles_per_seq)),
        row(GLA_KEY_WIDTH), row(GLA_KEY_WIDTH), row(GLA_VAL_WIDTH), row(GLA_KEY_WIDTH),
    )
    return pl.pallas_call(
        functools.partial(_inproj_kernel, tiles_per_seq),
        grid=(t // IN_TM,),
        in_specs=[row(D_MODEL), _const_spec((1, D_MODEL)),
                  _const_spec((3 * FOX_WIDTH, D_MODEL)),
                  _const_spec((2 * GLA_KEY_WIDTH + GLA_VAL_WIDTH, D_MODEL)),
                  _const_spec((SMALL_W, D_MODEL)), _const_spec((1, SMALL_W)),
                  _const_spec((SMALL_W, GLA_KEY_WIDTH)), _const_spec((1, GLA_KEY_WIDTH)),
                  _const_spec((LANES, LANES))],
        out_specs=out_specs,
        out_shape=out_shapes,
        scratch_shapes=[pltpu.VMEM((CUM_ROWS, LANES), F32)],
        compiler_params=_params(1),
        name="inproj",
    )(h2d, g, wfox, wgla, wsm, bsm, wa, ba, upper)


def _fox_kernel(seq, q_ref, k_ref, v_ref, c_ref, o_ref, kaug_ref, qaug_ref, vt_ref):
    lane = lax.broadcasted_iota(jnp.int32, (1, HEAD_PAIR), 1)
    row = lax.broadcasted_iota(jnp.int32, (HEAD_PAIR, 1), 0)
    n_heads = 2 * FOX_PAIRS
    for pair in range(FOX_PAIRS):
        pair_lanes = slice(pair * HEAD_PAIR, (pair + 1) * HEAD_PAIR)
        k_all = k_ref[0, :, pair_lanes]
        q_t = q_ref[0, :, pair_lanes].astype(F32).T
        v_t = v_ref[0, :, pair_lanes].astype(F32).T
        for j in range(2):
            h = 2 * pair + j
            lo_lane, hi_lane = j * FOX_HEAD_DIM, (j + 1) * FOX_HEAD_DIM
            bias0 = (1 - j) * FOX_HEAD_DIM
            parts = [p.astype(F32) for p in _split3(-c_ref[0, pair, j:j + 1, :])]
            bias_t = jnp.zeros((HEAD_PAIR, seq), F32)
            for i, part in enumerate(parts):
                bias_t = jnp.where(row == bias0 + i, part, bias_t)
            own_lane = (lane >= lo_lane) & (lane < hi_lane)
            kaug_ref[h] = jnp.where(own_lane, k_all, bias_t.T.astype(BF16))
            own_row = (row >= lo_lane) & (row < hi_lane)
            one_row = (row >= bias0) & (row < bias0 + len(parts))
            qaug_ref[h] = jnp.where(own_row, q_t, jnp.where(one_row, 1.0, 0.0)).astype(BF16)
            vt_ref[h] = jnp.concatenate(
                [v_t[lo_lane:hi_lane], jnp.ones((FOX_ONES_ROWS, seq), F32)], axis=0).astype(BF16)

    chains = []
    for qi in reversed(range(seq // TQ)):
        q0 = qi * TQ
        for h in range(n_heads):
            steps, k0 = [], 0
            while k0 + FOX_WIDE <= q0:
                steps.append((k0, FOX_WIDE, False))
                k0 += FOX_WIDE
            if k0 < q0:
                steps.append((k0, q0 - k0, False))
            steps.append((q0, TQ, True))
            chains.append((q0, h, steps))
    items = []
    for g in range(0, len(chains), FOX_GROUP):
        group = chains[g:g + FOX_GROUP]
        for t in range(max(len(c[2]) for c in group)):
            for q0, h, steps in group:
                if t < len(steps):
                    items.append((q0, h) + steps[t])

    def scores(item):
        q0, h, k0, width, _ = item
        return _dot(kaug_ref[h, k0:k0 + width, :], qaug_ref[h, :, q0:q0 + TQ])

    state, out_t = {}, {}
    pending = [scores(item) for item in items[:FOX_AHEAD]]
    for t, item in enumerate(items):
        q0, h, k0, width, diagonal = item
        s_t = pending.pop(0)
        if t + FOX_AHEAD < len(items):
            pending.append(scores(items[t + FOX_AHEAD]))
        m, acc = state.get((q0, h), (jnp.full((1, TQ), -jnp.inf, F32),
                                     jnp.zeros((FOX_HEAD_DIM + FOX_ONES_ROWS, TQ), F32)))
        if diagonal:
            key = lax.broadcasted_iota(jnp.int32, (width, TQ), 0) + k0
            qry = lax.broadcasted_iota(jnp.int32, (width, TQ), 1) + q0
            s_t = jnp.where(key <= qry, s_t, -jnp.inf)
        m_next = jnp.maximum(m, jnp.max(s_t, axis=0, keepdims=True))
        p_t = jnp.exp2(s_t - m_next)
        acc = acc * jnp.exp2(m - m_next) + _dot(vt_ref[h, :, k0:k0 + width], p_t.astype(BF16))
        state[(q0, h)] = (m_next, acc)
        if diagonal:
            denom = acc[FOX_HEAD_DIM:FOX_HEAD_DIM + 8]
            out_t[(q0, h)] = acc[:FOX_HEAD_DIM] / jnp.concatenate(
                [denom] * (FOX_HEAD_DIM // 8), axis=0)
            if (q0, h ^ 1) in out_t:
                pair = h // 2
                both = jnp.concatenate([out_t[(q0, 2 * pair)], out_t[(q0, 2 * pair + 1)]], axis=0)
                o_ref[0, q0:q0 + TQ, pair * HEAD_PAIR:(pair + 1) * HEAD_PAIR] = both.T.astype(BF16)


def _gla_kernel(seq, q_ref, k_ref, v_ref, la_ref, ltri_ref, o_ref):
    n_chunks = TT // GLA_CHUNK
    lane = lax.broadcasted_iota(jnp.int32, (1, HEAD_PAIR), 1)
    head_mask = (lane < GLA_DK, lane >= GLA_DK)
    row_chunk = lax.broadcasted_iota(jnp.int32, (TT, 1), 0) // GLA_CHUNK
    rr = lax.broadcasted_iota(jnp.int32, (TT, TT), 0)
    cc = lax.broadcasted_iota(jnp.int32, (TT, TT), 1)
    intra = (rr >= cc) & (rr // GLA_CHUNK == cc // GLA_CHUNK)
    ltri = ltri_ref[...]

    n_pairs = GLA_HEADS // 2
    n_tiles = seq // TT

    def rows_of(t):
        return slice(t * TT, (t + 1) * TT)

    def cumulate(t):
        hi, lo = _split2(la_ref[rows_of(t), :])
        return _dot(ltri, hi) + _dot(ltri, lo)

    def products(t, b_all):
        rows = rows_of(t)
        out = []
        for p in range(n_pairs):
            ksl = slice(p * HEAD_PAIR, (p + 1) * HEAD_PAIR)
            b = b_all[:, ksl]
            last_rows = [b[(i + 1) * GLA_CHUNK - 1:(i + 1) * GLA_CHUNK, :] for i in range(n_chunks)]
            b_last = jnp.concatenate(
                [jnp.broadcast_to(r, (GLA_CHUNK, HEAD_PAIR)) for r in last_rows], axis=0)
            q_in = (q_ref[rows, ksl] * jnp.exp(b)).astype(BF16)
            k = k_ref[rows, ksl]
            k_in = (k * jnp.exp(-b)).astype(BF16)
            k_st = (k * jnp.exp(b_last - b)).astype(BF16)
            kv = None
            attn = []
            for j in range(2):
                h = 2 * p + j
                vj = v_ref[rows, h * GLA_DV:(h + 1) * GLA_DV]
                kcat = jnp.concatenate(
                    [jnp.where((row_chunk == i) & head_mask[j], k_st, jnp.zeros_like(k_st))
                     for i in range(n_chunks)], axis=1)
                contrib = _dot_tn(vj, kcat)
                kv = contrib if kv is None else kv + contrib
                qh = jnp.where(head_mask[j], q_in, jnp.zeros_like(q_in))
                attn.append(_dot_nt(qh, k_in))
            out.append((q_in, [jnp.exp(r) for r in last_rows], kv, attn))
        return out

    state = [jnp.zeros((GLA_DV, HEAD_PAIR), F32) for _ in range(n_pairs)]

    def finish(t, prods):
        rows = rows_of(t)
        for p in range(n_pairs):
            q_in, decays, kv, attn = prods[p]
            o_inter = []
            for i in range(n_chunks):
                st = state[p].astype(BF16)
                st2 = jnp.concatenate(
                    [jnp.where(head_mask[j], st, jnp.zeros_like(st)) for j in range(2)], axis=0)
                o_inter.append(_dot_nt(q_in[i * GLA_CHUNK:(i + 1) * GLA_CHUNK, :], st2))
                state[p] = state[p] * decays[i] + kv[:, i * HEAD_PAIR:(i + 1) * HEAD_PAIR]
            o_intra = []
            for j in range(2):
                h = 2 * p + j
                vj = v_ref[rows, h * GLA_DV:(h + 1) * GLA_DV]
                o_intra.append(_dot(jnp.where(intra, attn[j], 0.0).astype(BF16), vj))
            o_ref[rows, 2 * p * GLA_DV:(2 * p + 2) * GLA_DV] = (
                jnp.concatenate(o_intra, axis=1) + jnp.concatenate(o_inter, axis=0))

    b_alls, prods = {}, {}
    for step in range(n_tiles + 2):
        if step < n_tiles:
            b_alls[step] = cumulate(step)
        if 0 <= step - 1 < n_tiles:
            prods[step - 1] = products(step - 1, b_alls.pop(step - 1))
        if 0 <= step - 2 < n_tiles:
            finish(step - 2, prods.pop(step - 2))


def _mixers_kernel(seq, fq_ref, fk_ref, fv_ref, c_ref, gq_ref, gk_ref, gv_ref, la_ref, ltri_ref,
                   ofox_ref, ogla_ref, kaug_ref, qaug_ref, vt_ref):
    _fox_kernel(seq, fq_ref, fk_ref, fv_ref, c_ref, ofox_ref, kaug_ref, qaug_ref, vt_ref)
    _gla_kernel(seq, gq_ref, gk_ref, gv_ref, la_ref, ltri_ref, ogla_ref)


def _mixers(fq, fk, fv, c, gq, gk, gv, la, ltri):
    batch, seq, _ = fq.shape
    assert FOX_PAIRS * HEAD_PAIR == FOX_WIDTH
    n_heads = 2 * FOX_PAIRS
    qkv = pl.BlockSpec((1, seq, FOX_WIDTH), lambda b: (b, 0, 0))
    row = lambda w: pl.BlockSpec((seq, w), lambda b: (b, 0))
    return pl.pallas_call(
        functools.partial(_mixers_kernel, seq),
        grid=(batch,),
        in_specs=[qkv, qkv, qkv, pl.BlockSpec((1, FOX_PAIRS, 2, seq), lambda b: (b, 0, 0, 0)),
                  row(GLA_KEY_WIDTH), row(GLA_KEY_WIDTH), row(GLA_VAL_WIDTH), row(GLA_KEY_WIDTH),
                  _const_spec((TT, TT))],
        out_specs=[qkv, row(GLA_VAL_WIDTH)],
        out_shape=[jax.ShapeDtypeStruct((batch, seq, FOX_WIDTH), BF16),
                   jax.ShapeDtypeStruct((batch * seq, GLA_VAL_WIDTH), F32)],
        scratch_shapes=[pltpu.VMEM((n_heads, seq, HEAD_PAIR), BF16),
                        pltpu.VMEM((n_heads, HEAD_PAIR, seq), BF16),
                        pltpu.VMEM((n_heads, FOX_HEAD_DIM + FOX_ONES_ROWS, seq), BF16)],
        compiler_params=_params(1),
        name="mixers",
    )(fq, fk, fv, c, gq, gk, gv, la, ltri)


def _outproj_kernel(h_ref, ofox_ref, ogla_ref, gpre_ref, wgate_ref, gnorm_ref, wbf_ref, wbg_ref,
                    wout_ref, gpost_ref, o_ref):
    subs = [slice(s * OUT_SUB, (s + 1) * OUT_SUB) for s in range(OUT_TM // OUT_SUB)]
    u = [_rms(h_ref[rows, :], gpre_ref[...]).astype(BF16) for rows in subs]
    gates = [_dot_nt(us, wgate_ref[...]) for us in u]
    branch_fox = [_dot(ofox_ref[rows, :], wbf_ref[...]) for rows in subs]
    branch_gla = []
    for s, rows in enumerate(subs):
        g_r = gates[s][:, :GLA_VAL_WIDTH]
        heads = []
        for hd in range(GLA_HEADS):
            sl = slice(hd * GLA_DV, (hd + 1) * GLA_DV)
            heads.append(_rms(ogla_ref[rows, sl], gnorm_ref[:, sl]))
        o_gla = jnp.concatenate(heads, axis=1)
        o_gla = (o_gla * (g_r * jax.nn.sigmoid(g_r))).astype(BF16)
        branch_gla.append(_dot(o_gla, wbg_ref[...]))
    merged = []
    for s in range(len(subs)):
        gate_fox = gates[s][:, GLA_VAL_WIDTH:GLA_VAL_WIDTH + D_MODEL]
        gate_gla = gates[s][:, GLA_VAL_WIDTH + D_MODEL:]
        y = jax.nn.sigmoid(gate_fox) * branch_fox[s] + jax.nn.sigmoid(gate_gla) * branch_gla[s]
        merged.append(_dot(y.astype(BF16), wout_ref[...]))
    for s, rows in enumerate(subs):
        o_ref[rows, :] = h_ref[rows, :] + _rms(merged[s], gpost_ref[...])


def _outproj(h2d, o_fox, o_gla, g_pre, wgate, gnorm, wbf, wbg, wout, g_post):
    t = h2d.shape[0]
    row = lambda w: pl.BlockSpec((OUT_TM, w), lambda i: (i, 0))
    return pl.pallas_call(
        _outproj_kernel,
        grid=(t // OUT_TM,),
        in_specs=[row(D_MODEL), row(FOX_WIDTH), row(GLA_VAL_WIDTH), _const_spec((1, D_MODEL)),
                  _const_spec((GLA_VAL_WIDTH + 2 * D_MODEL, D_MODEL)),
                  _const_spec((1, GLA_VAL_WIDTH)),
                  _const_spec((FOX_WIDTH, D_MODEL)), _const_spec((GLA_VAL_WIDTH, D_MODEL)),
                  _const_spec((D_MODEL, D_MODEL)), _const_spec((1, D_MODEL))],
        out_specs=row(D_MODEL),
        out_shape=jax.ShapeDtypeStruct((t, D_MODEL), F32),
        compiler_params=_params(1),
        name="outproj",
    )(h2d, o_fox, o_gla, g_pre, wgate, gnorm, wbf, wbg, wout, g_post)


def _chunk_tril(n):
    r = jnp.arange(n)[:, None]
    c = jnp.arange(n)[None, :]
    return (((r // GLA_CHUNK) == (c // GLA_CHUNK)) & (r >= c)).astype(BF16)


def _layer(h2d, batch, seq, ffn1_pre_g, ffn1_w_gate, ffn1_w_up, ffn1_w_down, ffn1_post_g,
           mix_pre_g, w_in, b_forget, w_alpha_up, b_alpha, gla_norm_g, w_branch_fox,
           w_branch_gla, w_out, mix_post_g, ffn2_pre_g, ffn2_w_gate, ffn2_w_up, ffn2_w_down,
           ffn2_post_g):
    vec = lambda g: g.reshape(1, -1)
    bf = lambda w: w.astype(BF16)

    splits = [FOX_WIDTH, FOX_WIDTH, FOX_WIDTH, FOX_HEADS, GLA_KEY_WIDTH, GLA_KEY_WIDTH,
              GLA_VAL_WIDTH, GLA_GATE_RANK, GLA_VAL_WIDTH, D_MODEL, D_MODEL]
    offs = [0]
    for s in splits:
        offs.append(offs[-1] + s)
    w_in_t = w_in.T
    col = lambda a, b: w_in_t[offs[a]:offs[b], :]
    wfox = bf(col(0, 3))
    wgla = bf(col(4, 7))
    pad_cols = SMALL_W - FOX_HEADS - GLA_GATE_RANK
    wsm = bf(jnp.concatenate([col(3, 4), col(7, 8), jnp.zeros((pad_cols, D_MODEL), F32)], axis=0))
    bsm = jnp.concatenate([b_forget, jnp.zeros((SMALL_W - FOX_HEADS,), F32)]).reshape(1, SMALL_W)
    wa = bf(jnp.concatenate([jnp.zeros((FOX_HEADS, GLA_KEY_WIDTH), F32), w_alpha_up,
                             jnp.zeros((pad_cols, GLA_KEY_WIDTH), F32)], axis=0))
    wgate = bf(col(8, 11))
    tok = jnp.arange(LANES)
    upper = (tok[:, None] <= tok[None, :]).astype(BF16)
    ltri = _chunk_tril(TT)

    h1, ffn2_w = _ffn(h2d, vec(ffn1_pre_g), bf(ffn1_w_gate), bf(ffn1_w_up), bf(ffn1_w_down),
                      vec(ffn1_post_g), cast=(ffn2_w_gate, ffn2_w_up, ffn2_w_down))

    fq, fk, fv, c, gq, gk, gv, la = _inproj(h1, seq, vec(mix_pre_g), wfox, wgla, wsm, bsm, wa,
                                            vec(b_alpha), upper)
    shape3 = (batch, seq, FOX_WIDTH)
    o_fox, o_gla = _mixers(fq.reshape(shape3), fk.reshape(shape3), fv.reshape(shape3),
                           c.reshape(batch, FOX_HEADS // 2, 2, seq), gq, gk, gv, la, ltri)
    o_fox = o_fox.reshape(batch * seq, FOX_WIDTH)

    h2 = _outproj(h1, o_fox, o_gla, vec(mix_pre_g), wgate, vec(gla_norm_g), bf(w_branch_fox),
                  bf(w_branch_gla), bf(w_out), vec(mix_post_g))

    return _ffn(h2, vec(ffn2_pre_g), *ffn2_w, vec(ffn2_post_g))[0]


def kernel(x, ffn1_pre_g, ffn1_w_gate, ffn1_w_up, ffn1_w_down, ffn1_post_g, mix_pre_g, w_in,
           b_forget, w_alpha_up, b_alpha, gla_norm_g, w_branch_fox, w_branch_gla, w_out,
           mix_post_g, ffn2_pre_g, ffn2_w_gate, ffn2_w_up, ffn2_w_down, ffn2_post_g):
    batch, seq, d = x.shape
    h = x.reshape(batch * seq, d)
    depth = ffn1_pre_g.shape[0]
    for l in range(depth):
        h = _layer(h, batch, seq, ffn1_pre_g[l], ffn1_w_gate[l], ffn1_w_up[l], ffn1_w_down[l],
                   ffn1_post_g[l], mix_pre_g[l], w_in[l], b_forget[l], w_alpha_up[l], b_alpha[l],
                   gla_norm_g[l], w_branch_fox[l], w_branch_gla[l], w_out[l], mix_post_g[l],
                   ffn2_pre_g[l], ffn2_w_gate[l], ffn2_w_up[l], ffn2_w_down[l], ffn2_post_g[l])
    return h.reshape(batch, seq, d)
```

```python
import functools

import jax
import jax.numpy as jnp
from jax import lax
from jax.experimental import pallas as pl
from jax.experimental.pallas import tpu as pltpu

F32 = jnp.float32
BF16 = jnp.bfloat16

D_MODEL = 1024
D_FF = 2816
FOX_HEADS = 8
FOX_HEAD_DIM = 64
FOX_WIDTH = FOX_HEADS * FOX_HEAD_DIM
GLA_HEADS = 4
GLA_DK = 64
GLA_DV = 128
GLA_KEY_WIDTH = GLA_HEADS * GLA_DK
GLA_VAL_WIDTH = GLA_HEADS * GLA_DV
GLA_GATE_RANK = 16
GLA_TAU = 16.0
GLA_CHUNK = 64
NORM_EPS = 1e-6

LANES = 128
HEAD_PAIR = LANES
VMEM_LIMIT_BYTES = 56 * 1024 * 1024

IN_TM = 1024
IN_SUB = 512
FFN_TM = 1024
FF_CHUNK = 256
OUT_TM = 1024
OUT_SUB = 512
FFN_SUB = 512
TQ = 256
FOX_WIDE = 512
FOX_ONES_ROWS = 16
FOX_PAIRS = 4
FOX_GROUP = 8
FOX_AHEAD = 5
LOG2_E = 1.4426950408889634
TT = 256
SMALL_W = LANES
CUM_ROWS = 16


def _rms(x, g):
    return x * lax.rsqrt(jnp.mean(x * x, axis=-1, keepdims=True) + NORM_EPS) * g


def _log_sigmoid(x):
    return jnp.minimum(x, 0.0) - jnp.log(1.0 + jnp.exp(-jnp.abs(x)))


def _dot(a, b):
    return jnp.dot(a, b, preferred_element_type=F32)


def _dot_nt(a, b):
    return lax.dot_general(a, b, (((1,), (1,)), ((), ())), preferred_element_type=F32)


def _dot_tn(a, b):
    return lax.dot_general(a, b, (((0,), (0,)), ((), ())), preferred_element_type=F32)


def _split2(x):
    hi = x.astype(BF16)
    lo = (x - hi.astype(F32)).astype(BF16)
    return hi, lo


def _split3(x):
    hi = x.astype(BF16)
    r = x - hi.astype(F32)
    mid = r.astype(BF16)
    lo = (r - mid.astype(F32)).astype(BF16)
    return hi, mid, lo


def _const_spec(shape):
    return pl.BlockSpec(shape, lambda *_: (0,) * len(shape), pipeline_mode=pl.Buffered(1))


def _params(n_axes):
    return pltpu.CompilerParams(dimension_semantics=("arbitrary",) * n_axes,
                                vmem_limit_bytes=VMEM_LIMIT_BYTES)


def _ffn_kernel(n_cast, x_ref, gpre_ref, wg_ref, wu_ref, wd_ref, gpost_ref, *rest):
    cast_in, o_ref, cast_out = rest[:n_cast], rest[n_cast], rest[n_cast + 1:]
    subs = [slice(s * FFN_SUB, (s + 1) * FFN_SUB) for s in range(FFN_TM // FFN_SUB)]
    xn = [_rms(x_ref[rows, :], gpre_ref[...]).astype(BF16) for rows in subs]
    acc = [None] * len(subs)
    for c in range(D_FF // FF_CHUNK):
        cols = slice(c * FF_CHUNK, (c + 1) * FF_CHUNK)
        for s in range(len(subs)):
            g = _dot(xn[s], wg_ref[:, cols])
            u = _dot(xn[s], wu_ref[:, cols])
            h = (g * jax.nn.sigmoid(g) * u).astype(BF16)
            part = _dot(h, wd_ref[cols, :])
            acc[s] = part if acc[s] is None else acc[s] + part
    for s, rows in enumerate(subs):
        o_ref[rows, :] = x_ref[rows, :] + 0.5 * _rms(acc[s], gpost_ref[...])
    for src, dst in zip(cast_in, cast_out):
        dst[...] = src[...].astype(BF16)


def _cast_spec(shape, steps):
    rows, cols = shape
    span = 1 if (rows // steps) % 16 == 0 else 2
    assert rows % steps == 0 and (span * rows // steps) % 16 == 0
    return pl.BlockSpec((span * rows // steps, cols), lambda i: (i // span, 0))


def _ffn(x2d, g_pre, w_gate, w_up, w_down, g_post, cast=()):
    t = x2d.shape[0]
    steps = t // FFN_TM
    row = pl.BlockSpec((FFN_TM, D_MODEL), lambda i: (i, 0))
    cast_specs = [_cast_spec(w.shape, steps) for w in cast]
    out = pl.pallas_call(
        functools.partial(_ffn_kernel, len(cast)),
        grid=(steps,),
        in_specs=[row, _const_spec((1, D_MODEL)), _const_spec((D_MODEL, D_FF)),
                  _const_spec((D_MODEL, D_FF)), _const_spec((D_FF, D_MODEL)),
                  _const_spec((1, D_MODEL))] + cast_specs,
        out_specs=[row] + cast_specs,
        out_shape=[jax.ShapeDtypeStruct((t, D_MODEL), F32)]
        + [jax.ShapeDtypeStruct(w.shape, BF16) for w in cast],
        compiler_params=_params(1),
        name="ffn",
    )(x2d, g_pre, w_gate, w_up, w_down, g_post, *cast)
    return out[0], tuple(out[1:])


def _inproj_kernel(tiles_per_seq, h_ref, g_ref, wfox_ref, wgla_ref, wsm_ref, bsm_ref, wa_ref,
                   ba_ref, upper_ref, fq_ref, fk_ref, fv_ref, c_ref, gq_ref, gk_ref, gv_ref,
                   la_ref, carry_ref):
    @pl.when(pl.program_id(0) % tiles_per_seq == 0)
    def _():
        carry_ref[...] = jnp.zeros_like(carry_ref)

    subs = [slice(s * IN_SUB, (s + 1) * IN_SUB) for s in range(IN_TM // IN_SUB)]
    scale = FOX_HEAD_DIM ** -0.5 * LOG2_E
    u = [_rms(h_ref[rows, :], g_ref[...]).astype(BF16) for rows in subs]

    zs = [_dot_nt(us, wsm_ref[...]) for us in u]

    for s, rows in enumerate(subs):
        zf = _dot_nt(u[s], wfox_ref[...])
        fq_ref[rows, :] = (zf[:, :FOX_WIDTH] * scale).astype(BF16)
        fk_ref[rows, :] = zf[:, FOX_WIDTH:2 * FOX_WIDTH].astype(BF16)
        fv_ref[rows, :] = zf[:, 2 * FOX_WIDTH:].astype(BF16)

    for s, rows in enumerate(subs):
        a = _dot(zs[s].astype(BF16), wa_ref[...]) + ba_ref[...]
        la_ref[rows, :] = _log_sigmoid(a) * (1.0 / GLA_TAU)

    for s, rows in enumerate(subs):
        zg = _dot_nt(u[s], wgla_ref[...])
        gq_ref[rows, :] = zg[:, :GLA_KEY_WIDTH] * (GLA_DK ** -0.5)
        gk_ref[rows, :] = zg[:, GLA_KEY_WIDTH:2 * GLA_KEY_WIDTH]
        gv_ref[rows, :] = zg[:, 2 * GLA_KEY_WIDTH:].astype(BF16)

    upper = upper_ref[...]
    carry = carry_ref[...]
    n_blocks = IN_SUB // LANES
    for s, rows in enumerate(subs):
        lf_t = _log_sigmoid(zs[s] + bsm_ref[...]).T[:CUM_ROWS]
        parts = jnp.concatenate(_split3(lf_t), axis=0)
        stacked = jnp.concatenate(
            [parts[:, k * LANES:(k + 1) * LANES] for k in range(n_blocks)], axis=0)
        sums = _dot(stacked, upper)
        c_blocks = []
        for k in range(n_blocks):
            blk = sums[k * 3 * CUM_ROWS:(k + 1) * 3 * CUM_ROWS]
            c_blocks.append(blk[:CUM_ROWS] + blk[CUM_ROWS:2 * CUM_ROWS] + blk[2 * CUM_ROWS:] + carry)
            carry = carry + jnp.sum(lf_t[:, k * LANES:(k + 1) * LANES], axis=1, keepdims=True)
        c_ref[0, :, rows] = jnp.concatenate(c_blocks, axis=1)[:FOX_HEADS] * LOG2_E
    carry_ref[...] = carry


def _inproj(h2d, seq, g, wfox, wgla, wsm, bsm, wa, ba, upper):
    t = h2d.shape[0]
    batch = t // seq
    tiles_per_seq = seq // IN_TM
    row = lambda w: pl.BlockSpec((IN_TM, w), lambda i: (i, 0))
    out_shapes = (
        jax.ShapeDtypeStruct((t, FOX_WIDTH), BF16),
        jax.ShapeDtypeStruct((t, FOX_WIDTH), BF16),
        jax.ShapeDtypeStruct((t, FOX_WIDTH), BF16),
        jax.ShapeDtypeStruct((batch, FOX_HEADS, seq), F32),
        jax.ShapeDtypeStruct((t, GLA_KEY_WIDTH), F32),
        jax.ShapeDtypeStruct((t, GLA_KEY_WIDTH), F32),
        jax.ShapeDtypeStruct((t, GLA_VAL_WIDTH), BF16),
        jax.ShapeDtypeStruct((t, GLA_KEY_WIDTH), F32),
    )
    out_specs = (
        row(FOX_WIDTH), row(FOX_WIDTH), row(FOX_WIDTH),
        pl.BlockSpec((1, FOX_HEADS, IN_TM), lambda i: (i // tiles_per_seq, 0, i % tiles_per_seq)),
        row(GLA_KEY_WIDTH), row(GLA_KEY_WIDTH), row(GLA_VAL_WIDTH), row(GLA_KEY_WIDTH),
    )
    return pl.pallas_call(
        functools.partial(_inproj_kernel, tiles_per_seq),
        grid=(t // IN_TM,),
        in_specs=[row(D_MODEL), _const_spec((1, D_MODEL)),
                  _const_spec((3 * FOX_WIDTH, D_MODEL)),
                  _const_spec((2 * GLA_KEY_WIDTH + GLA_VAL_WIDTH, D_MODEL)),
                  _const_spec((SMALL_W, D_MODEL)), _const_spec((1, SMALL_W)),
                  _const_spec((SMALL_W, GLA_KEY_WIDTH)), _const_spec((1, GLA_KEY_WIDTH)),
                  _const_spec((LANES, LANES))],
        out_specs=out_specs,
        out_shape=out_shapes,
        scratch_shapes=[pltpu.VMEM((CUM_ROWS, LANES), F32)],
        compiler_params=_params(1),
        name="inproj",
    )(h2d, g, wfox, wgla, wsm, bsm, wa, ba, upper)


def _fox_kernel(seq, q_ref, k_ref, v_ref, c_ref, o_ref, kaug_ref, qaug_ref, vt_ref):
    lane = lax.broadcasted_iota(jnp.int32, (1, HEAD_PAIR), 1)
    row = lax.broadcasted_iota(jnp.int32, (HEAD_PAIR, 1), 0)
    n_heads = 2 * FOX_PAIRS
    for pair in range(FOX_PAIRS):
        pair_lanes = slice(pair * HEAD_PAIR, (pair + 1) * HEAD_PAIR)
        k_all = k_ref[0, :, pair_lanes]
        q_t = q_ref[0, :, pair_lanes].astype(F32).T
        v_t = v_ref[0, :, pair_lanes].astype(F32).T
        for j in range(2):
            h = 2 * pair + j
            lo_lane, hi_lane = j * FOX_HEAD_DIM, (j + 1) * FOX_HEAD_DIM
            bias0 = (1 - j) * FOX_HEAD_DIM
            parts = [p.astype(F32) for p in _split3(-c_ref[0, pair, j:j + 1, :])]
            bias_t = jnp.zeros((HEAD_PAIR, seq), F32)
            for i, part in enumerate(parts):
                bias_t = jnp.where(row == bias0 + i, part, bias_t)
            own_lane = (lane >= lo_lane) & (lane < hi_lane)
            kaug_ref[h] = jnp.where(own_lane, k_all, bias_t.T.astype(BF16))
            own_row = (row >= lo_lane) & (row < hi_lane)
            one_row = (row >= bias0) & (row < bias0 + len(parts))
            qaug_ref[h] = jnp.where(own_row, q_t, jnp.where(one_row, 1.0, 0.0)).astype(BF16)
            vt_ref[h] = jnp.concatenate(
                [v_t[lo_lane:hi_lane], jnp.ones((FOX_ONES_ROWS, seq), F32)], axis=0).astype(BF16)

    chains = []
    for qi in reversed(range(seq // TQ)):
        q0 = qi * TQ
        for h in range(n_heads):
            steps, k0 = [], 0
            while k0 + FOX_WIDE <= q0:
                steps.append((k0, FOX_WIDE, False))
                k0 += FOX_WIDE
            if k0 < q0:
                steps.append((k0, q0 - k0, False))
            steps.append((q0, TQ, True))
            chains.append((q0, h, steps))
    items = []
    for g in range(0, len(chains), FOX_GROUP):
        group = chains[g:g + FOX_GROUP]
        for t in range(max(len(c[2]) for c in group)):
            for q0, h, steps in group:
                if t < len(steps):
                    items.append((q0, h) + steps[t])

    def scores(item):
        q0, h, k0, width, _ = item
        return _dot(kaug_ref[h, k0:k0 + width, :], qaug_ref[h, :, q0:q0 + TQ])

    state, out_t = {}, {}
    pending = [scores(item) for item in items[:FOX_AHEAD]]
    for t, item in enumerate(items):
        q0, h, k0, width, diagonal = item
        s_t = pending.pop(0)
        if t + FOX_AHEAD < len(items):
            pending.append(scores(items[t + FOX_AHEAD]))
        m, acc = state.get((q0, h), (jnp.full((1, TQ), -jnp.inf, F32),
                                     jnp.zeros((FOX_HEAD_DIM + FOX_ONES_ROWS, TQ), F32)))
        if diagonal:
            key = lax.broadcasted_iota(jnp.int32, (width, TQ), 0) + k0
            qry = lax.broadcasted_iota(jnp.int32, (width, TQ), 1) + q0
            s_t = jnp.where(key <= qry, s_t, -jnp.inf)
        m_next = jnp.maximum(m, jnp.max(s_t, axis=0, keepdims=True))
        p_t = jnp.exp2(s_t - m_next)
        acc = acc * jnp.exp2(m - m_next) + _dot(vt_ref[h, :, k0:k0 + width], p_t.astype(BF16))
        state[(q0, h)] = (m_next, acc)
        if diagonal:
            denom = acc[FOX_HEAD_DIM:FOX_HEAD_DIM + 8]
            out_t[(q0, h)] = acc[:FOX_HEAD_DIM] / jnp.concatenate(
                [denom] * (FOX_HEAD_DIM // 8), axis=0)
            if (q0, h ^ 1) in out_t:
                pair = h // 2
                both = jnp.concatenate([out_t[(q0, 2 * pair)], out_t[(q0, 2 * pair + 1)]], axis=0)
                o_ref[0, q0:q0 + TQ, pair * HEAD_PAIR:(pair + 1) * HEAD_PAIR] = both.T.astype(BF16)


def _gla_kernel(seq, q_ref, k_ref, v_ref, la_ref, ltri_ref, o_ref):
    n_chunks = TT // GLA_CHUNK
    lane = lax.broadcasted_iota(jnp.int32, (1, HEAD_PAIR), 1)
    head_mask = (lane < GLA_DK, lane >= GLA_DK)
    row_chunk = lax.broadcasted_iota(jnp.int32, (TT, 1), 0) // GLA_CHUNK
    rr = lax.broadcasted_iota(jnp.int32, (TT, TT), 0)
    cc = lax.broadcasted_iota(jnp.int32, (TT, TT), 1)
    intra = (rr >= cc) & (rr // GLA_CHUNK == cc // GLA_CHUNK)
    ltri = ltri_ref[...]

    n_pairs = GLA_HEADS // 2
    n_tiles = seq // TT

    def rows_of(t):
        return slice(t * TT, (t + 1) * TT)

    def cumulate(t):
        hi, lo = _split2(la_ref[rows_of(t), :])
        return _dot(ltri, hi) + _dot(ltri, lo)

    def products(t, b_all):
        rows = rows_of(t)
        out = []
        for p in range(n_pairs):
            ksl = slice(p * HEAD_PAIR, (p + 1) * HEAD_PAIR)
            b = b_all[:, ksl]
            last_rows = [b[(i + 1) * GLA_CHUNK - 1:(i + 1) * GLA_CHUNK, :] for i in range(n_chunks)]
            b_last = jnp.concatenate(
                [jnp.broadcast_to(r, (GLA_CHUNK, HEAD_PAIR)) for r in last_rows], axis=0)
            q_in = (q_ref[rows, ksl] * jnp.exp(b)).astype(BF16)
            k = k_ref[rows, ksl]
            k_in = (k * jnp.exp(-b)).astype(BF16)
            k_st = (k * jnp.exp(b_last - b)).astype(BF16)
            kv = None
            attn = []
            for j in range(2):
                h = 2 * p + j
                vj = v_ref[rows, h * GLA_DV:(h + 1) * GLA_DV]
                kcat = jnp.concatenate(
                    [jnp.where((row_chunk == i) & head_mask[j], k_st, jnp.zeros_like(k_st))
                     for i in range(n_chunks)], axis=1)
                contrib = _dot_tn(vj, kcat)
                kv = contrib if kv is None else kv + contrib
                qh = jnp.where(head_mask[j], q_in, jnp.zeros_like(q_in))
                attn.append(_dot_nt(qh, k_in))
            out.append((q_in, [jnp.exp(r) for r in last_rows], kv, attn))
        return out

    state = [jnp.zeros((GLA_DV, HEAD_PAIR), F32) for _ in range(n_pairs)]

    def finish(t, prods):
        rows = rows_of(t)
        for p in range(n_pairs):
            q_in, decays, kv, attn = prods[p]
            o_inter = []
            for i in range(n_chunks):
                st = state[p].astype(BF16)
                st2 = jnp.concatenate(
                    [jnp.where(head_mask[j], st, jnp.zeros_like(st)) for j in range(2)], axis=0)
                o_inter.append(_dot_nt(q_in[i * GLA_CHUNK:(i + 1) * GLA_CHUNK, :], st2))
                state[p] = state[p] * decays[i] + kv[:, i * HEAD_PAIR:(i + 1) * HEAD_PAIR]
            o_intra = []
            for j in range(2):
                h = 2 * p + j
                vj = v_ref[rows, h * GLA_DV:(h + 1) * GLA_DV]
                o_intra.append(_dot(jnp.where(intra, attn[j], 0.0).astype(BF16), vj))
            o_ref[rows, 2 * p * GLA_DV:(2 * p + 2) * GLA_DV] = (
                jnp.concatenate(o_intra, axis=1) + jnp.concatenate(o_inter, axis=0))

    b_alls, prods = {}, {}
    for step in range(n_tiles + 2):
        if step < n_tiles:
            b_alls[step] = cumulate(step)
        if 0 <= step - 1 < n_tiles:
            prods[step - 1] = products(step - 1, b_alls.pop(step - 1))
        if 0 <= step - 2 < n_tiles:
            finish(step - 2, prods.pop(step - 2))


def _mixers_kernel(seq, fq_ref, fk_ref, fv_ref, c_ref, gq_ref, gk_ref, gv_ref, la_ref, ltri_ref,
                   ofox_ref, ogla_ref, kaug_ref, qaug_ref, vt_ref):
    _fox_kernel(seq, fq_ref, fk_ref, fv_ref, c_ref, ofox_ref, kaug_ref, qaug_ref, vt_ref)
    _gla_kernel(seq, gq_ref, gk_ref, gv_ref, la_ref, ltri_ref, ogla_ref)


def _mixers(fq, fk, fv, c, gq, gk, gv, la, ltri):
    batch, seq, _ = fq.shape
    assert FOX_PAIRS * HEAD_PAIR == FOX_WIDTH
    n_heads = 2 * FOX_PAIRS
    qkv = pl.BlockSpec((1, seq, FOX_WIDTH), lambda b: (b, 0, 0))
    row = lambda w: pl.BlockSpec((seq, w), lambda b: (b, 0))
    return pl.pallas_call(
        functools.partial(_mixers_kernel, seq),
        grid=(batch,),
        in_specs=[qkv, qkv, qkv, pl.BlockSpec((1, FOX_PAIRS, 2, seq), lambda b: (b, 0, 0, 0)),
                  row(GLA_KEY_WIDTH), row(GLA_KEY_WIDTH), row(GLA_VAL_WIDTH), row(GLA_KEY_WIDTH),
                  _const_spec((TT, TT))],
        out_specs=[qkv, row(GLA_VAL_WIDTH)],
        out_shape=[jax.ShapeDtypeStruct((batch, seq, FOX_WIDTH), BF16),
                   jax.ShapeDtypeStruct((batch * seq, GLA_VAL_WIDTH), F32)],
        scratch_shapes=[pltpu.VMEM((n_heads, seq, HEAD_PAIR), BF16),
                        pltpu.VMEM((n_heads, HEAD_PAIR, seq), BF16),
                        pltpu.VMEM((n_heads, FOX_HEAD_DIM + FOX_ONES_ROWS, seq), BF16)],
        compiler_params=_params(1),
        name="mixers",
    )(fq, fk, fv, c, gq, gk, gv, la, ltri)


def _outproj_kernel(h_ref, ofox_ref, ogla_ref, gpre_ref, wgate_ref, gnorm_ref, wbf_ref, wbg_ref,
                    wout_ref, gpost_ref, o_ref):
    subs = [slice(s * OUT_SUB, (s + 1) * OUT_SUB) for s in range(OUT_TM // OUT_SUB)]
    u = [_rms(h_ref[rows, :], gpre_ref[...]).astype(BF16) for rows in subs]
    gates = [_dot_nt(us, wgate_ref[...]) for us in u]
    branch_fox = [_dot(ofox_ref[rows, :], wbf_ref[...]) for rows in subs]
    branch_gla = []
    for s, rows in enumerate(subs):
        g_r = gates[s][:, :GLA_VAL_WIDTH]
        heads = []
        for hd in range(GLA_HEADS):
            sl = slice(hd * GLA_DV, (hd + 1) * GLA_DV)
            heads.append(_rms(ogla_ref[rows, sl], gnorm_ref[:, sl]))
        o_gla = jnp.concatenate(heads, axis=1)
        o_gla = (o_gla * (g_r * jax.nn.sigmoid(g_r))).astype(BF16)
        branch_gla.append(_dot(o_gla, wbg_ref[...]))
    merged = []
    for s in range(len(subs)):
        gate_fox = gates[s][:, GLA_VAL_WIDTH:GLA_VAL_WIDTH + D_MODEL]
        gate_gla = gates[s][:, GLA_VAL_WIDTH + D_MODEL:]
        y = jax.nn.sigmoid(gate_fox) * branch_fox[s] + jax.nn.sigmoid(gate_gla) * branch_gla[s]
        merged.append(_dot(y.astype(BF16), wout_ref[...]))
    for s, rows in enumerate(subs):
        o_ref[rows, :] = h_ref[rows, :] + _rms(merged[s], gpost_ref[...])


def _outproj(h2d, o_fox, o_gla, g_pre, wgate, gnorm, wbf, wbg, wout, g_post):
    t = h2d.shape[0]
    row = lambda w: pl.BlockSpec((OUT_TM, w), lambda i: (i, 0))
    return pl.pallas_call(
        _outproj_kernel,
        grid=(t // OUT_TM,),
        in_specs=[row(D_MODEL), row(FOX_WIDTH), row(GLA_VAL_WIDTH), _const_spec((1, D_MODEL)),
                  _const_spec((GLA_VAL_WIDTH + 2 * D_MODEL, D_MODEL)),
                  _const_spec((1, GLA_VAL_WIDTH)),
                  _const_spec((FOX_WIDTH, D_MODEL)), _const_spec((GLA_VAL_WIDTH, D_MODEL)),
                  _const_spec((D_MODEL, D_MODEL)), _const_spec((1, D_MODEL))],
        out_specs=row(D_MODEL),
        out_shape=jax.ShapeDtypeStruct((t, D_MODEL), F32),
        compiler_params=_params(1),
        name="outproj",
    )(h2d, o_fox, o_gla, g_pre, wgate, gnorm, wbf, wbg, wout, g_post)


def _chunk_tril(n):
    r = jnp.arange(n)[:, None]
    c = jnp.arange(n)[None, :]
    return (((r // GLA_CHUNK) == (c // GLA_CHUNK)) & (r >= c)).astype(BF16)


def _layer(h2d, batch, seq, ffn1_pre_g, ffn1_w_gate, ffn1_w_up, ffn1_w_down, ffn1_post_g,
           mix_pre_g, w_in, b_forget, w_alpha_up, b_alpha, gla_norm_g, w_branch_fox,
           w_branch_gla, w_out, mix_post_g, ffn2_pre_g, ffn2_w_gate, ffn2_w_up, ffn2_w_down,
           ffn2_post_g):
    vec = lambda g: g.reshape(1, -1)
    bf = lambda w: w.astype(BF16)

    splits = [FOX_WIDTH, FOX_WIDTH, FOX_WIDTH, FOX_HEADS, GLA_KEY_WIDTH, GLA_KEY_WIDTH,
              GLA_VAL_WIDTH, GLA_GATE_RANK, GLA_VAL_WIDTH, D_MODEL, D_MODEL]
    offs = [0]
    for s in splits:
        offs.append(offs[-1] + s)
    w_in_t = w_in.T
    col = lambda a, b: w_in_t[offs[a]:offs[b], :]
    wfox = bf(col(0, 3))
    wgla = bf(col(4, 7))
    pad_cols = SMALL_W - FOX_HEADS - GLA_GATE_RANK
    wsm = bf(jnp.concatenate([col(3, 4), col(7, 8), jnp.zeros((pad_cols, D_MODEL), F32)], axis=0))
    bsm = jnp.concatenate([b_forget, jnp.zeros((SMALL_W - FOX_HEADS,), F32)]).reshape(1, SMALL_W)
    wa = bf(jnp.concatenate([jnp.zeros((FOX_HEADS, GLA_KEY_WIDTH), F32), w_alpha_up,
                             jnp.zeros((pad_cols, GLA_KEY_WIDTH), F32)], axis=0))
    wgate = bf(col(8, 11))
    tok = jnp.arange(LANES)
    upper = (tok[:, None] <= tok[None, :]).astype(BF16)
    ltri = _chunk_tril(TT)

    h1, ffn2_w = _ffn(h2d, vec(ffn1_pre_g), bf(ffn1_w_gate), bf(ffn1_w_up), bf(ffn1_w_down),
                      vec(ffn1_post_g), cast=(ffn2_w_gate, ffn2_w_up, ffn2_w_down))

    fq, fk, fv, c, gq, gk, gv, la = _inproj(h1, seq, vec(mix_pre_g), wfox, wgla, wsm, bsm, wa,
                                            vec(b_alpha), upper)
    shape3 = (batch, seq, FOX_WIDTH)
    o_fox, o_gla = _mixers(fq.reshape(shape3), fk.reshape(shape3), fv.reshape(shape3),
                           c.reshape(batch, FOX_HEADS // 2, 2, seq), gq, gk, gv, la, ltri)
    o_fox = o_fox.reshape(batch * seq, FOX_WIDTH)

    h2 = _outproj(h1, o_fox, o_gla, vec(mix_pre_g), wgate, vec(gla_norm_g), bf(w_branch_fox),
                  bf(w_branch_gla), bf(w_out), vec(mix_post_g))

    return _ffn(h2, vec(ffn2_pre_g), *ffn2_w, vec(ffn2_post_g))[0]


def kernel(x, ffn1_pre_g, ffn1_w_gate, ffn1_w_up, ffn1_w_down, ffn1_post_g, mix_pre_g, w_in,
           b_forget, w_alpha_up, b_alpha, gla_norm_g, w_branch_fox, w_branch_gla, w_out,
           mix_post_g, ffn2_pre_g, ffn2_w_gate, ffn2_w_up, ffn2_w_down, ffn2_post_g):
    batch, seq, d = x.shape
    h = x.reshape(batch * seq, d)
    depth = ffn1_pre_g.shape[0]
    for l in range(depth):
        h = _layer(h, batch, seq, ffn1_pre_g[l], ffn1_w_gate[l], ffn1_w_up[l], ffn1_w_down[l],
                   ffn1_post_g[l], mix_pre_g[l], w_in[l], b_forget[l], w_alpha_up[l], b_alpha[l],
                   gla_norm_g[l], w_branch_fox[l], w_branch_gla[l], w_out[l], mix_post_g[l],
                   ffn2_pre_g[l], ffn2_w_gate[l], ffn2_w_up[l], ffn2_w_down[l], ffn2_post_g[l])
    return h.reshape(batch, seq, d)
```

```python
import functools

import jax
import jax.numpy as jnp
from jax import lax
from jax.experimental import pallas as pl
from jax.experimental.pallas import tpu as pltpu

F32 = jnp.float32
BF16 = jnp.bfloat16

D_MODEL = 1024
D_FF = 2816
FOX_HEADS = 8
FOX_HEAD_DIM = 64
FOX_WIDTH = FOX_HEADS * FOX_HEAD_DIM
GLA_HEADS = 4
GLA_DK = 64
GLA_DV = 128
GLA_KEY_WIDTH = GLA_HEADS * GLA_DK
GLA_VAL_WIDTH = GLA_HEADS * GLA_DV
GLA_GATE_RANK = 16
GLA_TAU = 16.0
GLA_CHUNK = 64
NORM_EPS = 1e-6

LANES = 128
HEAD_PAIR = LANES
VMEM_LIMIT_BYTES = 56 * 1024 * 1024

IN_TM = 1024
IN_SUB = 512
FFN_TM = 1024
FF_CHUNK = 256
OUT_TM = 1024
OUT_SUB = 512
FFN_SUB = 512
TQ = 256
FOX_WIDE = 512
FOX_ONES_ROWS = 16
FOX_PAIRS = 4
FOX_GROUP = 16
FOX_AHEAD = 4
LOG2_E = 1.4426950408889634
TT = 256
SMALL_W = LANES
CUM_ROWS = 16


def _rms(x, g):
    return x * lax.rsqrt(jnp.mean(x * x, axis=-1, keepdims=True) + NORM_EPS) * g


def _log_sigmoid(x):
    return jnp.minimum(x, 0.0) - jnp.log(1.0 + jnp.exp(-jnp.abs(x)))


def _dot(a, b):
    return jnp.dot(a, b, preferred_element_type=F32)


def _dot_nt(a, b):
    return lax.dot_general(a, b, (((1,), (1,)), ((), ())), preferred_element_type=F32)


def _dot_tn(a, b):
    return lax.dot_general(a, b, (((0,), (0,)), ((), ())), preferred_element_type=F32)


def _split2(x):
    hi = x.astype(BF16)
    lo = (x - hi.astype(F32)).astype(BF16)
    return hi, lo


def _split3(x):
    hi = x.astype(BF16)
    r = x - hi.astype(F32)
    mid = r.astype(BF16)
    lo = (r - mid.astype(F32)).astype(BF16)
    return hi, mid, lo


def _const_spec(shape):
    return pl.BlockSpec(shape, lambda *_: (0,) * len(shape), pipeline_mode=pl.Buffered(1))


def _params(n_axes):
    return pltpu.CompilerParams(dimension_semantics=("arbitrary",) * n_axes,
                                vmem_limit_bytes=VMEM_LIMIT_BYTES)


def _ffn_kernel(n_cast, x_ref, gpre_ref, wg_ref, wu_ref, wd_ref, gpost_ref, *rest):
    cast_in, o_ref, cast_out = rest[:n_cast], rest[n_cast], rest[n_cast + 1:]
    subs = [slice(s * FFN_SUB, (s + 1) * FFN_SUB) for s in range(FFN_TM // FFN_SUB)]
    xn = [_rms(x_ref[rows, :], gpre_ref[...]).astype(BF16) for rows in subs]
    acc = [None] * len(subs)
    for c in range(D_FF // FF_CHUNK):
        cols = slice(c * FF_CHUNK, (c + 1) * FF_CHUNK)
        for s in range(len(subs)):
            g = _dot(xn[s], wg_ref[:, cols])
            u = _dot(xn[s], wu_ref[:, cols])
            h = (g * jax.nn.sigmoid(g) * u).astype(BF16)
            part = _dot(h, wd_ref[cols, :])
            acc[s] = part if acc[s] is None else acc[s] + part
    for s, rows in enumerate(subs):
        o_ref[rows, :] = x_ref[rows, :] + 0.5 * _rms(acc[s], gpost_ref[...])
    for src, dst in zip(cast_in, cast_out):
        dst[...] = src[...].astype(BF16)


def _cast_spec(shape, steps):
    rows, cols = shape
    span = 1 if (rows // steps) % 16 == 0 else 2
    assert rows % steps == 0 and (span * rows // steps) % 16 == 0
    return pl.BlockSpec((span * rows // steps, cols), lambda i: (i // span, 0))


def _ffn(x2d, g_pre, w_gate, w_up, w_down, g_post, cast=()):
    t = x2d.shape[0]
    steps = t // FFN_TM
    row = pl.BlockSpec((FFN_TM, D_MODEL), lambda i: (i, 0))
    cast_specs = [_cast_spec(w.shape, steps) for w in cast]
    out = pl.pallas_call(
        functools.partial(_ffn_kernel, len(cast)),
        grid=(steps,),
        in_specs=[row, _const_spec((1, D_MODEL)), _const_spec((D_MODEL, D_FF)),
                  _const_spec((D_MODEL, D_FF)), _const_spec((D_FF, D_MODEL)),
                  _const_spec((1, D_MODEL))] + cast_specs,
        out_specs=[row] + cast_specs,
        out_shape=[jax.ShapeDtypeStruct((t, D_MODEL), F32)]
        + [jax.ShapeDtypeStruct(w.shape, BF16) for w in cast],
        compiler_params=_params(1),
        name="ffn",
    )(x2d, g_pre, w_gate, w_up, w_down, g_post, *cast)
    return out[0], tuple(out[1:])


def _inproj_kernel(tiles_per_seq, h_ref, g_ref, wfox_ref, wgla_ref, wsm_ref, bsm_ref, wa_ref,
                   ba_ref, upper_ref, fq_ref, fk_ref, fv_ref, c_ref, gq_ref, gk_ref, gv_ref,
                   la_ref, carry_ref):
    @pl.when(pl.program_id(0) % tiles_per_seq == 0)
    def _():
        carry_ref[...] = jnp.zeros_like(carry_ref)

    subs = [slice(s * IN_SUB, (s + 1) * IN_SUB) for s in range(IN_TM // IN_SUB)]
    scale = FOX_HEAD_DIM ** -0.5 * LOG2_E
    u = [_rms(h_ref[rows, :], g_ref[...]).astype(BF16) for rows in subs]

    zs = [_dot_nt(us, wsm_ref[...]) for us in u]

    for s, rows in enumerate(subs):
        zf = _dot_nt(u[s], wfox_ref[...])
        fq_ref[rows, :] = (zf[:, :FOX_WIDTH] * scale).astype(BF16)
        fk_ref[rows, :] = zf[:, FOX_WIDTH:2 * FOX_WIDTH].astype(BF16)
        fv_ref[rows, :] = zf[:, 2 * FOX_WIDTH:].astype(BF16)

    for s, rows in enumerate(subs):
        a = _dot(zs[s].astype(BF16), wa_ref[...]) + ba_ref[...]
        la_ref[rows, :] = _log_sigmoid(a) * (1.0 / GLA_TAU)

    for s, rows in enumerate(subs):
        zg = _dot_nt(u[s], wgla_ref[...])
        gq_ref[rows, :] = zg[:, :GLA_KEY_WIDTH] * (GLA_DK ** -0.5)
        gk_ref[rows, :] = zg[:, GLA_KEY_WIDTH:2 * GLA_KEY_WIDTH]
        gv_ref[rows, :] = zg[:, 2 * GLA_KEY_WIDTH:].astype(BF16)

    upper = upper_ref[...]
    carry = carry_ref[...]
    n_blocks = IN_SUB // LANES
    for s, rows in enumerate(subs):
        lf_t = _log_sigmoid(zs[s] + bsm_ref[...]).T[:CUM_ROWS]
        parts = jnp.concatenate(_split3(lf_t), axis=0)
        stacked = jnp.concatenate(
            [parts[:, k * LANES:(k + 1) * LANES] for k in range(n_blocks)], axis=0)
        sums = _dot(stacked, upper)
        c_blocks = []
        for k in range(n_blocks):
            blk = sums[k * 3 * CUM_ROWS:(k + 1) * 3 * CUM_ROWS]
            c_blocks.append(blk[:CUM_ROWS] + blk[CUM_ROWS:2 * CUM_ROWS] + blk[2 * CUM_ROWS:] + carry)
            carry = carry + jnp.sum(lf_t[:, k * LANES:(k + 1) * LANES], axis=1, keepdims=True)
        c_ref[0, :, rows] = jnp.concatenate(c_blocks, axis=1)[:FOX_HEADS] * LOG2_E
    carry_ref[...] = carry


def _inproj(h2d, seq, g, wfox, wgla, wsm, bsm, wa, ba, upper):
    t = h2d.shape[0]
    batch = t // seq
    tiles_per_seq = seq // IN_TM
    row = lambda w: pl.BlockSpec((IN_TM, w), lambda i: (i, 0))
    out_shapes = (
        jax.ShapeDtypeStruct((t, FOX_WIDTH), BF16),
        jax.ShapeDtypeStruct((t, FOX_WIDTH), BF16),
        jax.ShapeDtypeStruct((t, FOX_WIDTH), BF16),
        jax.ShapeDtypeStruct((batch, FOX_HEADS, seq), F32),
        jax.ShapeDtypeStruct((t, GLA_KEY_WIDTH), F32),
        jax.ShapeDtypeStruct((t, GLA_KEY_WIDTH), F32),
        jax.ShapeDtypeStruct((t, GLA_VAL_WIDTH), BF16),
        jax.ShapeDtypeStruct((t, GLA_KEY_WIDTH), F32),
    )
    out_specs = (
        row(FOX_WIDTH), row(FOX_WIDTH), row(FOX_WIDTH),
        pl.BlockSpec((1, FOX_HEADS, IN_TM), lambda i: (i // tiles_per_seq, 0, i % tiles_per_seq)),
        row(GLA_KEY_WIDTH), row(GLA_KEY_WIDTH), row(GLA_VAL_WIDTH), row(GLA_KEY_WIDTH),
    )
    return pl.pallas_call(
        functools.partial(_inproj_kernel, tiles_per_seq),
        grid=(t // IN_TM,),
        in_specs=[row(D_MODEL), _const_spec((1, D_MODEL)),
                  _const_spec((3 * FOX_WIDTH, D_MODEL)),
                  _const_spec((2 * GLA_KEY_WIDTH + GLA_VAL_WIDTH, D_MODEL)),
                  _const_spec((SMALL_W, D_MODEL)), _const_spec((1, SMALL_W)),
                  _const_spec((SMALL_W, GLA_KEY_WIDTH)), _const_spec((1, GLA_KEY_WIDTH)),
                  _const_spec((LANES, LANES))],
        out_specs=out_specs,
        out_shape=out_shapes,
        scratch_shapes=[pltpu.VMEM((CUM_ROWS, LANES), F32)],
        compiler_params=_params(1),
        name="inproj",
    )(h2d, g, wfox, wgla, wsm, bsm, wa, ba, upper)


def _fox_kernel(seq, q_ref, k_ref, v_ref, c_ref, o_ref, kaug_ref, qaug_ref, vt_ref):
    lane = lax.broadcasted_iota(jnp.int32, (1, HEAD_PAIR), 1)
    row = lax.broadcasted_iota(jnp.int32, (HEAD_PAIR, 1), 0)
    n_heads = 2 * FOX_PAIRS
    for pair in range(FOX_PAIRS):
        pair_lanes = slice(pair * HEAD_PAIR, (pair + 1) * HEAD_PAIR)
        k_all = k_ref[0, :, pair_lanes]
        q_t = q_ref[0, :, pair_lanes].astype(F32).T
        v_t = v_ref[0, :, pair_lanes].astype(F32).T
        for j in range(2):
            h = 2 * pair + j
            lo_lane, hi_lane = j * FOX_HEAD_DIM, (j + 1) * FOX_HEAD_DIM
            bias0 = (1 - j) * FOX_HEAD_DIM
            parts = [p.astype(F32) for p in _split3(-c_ref[0, pair, j:j + 1, :])]
            bias_t = jnp.zeros((HEAD_PAIR, seq), F32)
            for i, part in enumerate(parts):
                bias_t = jnp.where(row == bias0 + i, part, bias_t)
            own_lane = (lane >= lo_lane) & (lane < hi_lane)
            kaug_ref[h] = jnp.where(own_lane, k_all, bias_t.T.astype(BF16))
            own_row = (row >= lo_lane) & (row < hi_lane)
            one_row = (row >= bias0) & (row < bias0 + len(parts))
            qaug_ref[h] = jnp.where(own_row, q_t, jnp.where(one_row, 1.0, 0.0)).astype(BF16)
            vt_ref[h] = jnp.concatenate(
                [v_t[lo_lane:hi_lane], jnp.ones((FOX_ONES_ROWS, seq), F32)], axis=0).astype(BF16)

    chains = []
    for qi in reversed(range(seq // TQ)):
        q0 = qi * TQ
        for h in range(n_heads):
            steps, k0 = [], 0
            while k0 + FOX_WIDE <= q0:
                steps.append((k0, FOX_WIDE, False))
                k0 += FOX_WIDE
            if k0 < q0:
                steps.append((k0, q0 - k0, False))
            steps.append((q0, TQ, True))
            chains.append((q0, h, steps))
    items = []
    for g in range(0, len(chains), FOX_GROUP):
        group = chains[g:g + FOX_GROUP]
        for t in range(max(len(c[2]) for c in group)):
            for q0, h, steps in group:
                if t < len(steps):
                    items.append((q0, h) + steps[t])

    def scores(item):
        q0, h, k0, width, _ = item
        return _dot(kaug_ref[h, k0:k0 + width, :], qaug_ref[h, :, q0:q0 + TQ])

    state, out_t = {}, {}
    pending = [scores(item) for item in items[:FOX_AHEAD]]
    for t, item in enumerate(items):
        q0, h, k0, width, diagonal = item
        s_t = pending.pop(0)
        if t + FOX_AHEAD < len(items):
            pending.append(scores(items[t + FOX_AHEAD]))
        m, acc = state.get((q0, h), (jnp.full((1, TQ), -jnp.inf, F32),
                                     jnp.zeros((FOX_HEAD_DIM + FOX_ONES_ROWS, TQ), F32)))
        if diagonal:
            key = lax.broadcasted_iota(jnp.int32, (width, TQ), 0) + k0
            qry = lax.broadcasted_iota(jnp.int32, (width, TQ), 1) + q0
            s_t = jnp.where(key <= qry, s_t, -jnp.inf)
        m_next = jnp.maximum(m, jnp.max(s_t, axis=0, keepdims=True))
        p_t = jnp.exp2(s_t - m_next)
        acc = acc * jnp.exp2(m - m_next) + _dot(vt_ref[h, :, k0:k0 + width], p_t.astype(BF16))
        state[(q0, h)] = (m_next, acc)
        if diagonal:
            denom = acc[FOX_HEAD_DIM:FOX_HEAD_DIM + 8]
            out_t[(q0, h)] = acc[:FOX_HEAD_DIM] / jnp.concatenate(
                [denom] * (FOX_HEAD_DIM // 8), axis=0)
            if (q0, h ^ 1) in out_t:
                pair = h // 2
                both = jnp.concatenate([out_t[(q0, 2 * pair)], out_t[(q0, 2 * pair + 1)]], axis=0)
                o_ref[0, q0:q0 + TQ, pair * HEAD_PAIR:(pair + 1) * HEAD_PAIR] = both.T.astype(BF16)


def _gla_kernel(seq, q_ref, k_ref, v_ref, la_ref, ltri_ref, o_ref):
    n_chunks = TT // GLA_CHUNK
    lane = lax.broadcasted_iota(jnp.int32, (1, HEAD_PAIR), 1)
    head_mask = (lane < GLA_DK, lane >= GLA_DK)
    row_chunk = lax.broadcasted_iota(jnp.int32, (TT, 1), 0) // GLA_CHUNK
    rr = lax.broadcasted_iota(jnp.int32, (TT, TT), 0)
    cc = lax.broadcasted_iota(jnp.int32, (TT, TT), 1)
    intra = (rr >= cc) & (rr // GLA_CHUNK == cc // GLA_CHUNK)
    ltri = ltri_ref[...]

    n_pairs = GLA_HEADS // 2
    n_tiles = seq // TT

    def rows_of(t):
        return slice(t * TT, (t + 1) * TT)

    def cumulate(t):
        hi, lo = _split2(la_ref[rows_of(t), :])
        return _dot(ltri, hi) + _dot(ltri, lo)

    def products(t, b_all):
        rows = rows_of(t)
        out = []
        for p in range(n_pairs):
            ksl = slice(p * HEAD_PAIR, (p + 1) * HEAD_PAIR)
            b = b_all[:, ksl]
            last_rows = [b[(i + 1) * GLA_CHUNK - 1:(i + 1) * GLA_CHUNK, :] for i in range(n_chunks)]
            b_last = jnp.concatenate(
                [jnp.broadcast_to(r, (GLA_CHUNK, HEAD_PAIR)) for r in last_rows], axis=0)
            q_in = (q_ref[rows, ksl] * jnp.exp(b)).astype(BF16)
            k = k_ref[rows, ksl]
            k_in = (k * jnp.exp(-b)).astype(BF16)
            k_st = (k * jnp.exp(b_last - b)).astype(BF16)
            kv = None
            attn = []
            for j in range(2):
                h = 2 * p + j
                vj = v_ref[rows, h * GLA_DV:(h + 1) * GLA_DV]
                kcat = jnp.concatenate(
                    [jnp.where((row_chunk == i) & head_mask[j], k_st, jnp.zeros_like(k_st))
                     for i in range(n_chunks)], axis=1)
                contrib = _dot_tn(vj, kcat)
                kv = contrib if kv is None else kv + contrib
                qh = jnp.where(head_mask[j], q_in, jnp.zeros_like(q_in))
                attn.append(_dot_nt(qh, k_in))
            out.append((q_in, [jnp.exp(r) for r in last_rows], kv, attn))
        return out

    state = [jnp.zeros((GLA_DV, HEAD_PAIR), F32) for _ in range(n_pairs)]

    def finish(t, prods):
        rows = rows_of(t)
        for p in range(n_pairs):
            q_in, decays, kv, attn = prods[p]
            o_inter = []
            for i in range(n_chunks):
                st = state[p].astype(BF16)
                st2 = jnp.concatenate(
                    [jnp.where(head_mask[j], st, jnp.zeros_like(st)) for j in range(2)], axis=0)
                o_inter.append(_dot_nt(q_in[i * GLA_CHUNK:(i + 1) * GLA_CHUNK, :], st2))
                state[p] = state[p] * decays[i] + kv[:, i * HEAD_PAIR:(i + 1) * HEAD_PAIR]
            o_intra = []
            for j in range(2):
                h = 2 * p + j
                vj = v_ref[rows, h * GLA_DV:(h + 1) * GLA_DV]
                o_intra.append(_dot(jnp.where(intra, attn[j], 0.0).astype(BF16), vj))
            o_ref[rows, 2 * p * GLA_DV:(2 * p + 2) * GLA_DV] = (
                jnp.concatenate(o_intra, axis=1) + jnp.concatenate(o_inter, axis=0))

    b_alls, prods = {}, {}
    for step in range(n_tiles + 2):
        if step < n_tiles:
            b_alls[step] = cumulate(step)
        if 0 <= step - 1 < n_tiles:
            prods[step - 1] = products(step - 1, b_alls.pop(step - 1))
        if 0 <= step - 2 < n_tiles:
            finish(step - 2, prods.pop(step - 2))


def _mixers_kernel(seq, fq_ref, fk_ref, fv_ref, c_ref, gq_ref, gk_ref, gv_ref, la_ref, ltri_ref,
                   ofox_ref, ogla_ref, kaug_ref, qaug_ref, vt_ref):
    _fox_kernel(seq, fq_ref, fk_ref, fv_ref, c_ref, ofox_ref, kaug_ref, qaug_ref, vt_ref)
    _gla_kernel(seq, gq_ref, gk_ref, gv_ref, la_ref, ltri_ref, ogla_ref)


def _mixers(fq, fk, fv, c, gq, gk, gv, la, ltri):
    batch, seq, _ = fq.shape
    assert FOX_PAIRS * HEAD_PAIR == FOX_WIDTH
    n_heads = 2 * FOX_PAIRS
    qkv = pl.BlockSpec((1, seq, FOX_WIDTH), lambda b: (b, 0, 0))
    row = lambda w: pl.BlockSpec((seq, w), lambda b: (b, 0))
    return pl.pallas_call(
        functools.partial(_mixers_kernel, seq),
        grid=(batch,),
        in_specs=[qkv, qkv, qkv, pl.BlockSpec((1, FOX_PAIRS, 2, seq), lambda b: (b, 0, 0, 0)),
                  row(GLA_KEY_WIDTH), row(GLA_KEY_WIDTH), row(GLA_VAL_WIDTH), row(GLA_KEY_WIDTH),
                  _const_spec((TT, TT))],
        out_specs=[qkv, row(GLA_VAL_WIDTH)],
        out_shape=[jax.ShapeDtypeStruct((batch, seq, FOX_WIDTH), BF16),
                   jax.ShapeDtypeStruct((batch * seq, GLA_VAL_WIDTH), F32)],
        scratch_shapes=[pltpu.VMEM((n_heads, seq, HEAD_PAIR), BF16),
                        pltpu.VMEM((n_heads, HEAD_PAIR, seq), BF16),
                        pltpu.VMEM((n_heads, FOX_HEAD_DIM + FOX_ONES_ROWS, seq), BF16)],
        compiler_params=_params(1),
        name="mixers",
    )(fq, fk, fv, c, gq, gk, gv, la, ltri)


def _outproj_kernel(h_ref, ofox_ref, ogla_ref, gpre_ref, wgate_ref, gnorm_ref, wbf_ref, wbg_ref,
                    wout_ref, gpost_ref, o_ref):
    subs = [slice(s * OUT_SUB, (s + 1) * OUT_SUB) for s in range(OUT_TM // OUT_SUB)]
    u = [_rms(h_ref[rows, :], gpre_ref[...]).astype(BF16) for rows in subs]
    gates = [_dot_nt(us, wgate_ref[...]) for us in u]
    branch_fox = [_dot(ofox_ref[rows, :], wbf_ref[...]) for rows in subs]
    branch_gla = []
    for s, rows in enumerate(subs):
        g_r = gates[s][:, :GLA_VAL_WIDTH]
        heads = []
        for hd in range(GLA_HEADS):
            sl = slice(hd * GLA_DV, (hd + 1) * GLA_DV)
            heads.append(_rms(ogla_ref[rows, sl], gnorm_ref[:, sl]))
        o_gla = jnp.concatenate(heads, axis=1)
        o_gla = (o_gla * (g_r * jax.nn.sigmoid(g_r))).astype(BF16)
        branch_gla.append(_dot(o_gla, wbg_ref[...]))
    merged = []
    for s in range(len(subs)):
        gate_fox = gates[s][:, GLA_VAL_WIDTH:GLA_VAL_WIDTH + D_MODEL]
        gate_gla = gates[s][:, GLA_VAL_WIDTH + D_MODEL:]
        y = jax.nn.sigmoid(gate_fox) * branch_fox[s] + jax.nn.sigmoid(gate_gla) * branch_gla[s]
        merged.append(_dot(y.astype(BF16), wout_ref[...]))
    for s, rows in enumerate(subs):
        o_ref[rows, :] = h_ref[rows, :] + _rms(merged[s], gpost_ref[...])


def _outproj(h2d, o_fox, o_gla, g_pre, wgate, gnorm, wbf, wbg, wout, g_post):
    t = h2d.shape[0]
    row = lambda w: pl.BlockSpec((OUT_TM, w), lambda i: (i, 0))
    return pl.pallas_call(
        _outproj_kernel,
        grid=(t // OUT_TM,),
        in_specs=[row(D_MODEL), row(FOX_WIDTH), row(GLA_VAL_WIDTH), _const_spec((1, D_MODEL)),
                  _const_spec((GLA_VAL_WIDTH + 2 * D_MODEL, D_MODEL)),
                  _const_spec((1, GLA_VAL_WIDTH)),
                  _const_spec((FOX_WIDTH, D_MODEL)), _const_spec((GLA_VAL_WIDTH, D_MODEL)),
                  _const_spec((D_MODEL, D_MODEL)), _const_spec((1, D_MODEL))],
        out_specs=row(D_MODEL),
        out_shape=jax.ShapeDtypeStruct((t, D_MODEL), F32),
        compiler_params=_params(1),
        name="outproj",
    )(h2d, o_fox, o_gla, g_pre, wgate, gnorm, wbf, wbg, wout, g_post)


def _chunk_tril(n):
    r = jnp.arange(n)[:, None]
    c = jnp.arange(n)[None, :]
    return (((r // GLA_CHUNK) == (c // GLA_CHUNK)) & (r >= c)).astype(BF16)


def _layer(h2d, batch, seq, ffn1_pre_g, ffn1_w_gate, ffn1_w_up, ffn1_w_down, ffn1_post_g,
           mix_pre_g, w_in, b_forget, w_alpha_up, b_alpha, gla_norm_g, w_branch_fox,
           w_branch_gla, w_out, mix_post_g, ffn2_pre_g, ffn2_w_gate, ffn2_w_up, ffn2_w_down,
           ffn2_post_g):
    vec = lambda g: g.reshape(1, -1)
    bf = lambda w: w.astype(BF16)

    splits = [FOX_WIDTH, FOX_WIDTH, FOX_WIDTH, FOX_HEADS, GLA_KEY_WIDTH, GLA_KEY_WIDTH,
              GLA_VAL_WIDTH, GLA_GATE_RANK, GLA_VAL_WIDTH, D_MODEL, D_MODEL]
    offs = [0]
    for s in splits:
        offs.append(offs[-1] + s)
    w_in_t = w_in.T
    col = lambda a, b: w_in_t[offs[a]:offs[b], :]
    wfox = bf(col(0, 3))
    wgla = bf(col(4, 7))
    pad_cols = SMALL_W - FOX_HEADS - GLA_GATE_RANK
    wsm = bf(jnp.concatenate([col(3, 4), col(7, 8), jnp.zeros((pad_cols, D_MODEL), F32)], axis=0))
    bsm = jnp.concatenate([b_forget, jnp.zeros((SMALL_W - FOX_HEADS,), F32)]).reshape(1, SMALL_W)
    wa = bf(jnp.concatenate([jnp.zeros((FOX_HEADS, GLA_KEY_WIDTH), F32), w_alpha_up,
                             jnp.zeros((pad_cols, GLA_KEY_WIDTH), F32)], axis=0))
    wgate = bf(col(8, 11))
    tok = jnp.arange(LANES)
    upper = (tok[:, None] <= tok[None, :]).astype(BF16)
    ltri = _chunk_tril(TT)

    h1, ffn2_w = _ffn(h2d, vec(ffn1_pre_g), bf(ffn1_w_gate), bf(ffn1_w_up), bf(ffn1_w_down),
                      vec(ffn1_post_g), cast=(ffn2_w_gate, ffn2_w_up, ffn2_w_down))

    fq, fk, fv, c, gq, gk, gv, la = _inproj(h1, seq, vec(mix_pre_g), wfox, wgla, wsm, bsm, wa,
                                            vec(b_alpha), upper)
    shape3 = (batch, seq, FOX_WIDTH)
    o_fox, o_gla = _mixers(fq.reshape(shape3), fk.reshape(shape3), fv.reshape(shape3),
                           c.reshape(batch, FOX_HEADS // 2, 2, seq), gq, gk, gv, la, ltri)
    o_fox = o_fox.reshape(batch * seq, FOX_WIDTH)

    h2 = _outproj(h1, o_fox, o_gla, vec(mix_pre_g), wgate, vec(gla_norm_g), bf(w_branch_fox),
                  bf(w_branch_gla), bf(w_out), vec(mix_post_g))

    return _ffn(h2, vec(ffn2_pre_g), *ffn2_w, vec(ffn2_post_g))[0]


def kernel(x, ffn1_pre_g, ffn1_w_gate, ffn1_w_up, ffn1_w_down, ffn1_post_g, mix_pre_g, w_in,
           b_forget, w_alpha_up, b_alpha, gla_norm_g, w_branch_fox, w_branch_gla, w_out,
           mix_post_g, ffn2_pre_g, ffn2_w_gate, ffn2_w_up, ffn2_w_down, ffn2_post_g):
    batch, seq, d = x.shape
    h = x.reshape(batch * seq, d)
    depth = ffn1_pre_g.shape[0]
    for l in range(depth):
        h = _layer(h, batch, seq, ffn1_pre_g[l], ffn1_w_gate[l], ffn1_w_up[l], ffn1_w_down[l],
                   ffn1_post_g[l], mix_pre_g[l], w_in[l], b_forget[l], w_alpha_up[l], b_alpha[l],
                   gla_norm_g[l], w_branch_fox[l], w_branch_gla[l], w_out[l], mix_post_g[l],
                   ffn2_pre_g[l], ffn2_w_gate[l], ffn2_w_up[l], ffn2_w_down[l], ffn2_post_g[l])
    return h.reshape(batch, seq, d)
```

```python
import functools

import jax
import jax.numpy as jnp
from jax import lax
from jax.experimental import pallas as pl
from jax.experimental.pallas import tpu as pltpu

F32 = jnp.float32
BF16 = jnp.bfloat16

D_MODEL = 1024
D_FF = 2816
FOX_HEADS = 8
FOX_HEAD_DIM = 64
FOX_WIDTH = FOX_HEADS * FOX_HEAD_DIM
GLA_HEADS = 4
GLA_DK = 64
GLA_DV = 128
GLA_KEY_WIDTH = GLA_HEADS * GLA_DK
GLA_VAL_WIDTH = GLA_HEADS * GLA_DV
GLA_GATE_RANK = 16
GLA_TAU = 16.0
GLA_CHUNK = 64
NORM_EPS = 1e-6

LANES = 128
HEAD_PAIR = LANES
VMEM_LIMIT_BYTES = 56 * 1024 * 1024

IN_TM = 1024
IN_SUB = 512
FFN_TM = 1024
FF_CHUNK = 256
OUT_TM = 1024
OUT_SUB = 512
FFN_SUB = 512
TQ = 256
FOX_WIDE = 512
FOX_ONES_ROWS = 16
FOX_PAIRS = 4
FOX_GROUP = 32
FOX_AHEAD = 4
LOG2_E = 1.4426950408889634
TT = 256
SMALL_W = LANES
CUM_ROWS = 16


def _rms(x, g):
    return x * lax.rsqrt(jnp.mean(x * x, axis=-1, keepdims=True) + NORM_EPS) * g


def _log_sigmoid(x):
    return jnp.minimum(x, 0.0) - jnp.log(1.0 + jnp.exp(-jnp.abs(x)))


def _dot(a, b):
    return jnp.dot(a, b, preferred_element_type=F32)


def _dot_nt(a, b):
    return lax.dot_general(a, b, (((1,), (1,)), ((), ())), preferred_element_type=F32)


def _dot_tn(a, b):
    return lax.dot_general(a, b, (((0,), (0,)), ((), ())), preferred_element_type=F32)


def _split2(x):
    hi = x.astype(BF16)
    lo = (x - hi.astype(F32)).astype(BF16)
    return hi, lo


def _split3(x):
    hi = x.astype(BF16)
    r = x - hi.astype(F32)
    mid = r.astype(BF16)
    lo = (r - mid.astype(F32)).astype(BF16)
    return hi, mid, lo


def _const_spec(shape):
    return pl.BlockSpec(shape, lambda *_: (0,) * len(shape), pipeline_mode=pl.Buffered(1))


def _params(n_axes):
    return pltpu.CompilerParams(dimension_semantics=("arbitrary",) * n_axes,
                                vmem_limit_bytes=VMEM_LIMIT_BYTES)


def _ffn_kernel(n_cast, x_ref, gpre_ref, wg_ref, wu_ref, wd_ref, gpost_ref, *rest):
    cast_in, o_ref, cast_out = rest[:n_cast], rest[n_cast], rest[n_cast + 1:]
    subs = [slice(s * FFN_SUB, (s + 1) * FFN_SUB) for s in range(FFN_TM // FFN_SUB)]
    xn = [_rms(x_ref[rows, :], gpre_ref[...]).astype(BF16) for rows in subs]
    acc = [None] * len(subs)
    for c in range(D_FF // FF_CHUNK):
        cols = slice(c * FF_CHUNK, (c + 1) * FF_CHUNK)
        for s in range(len(subs)):
            g = _dot(xn[s], wg_ref[:, cols])
            u = _dot(xn[s], wu_ref[:, cols])
            h = (g * jax.nn.sigmoid(g) * u).astype(BF16)
            part = _dot(h, wd_ref[cols, :])
            acc[s] = part if acc[s] is None else acc[s] + part
    for s, rows in enumerate(subs):
        o_ref[rows, :] = x_ref[rows, :] + 0.5 * _rms(acc[s], gpost_ref[...])
    for src, dst in zip(cast_in, cast_out):
        dst[...] = src[...].astype(BF16)


def _cast_spec(shape, steps):
    rows, cols = shape
    span = 1 if (rows // steps) % 16 == 0 else 2
    assert rows % steps == 0 and (span * rows // steps) % 16 == 0
    return pl.BlockSpec((span * rows // steps, cols), lambda i: (i // span, 0))


def _ffn(x2d, g_pre, w_gate, w_up, w_down, g_post, cast=()):
    t = x2d.shape[0]
    steps = t // FFN_TM
    row = pl.BlockSpec((FFN_TM, D_MODEL), lambda i: (i, 0))
    cast_specs = [_cast_spec(w.shape, steps) for w in cast]
    out = pl.pallas_call(
        functools.partial(_ffn_kernel, len(cast)),
        grid=(steps,),
        in_specs=[row, _const_spec((1, D_MODEL)), _const_spec((D_MODEL, D_FF)),
                  _const_spec((D_MODEL, D_FF)), _const_spec((D_FF, D_MODEL)),
                  _const_spec((1, D_MODEL))] + cast_specs,
        out_specs=[row] + cast_specs,
        out_shape=[jax.ShapeDtypeStruct((t, D_MODEL), F32)]
        + [jax.ShapeDtypeStruct(w.shape, BF16) for w in cast],
        compiler_params=_params(1),
        name="ffn",
    )(x2d, g_pre, w_gate, w_up, w_down, g_post, *cast)
    return out[0], tuple(out[1:])


def _inproj_kernel(tiles_per_seq, h_ref, g_ref, wfox_ref, wgla_ref, wsm_ref, bsm_ref, wa_ref,
                   ba_ref, upper_ref, fq_ref, fk_ref, fv_ref, c_ref, gq_ref, gk_ref, gv_ref,
                   la_ref, carry_ref):
    @pl.when(pl.program_id(0) % tiles_per_seq == 0)
    def _():
        carry_ref[...] = jnp.zeros_like(carry_ref)

    subs = [slice(s * IN_SUB, (s + 1) * IN_SUB) for s in range(IN_TM // IN_SUB)]
    scale = FOX_HEAD_DIM ** -0.5 * LOG2_E
    u = [_rms(h_ref[rows, :], g_ref[...]).astype(BF16) for rows in subs]

    zs = [_dot_nt(us, wsm_ref[...]) for us in u]

    for s, rows in enumerate(subs):
        zf = _dot_nt(u[s], wfox_ref[...])
        fq_ref[rows, :] = (zf[:, :FOX_WIDTH] * scale).astype(BF16)
        fk_ref[rows, :] = zf[:, FOX_WIDTH:2 * FOX_WIDTH].astype(BF16)
        fv_ref[rows, :] = zf[:, 2 * FOX_WIDTH:].astype(BF16)

    for s, rows in enumerate(subs):
        a = _dot(zs[s].astype(BF16), wa_ref[...]) + ba_ref[...]
        la_ref[rows, :] = _log_sigmoid(a) * (1.0 / GLA_TAU)

    for s, rows in enumerate(subs):
        zg = _dot_nt(u[s], wgla_ref[...])
        gq_ref[rows, :] = zg[:, :GLA_KEY_WIDTH] * (GLA_DK ** -0.5)
        gk_ref[rows, :] = zg[:, GLA_KEY_WIDTH:2 * GLA_KEY_WIDTH]
        gv_ref[rows, :] = zg[:, 2 * GLA_KEY_WIDTH:].astype(BF16)

    upper = upper_ref[...]
    carry = carry_ref[...]
    n_blocks = IN_SUB // LANES
    for s, rows in enumerate(subs):
        lf_t = _log_sigmoid(zs[s] + bsm_ref[...]).T[:CUM_ROWS]
        parts = jnp.concatenate(_split3(lf_t), axis=0)
        stacked = jnp.concatenate(
            [parts[:, k * LANES:(k + 1) * LANES] for k in range(n_blocks)], axis=0)
        sums = _dot(stacked, upper)
        c_blocks = []
        for k in range(n_blocks):
            blk = sums[k * 3 * CUM_ROWS:(k + 1) * 3 * CUM_ROWS]
            c_blocks.append(blk[:CUM_ROWS] + blk[CUM_ROWS:2 * CUM_ROWS] + blk[2 * CUM_ROWS:] + carry)
            carry = carry + jnp.sum(lf_t[:, k * LANES:(k + 1) * LANES], axis=1, keepdims=True)
        c_ref[0, :, rows] = jnp.concatenate(c_blocks, axis=1)[:FOX_HEADS] * LOG2_E
    carry_ref[...] = carry


def _inproj(h2d, seq, g, wfox, wgla, wsm, bsm, wa, ba, upper):
    t = h2d.shape[0]
    batch = t // seq
    tiles_per_seq = seq // IN_TM
    row = lambda w: pl.BlockSpec((IN_TM, w), lambda i: (i, 0))
    out_shapes = (
        jax.ShapeDtypeStruct((t, FOX_WIDTH), BF16),
        jax.ShapeDtypeStruct((t, FOX_WIDTH), BF16),
        jax.ShapeDtypeStruct((t, FOX_WIDTH), BF16),
        jax.ShapeDtypeStruct((batch, FOX_HEADS, seq), F32),
        jax.ShapeDtypeStruct((t, GLA_KEY_WIDTH), F32),
        jax.ShapeDtypeStruct((t, GLA_KEY_WIDTH), F32),
        jax.ShapeDtypeStruct((t, GLA_VAL_WIDTH), BF16),
        jax.ShapeDtypeStruct((t, GLA_KEY_WIDTH), F32),
    )
    out_specs = (
        row(FOX_WIDTH), row(FOX_WIDTH), row(FOX_WIDTH),
        pl.BlockSpec((1, FOX_HEADS, IN_TM), lambda i: (i // tiles_per_seq, 0, i % tiles_per_seq)),
        row(GLA_KEY_WIDTH), row(GLA_KEY_WIDTH), row(GLA_VAL_WIDTH), row(GLA_KEY_WIDTH),
    )
    return pl.pallas_call(
        functools.partial(_inproj_kernel, tiles_per_seq),
        grid=(t // IN_TM,),
        in_specs=[row(D_MODEL), _const_spec((1, D_MODEL)),
                  _const_spec((3 * FOX_WIDTH, D_MODEL)),
                  _const_spec((2 * GLA_KEY_WIDTH + GLA_VAL_WIDTH, D_MODEL)),
                  _const_spec((SMALL_W, D_MODEL)), _const_spec((1, SMALL_W)),
                  _const_spec((SMALL_W, GLA_KEY_WIDTH)), _const_spec((1, GLA_KEY_WIDTH)),
                  _const_spec((LANES, LANES))],
        out_specs=out_specs,
        out_shape=out_shapes,
        scratch_shapes=[pltpu.VMEM((CUM_ROWS, LANES), F32)],
        compiler_params=_params(1),
        name="inproj",
    )(h2d, g, wfox, wgla, wsm, bsm, wa, ba, upper)


def _fox_kernel(seq, q_ref, k_ref, v_ref, c_ref, o_ref, kaug_ref, qaug_ref, vt_ref):
    lane = lax.broadcasted_iota(jnp.int32, (1, HEAD_PAIR), 1)
    row = lax.broadcasted_iota(jnp.int32, (HEAD_PAIR, 1), 0)
    n_heads = 2 * FOX_PAIRS
    for pair in range(FOX_PAIRS):
        pair_lanes = slice(pair * HEAD_PAIR, (pair + 1) * HEAD_PAIR)
        k_all = k_ref[0, :, pair_lanes]
        q_t = q_ref[0, :, pair_lanes].astype(F32).T
        v_t = v_ref[0, :, pair_lanes].astype(F32).T
        for j in range(2):
            h = 2 * pair + j
            lo_lane, hi_lane = j * FOX_HEAD_DIM, (j + 1) * FOX_HEAD_DIM
            bias0 = (1 - j) * FOX_HEAD_DIM
            parts = [p.astype(F32) for p in _split3(-c_ref[0, pair, j:j + 1, :])]
            bias_t = jnp.zeros((HEAD_PAIR, seq), F32)
            for i, part in enumerate(parts):
                bias_t = jnp.where(row == bias0 + i, part, bias_t)
            own_lane = (lane >= lo_lane) & (lane < hi_lane)
            kaug_ref[h] = jnp.where(own_lane, k_all, bias_t.T.astype(BF16))
            own_row = (row >= lo_lane) & (row < hi_lane)
            one_row = (row >= bias0) & (row < bias0 + len(parts))
            qaug_ref[h] = jnp.where(own_row, q_t, jnp.where(one_row, 1.0, 0.0)).astype(BF16)
            vt_ref[h] = jnp.concatenate(
                [v_t[lo_lane:hi_lane], jnp.ones((FOX_ONES_ROWS, seq), F32)], axis=0).astype(BF16)

    chains = []
    for qi in reversed(range(seq // TQ)):
        q0 = qi * TQ
        for h in range(n_heads):
            steps, k0 = [], 0
            while k0 + FOX_WIDE <= q0:
                steps.append((k0, FOX_WIDE, False))
                k0 += FOX_WIDE
            if k0 < q0:
                steps.append((k0, q0 - k0, False))
            steps.append((q0, TQ, True))
            chains.append((q0, h, steps))
    items = []
    for g in range(0, len(chains), FOX_GROUP):
        group = chains[g:g + FOX_GROUP]
        for t in range(max(len(c[2]) for c in group)):
            for q0, h, steps in group:
                if t < len(steps):
                    items.append((q0, h) + steps[t])

    def scores(item):
        q0, h, k0, width, _ = item
        return _dot(kaug_ref[h, k0:k0 + width, :], qaug_ref[h, :, q0:q0 + TQ])

    state, out_t = {}, {}
    pending = [scores(item) for item in items[:FOX_AHEAD]]
    for t, item in enumerate(items):
        q0, h, k0, width, diagonal = item
        s_t = pending.pop(0)
        if t + FOX_AHEAD < len(items):
            pending.append(scores(items[t + FOX_AHEAD]))
        m, acc = state.get((q0, h), (jnp.full((1, TQ), -jnp.inf, F32),
                                     jnp.zeros((FOX_HEAD_DIM + FOX_ONES_ROWS, TQ), F32)))
        if diagonal:
            key = lax.broadcasted_iota(jnp.int32, (width, TQ), 0) + k0
            qry = lax.broadcasted_iota(jnp.int32, (width, TQ), 1) + q0
            s_t = jnp.where(key <= qry, s_t, -jnp.inf)
        m_next = jnp.maximum(m, jnp.max(s_t, axis=0, keepdims=True))
        p_t = jnp.exp2(s_t - m_next)
        acc = acc * jnp.exp2(m - m_next) + _dot(vt_ref[h, :, k0:k0 + width], p_t.astype(BF16))
        state[(q0, h)] = (m_next, acc)
        if diagonal:
            denom = acc[FOX_HEAD_DIM:FOX_HEAD_DIM + 8]
            out_t[(q0, h)] = acc[:FOX_HEAD_DIM] / jnp.concatenate(
                [denom] * (FOX_HEAD_DIM // 8), axis=0)
            if (q0, h ^ 1) in out_t:
                pair = h // 2
                both = jnp.concatenate([out_t[(q0, 2 * pair)], out_t[(q0, 2 * pair + 1)]], axis=0)
                o_ref[0, q0:q0 + TQ, pair * HEAD_PAIR:(pair + 1) * HEAD_PAIR] = both.T.astype(BF16)


def _gla_kernel(seq, q_ref, k_ref, v_ref, la_ref, ltri_ref, o_ref):
    n_chunks = TT // GLA_CHUNK
    lane = lax.broadcasted_iota(jnp.int32, (1, HEAD_PAIR), 1)
    head_mask = (lane < GLA_DK, lane >= GLA_DK)
    row_chunk = lax.broadcasted_iota(jnp.int32, (TT, 1), 0) // GLA_CHUNK
    rr = lax.broadcasted_iota(jnp.int32, (TT, TT), 0)
    cc = lax.broadcasted_iota(jnp.int32, (TT, TT), 1)
    intra = (rr >= cc) & (rr // GLA_CHUNK == cc // GLA_CHUNK)
    ltri = ltri_ref[...]

    n_pairs = GLA_HEADS // 2
    n_tiles = seq // TT

    def rows_of(t):
        return slice(t * TT, (t + 1) * TT)

    def cumulate(t):
        hi, lo = _split2(la_ref[rows_of(t), :])
        return _dot(ltri, hi) + _dot(ltri, lo)

    def products(t, b_all):
        rows = rows_of(t)
        out = []
        for p in range(n_pairs):
            ksl = slice(p * HEAD_PAIR, (p + 1) * HEAD_PAIR)
            b = b_all[:, ksl]
            last_rows = [b[(i + 1) * GLA_CHUNK - 1:(i + 1) * GLA_CHUNK, :] for i in range(n_chunks)]
            b_last = jnp.concatenate(
                [jnp.broadcast_to(r, (GLA_CHUNK, HEAD_PAIR)) for r in last_rows], axis=0)
            q_in = (q_ref[rows, ksl] * jnp.exp(b)).astype(BF16)
            k = k_ref[rows, ksl]
            k_in = (k * jnp.exp(-b)).astype(BF16)
            k_st = (k * jnp.exp(b_last - b)).astype(BF16)
            kv = None
            attn = []
            for j in range(2):
                h = 2 * p + j
                vj = v_ref[rows, h * GLA_DV:(h + 1) * GLA_DV]
                kcat = jnp.concatenate(
                    [jnp.where((row_chunk == i) & head_mask[j], k_st, jnp.zeros_like(k_st))
                     for i in range(n_chunks)], axis=1)
                contrib = _dot_tn(vj, kcat)
                kv = contrib if kv is None else kv + contrib
                qh = jnp.where(head_mask[j], q_in, jnp.zeros_like(q_in))
                attn.append(_dot_nt(qh, k_in))
            out.append((q_in, [jnp.exp(r) for r in last_rows], kv, attn))
        return out

    state = [jnp.zeros((GLA_DV, HEAD_PAIR), F32) for _ in range(n_pairs)]

    def finish(t, prods):
        rows = rows_of(t)
        for p in range(n_pairs):
            q_in, decays, kv, attn = prods[p]
            o_inter = []
            for i in range(n_chunks):
                st = state[p].astype(BF16)
                st2 = jnp.concatenate(
                    [jnp.where(head_mask[j], st, jnp.zeros_like(st)) for j in range(2)], axis=0)
                o_inter.append(_dot_nt(q_in[i * GLA_CHUNK:(i + 1) * GLA_CHUNK, :], st2))
                state[p] = state[p] * decays[i] + kv[:, i * HEAD_PAIR:(i + 1) * HEAD_PAIR]
            o_intra = []
            for j in range(2):
                h = 2 * p + j
                vj = v_ref[rows, h * GLA_DV:(h + 1) * GLA_DV]
                o_intra.append(_dot(jnp.where(intra, attn[j], 0.0).astype(BF16), vj))
            o_ref[rows, 2 * p * GLA_DV:(2 * p + 2) * GLA_DV] = (
                jnp.concatenate(o_intra, axis=1) + jnp.concatenate(o_inter, axis=0))

    b_alls, prods = {}, {}
    for step in range(n_tiles + 2):
        if step < n_tiles:
            b_alls[step] = cumulate(step)
        if 0 <= step - 1 < n_tiles:
            prods[step - 1] = products(step - 1, b_alls.pop(step - 1))
        if 0 <= step - 2 < n_tiles:
            finish(step - 2, prods.pop(step - 2))


def _mixers_kernel(seq, fq_ref, fk_ref, fv_ref, c_ref, gq_ref, gk_ref, gv_ref, la_ref, ltri_ref,
                   ofox_ref, ogla_ref, kaug_ref, qaug_ref, vt_ref):
    _fox_kernel(seq, fq_ref, fk_ref, fv_ref, c_ref, ofox_ref, kaug_ref, qaug_ref, vt_ref)
    _gla_kernel(seq, gq_ref, gk_ref, gv_ref, la_ref, ltri_ref, ogla_ref)


def _mixers(fq, fk, fv, c, gq, gk, gv, la, ltri):
    batch, seq, _ = fq.shape
    assert FOX_PAIRS * HEAD_PAIR == FOX_WIDTH
    n_heads = 2 * FOX_PAIRS
    qkv = pl.BlockSpec((1, seq, FOX_WIDTH), lambda b: (b, 0, 0))
    row = lambda w: pl.BlockSpec((seq, w), lambda b: (b, 0))
    return pl.pallas_call(
        functools.partial(_mixers_kernel, seq),
        grid=(batch,),
        in_specs=[qkv, qkv, qkv, pl.BlockSpec((1, FOX_PAIRS, 2, seq), lambda b: (b, 0, 0, 0)),
                  row(GLA_KEY_WIDTH), row(GLA_KEY_WIDTH), row(GLA_VAL_WIDTH), row(GLA_KEY_WIDTH),
                  _const_spec((TT, TT))],
        out_specs=[qkv, row(GLA_VAL_WIDTH)],
        out_shape=[jax.ShapeDtypeStruct((batch, seq, FOX_WIDTH), BF16),
                   jax.ShapeDtypeStruct((batch * seq, GLA_VAL_WIDTH), F32)],
        scratch_shapes=[pltpu.VMEM((n_heads, seq, HEAD_PAIR), BF16),
                        pltpu.VMEM((n_heads, HEAD_PAIR, seq), BF16),
                        pltpu.VMEM((n_heads, FOX_HEAD_DIM + FOX_ONES_ROWS, seq), BF16)],
        compiler_params=_params(1),
        name="mixers",
    )(fq, fk, fv, c, gq, gk, gv, la, ltri)


def _outproj_kernel(h_ref, ofox_ref, ogla_ref, gpre_ref, wgate_ref, gnorm_ref, wbf_ref, wbg_ref,
                    wout_ref, gpost_ref, o_ref):
    subs = [slice(s * OUT_SUB, (s + 1) * OUT_SUB) for s in range(OUT_TM // OUT_SUB)]
    u = [_rms(h_ref[rows, :], gpre_ref[...]).astype(BF16) for rows in subs]
    gates = [_dot_nt(us, wgate_ref[...]) for us in u]
    branch_fox = [_dot(ofox_ref[rows, :], wbf_ref[...]) for rows in subs]
    branch_gla = []
    for s, rows in enumerate(subs):
        g_r = gates[s][:, :GLA_VAL_WIDTH]
        heads = []
        for hd in range(GLA_HEADS):
            sl = slice(hd * GLA_DV, (hd + 1) * GLA_DV)
            heads.append(_rms(ogla_ref[rows, sl], gnorm_ref[:, sl]))
        o_gla = jnp.concatenate(heads, axis=1)
        o_gla = (o_gla * (g_r * jax.nn.sigmoid(g_r))).astype(BF16)
        branch_gla.append(_dot(o_gla, wbg_ref[...]))
    merged = []
    for s in range(len(subs)):
        gate_fox = gates[s][:, GLA_VAL_WIDTH:GLA_VAL_WIDTH + D_MODEL]
        gate_gla = gates[s][:, GLA_VAL_WIDTH + D_MODEL:]
        y = jax.nn.sigmoid(gate_fox) * branch_fox[s] + jax.nn.sigmoid(gate_gla) * branch_gla[s]
        merged.append(_dot(y.astype(BF16), wout_ref[...]))
    for s, rows in enumerate(subs):
        o_ref[rows, :] = h_ref[rows, :] + _rms(merged[s], gpost_ref[...])


def _outproj(h2d, o_fox, o_gla, g_pre, wgate, gnorm, wbf, wbg, wout, g_post):
    t = h2d.shape[0]
    row = lambda w: pl.BlockSpec((OUT_TM, w), lambda i: (i, 0))
    return pl.pallas_call(
        _outproj_kernel,
        grid=(t // OUT_TM,),
        in_specs=[row(D_MODEL), row(FOX_WIDTH), row(GLA_VAL_WIDTH), _const_spec((1, D_MODEL)),
                  _const_spec((GLA_VAL_WIDTH + 2 * D_MODEL, D_MODEL)),
                  _const_spec((1, GLA_VAL_WIDTH)),
                  _const_spec((FOX_WIDTH, D_MODEL)), _const_spec((GLA_VAL_WIDTH, D_MODEL)),
                  _const_spec((D_MODEL, D_MODEL)), _const_spec((1, D_MODEL))],
        out_specs=row(D_MODEL),
        out_shape=jax.ShapeDtypeStruct((t, D_MODEL), F32),
        compiler_params=_params(1),
        name="outproj",
    )(h2d, o_fox, o_gla, g_pre, wgate, gnorm, wbf, wbg, wout, g_post)


def _chunk_tril(n):
    r = jnp.arange(n)[:, None]
    c = jnp.arange(n)[None, :]
    return (((r // GLA_CHUNK) == (c // GLA_CHUNK)) & (r >= c)).astype(BF16)


def _layer(h2d, batch, seq, ffn1_pre_g, ffn1_w_gate, ffn1_w_up, ffn1_w_down, ffn1_post_g,
           mix_pre_g, w_in, b_forget, w_alpha_up, b_alpha, gla_norm_g, w_branch_fox,
           w_branch_gla, w_out, mix_post_g, ffn2_pre_g, ffn2_w_gate, ffn2_w_up, ffn2_w_down,
           ffn2_post_g):
    vec = lambda g: g.reshape(1, -1)
    bf = lambda w: w.astype(BF16)

    splits = [FOX_WIDTH, FOX_WIDTH, FOX_WIDTH, FOX_HEADS, GLA_KEY_WIDTH, GLA_KEY_WIDTH,
              GLA_VAL_WIDTH, GLA_GATE_RANK, GLA_VAL_WIDTH, D_MODEL, D_MODEL]
    offs = [0]
    for s in splits:
        offs.append(offs[-1] + s)
    w_in_t = w_in.T
    col = lambda a, b: w_in_t[offs[a]:offs[b], :]
    wfox = bf(col(0, 3))
    wgla = bf(col(4, 7))
    pad_cols = SMALL_W - FOX_HEADS - GLA_GATE_RANK
    wsm = bf(jnp.concatenate([col(3, 4), col(7, 8), jnp.zeros((pad_cols, D_MODEL), F32)], axis=0))
    bsm = jnp.concatenate([b_forget, jnp.zeros((SMALL_W - FOX_HEADS,), F32)]).reshape(1, SMALL_W)
    wa = bf(jnp.concatenate([jnp.zeros((FOX_HEADS, GLA_KEY_WIDTH), F32), w_alpha_up,
                             jnp.zeros((pad_cols, GLA_KEY_WIDTH), F32)], axis=0))
    wgate = bf(col(8, 11))
    tok = jnp.arange(LANES)
    upper = (tok[:, None] <= tok[None, :]).astype(BF16)
    ltri = _chunk_tril(TT)

    h1, ffn2_w = _ffn(h2d, vec(ffn1_pre_g), bf(ffn1_w_gate), bf(ffn1_w_up), bf(ffn1_w_down),
                      vec(ffn1_post_g), cast=(ffn2_w_gate, ffn2_w_up, ffn2_w_down))

    fq, fk, fv, c, gq, gk, gv, la = _inproj(h1, seq, vec(mix_pre_g), wfox, wgla, wsm, bsm, wa,
                                            vec(b_alpha), upper)
    shape3 = (batch, seq, FOX_WIDTH)
    o_fox, o_gla = _mixers(fq.reshape(shape3), fk.reshape(shape3), fv.reshape(shape3),
                           c.reshape(batch, FOX_HEADS // 2, 2, seq), gq, gk, gv, la, ltri)
    o_fox = o_fox.reshape(batch * seq, FOX_WIDTH)

    h2 = _outproj(h1, o_fox, o_gla, vec(mix_pre_g), wgate, vec(gla_norm_g), bf(w_branch_fox),
                  bf(w_branch_gla), bf(w_out), vec(mix_post_g))

    return _ffn(h2, vec(ffn2_pre_g), *ffn2_w, vec(ffn2_post_g))[0]


def kernel(x, ffn1_pre_g, ffn1_w_gate, ffn1_w_up, ffn1_w_down, ffn1_post_g, mix_pre_g, w_in,
           b_forget, w_alpha_up, b_alpha, gla_norm_g, w_branch_fox, w_branch_gla, w_out,
           mix_post_g, ffn2_pre_g, ffn2_w_gate, ffn2_w_up, ffn2_w_down, ffn2_post_g):
    batch, seq, d = x.shape
    h = x.reshape(batch * seq, d)
    depth = ffn1_pre_g.shape[0]
    for l in range(depth):
        h = _layer(h, batch, seq, ffn1_pre_g[l], ffn1_w_gate[l], ffn1_w_up[l], ffn1_w_down[l],
                   ffn1_post_g[l], mix_pre_g[l], w_in[l], b_forget[l], w_alpha_up[l], b_alpha[l],
                   gla_norm_g[l], w_branch_fox[l], w_branch_gla[l], w_out[l], mix_post_g[l],
                   ffn2_pre_g[l], ffn2_w_gate[l], ffn2_w_up[l], ffn2_w_down[l], ffn2_post_g[l])
    return h.reshape(batch, seq, d)
```

```python
import functools

import jax
import jax.numpy as jnp
from jax import lax
from jax.experimental import pallas as pl
from jax.experimental.pallas import tpu as pltpu

F32 = jnp.float32
BF16 = jnp.bfloat16

D_MODEL = 1024
D_FF = 2816
FOX_HEADS = 8
FOX_HEAD_DIM = 64
FOX_WIDTH = FOX_HEADS * FOX_HEAD_DIM
GLA_HEADS = 4
GLA_DK = 64
GLA_DV = 128
GLA_KEY_WIDTH = GLA_HEADS * GLA_DK
GLA_VAL_WIDTH = GLA_HEADS * GLA_DV
GLA_GATE_RANK = 16
GLA_TAU = 16.0
GLA_CHUNK = 64
NORM_EPS = 1e-6

LANES = 128
HEAD_PAIR = LANES
VMEM_LIMIT_BYTES = 56 * 1024 * 1024

IN_TM = 1024
IN_SUB = 512
FFN_TM = 1024
FF_CHUNK = 256
OUT_TM = 1024
OUT_SUB = 512
FFN_SUB = 512
TQ = 256
FOX_WIDE = 512
FOX_ONES_ROWS = 16
FOX_PAIRS = 4
FOX_GROUP = 16
FOX_AHEAD = 4
LOG2_E = 1.4426950408889634
TT = 256
SMALL_W = LANES
CUM_ROWS = 16


def _rms(x, g):
    return x * lax.rsqrt(jnp.mean(x * x, axis=-1, keepdims=True) + NORM_EPS) * g


def _log_sigmoid(x):
    return jnp.minimum(x, 0.0) - jnp.log(1.0 + jnp.exp(-jnp.abs(x)))


def _dot(a, b):
    return jnp.dot(a, b, preferred_element_type=F32)


def _dot_nt(a, b):
    return lax.dot_general(a, b, (((1,), (1,)), ((), ())), preferred_element_type=F32)


def _dot_tn(a, b):
    return lax.dot_general(a, b, (((0,), (0,)), ((), ())), preferred_element_type=F32)


def _split2(x):
    hi = x.astype(BF16)
    lo = (x - hi.astype(F32)).astype(BF16)
    return hi, lo


def _split3(x):
    hi = x.astype(BF16)
    r = x - hi.astype(F32)
    mid = r.astype(BF16)
    lo = (r - mid.astype(F32)).astype(BF16)
    return hi, mid, lo


def _const_spec(shape):
    return pl.BlockSpec(shape, lambda *_: (0,) * len(shape), pipeline_mode=pl.Buffered(1))


def _params(n_axes, fuse_inputs=None):
    return pltpu.CompilerParams(dimension_semantics=("arbitrary",) * n_axes,
                                vmem_limit_bytes=VMEM_LIMIT_BYTES,
                                allow_input_fusion=fuse_inputs)


def _ffn_kernel(n_cast, x_ref, gpre_ref, wg_ref, wu_ref, wd_ref, gpost_ref, *rest):
    cast_in, o_ref, cast_out = rest[:n_cast], rest[n_cast], rest[n_cast + 1:]
    subs = [slice(s * FFN_SUB, (s + 1) * FFN_SUB) for s in range(FFN_TM // FFN_SUB)]
    xn = [_rms(x_ref[rows, :], gpre_ref[...]).astype(BF16) for rows in subs]
    acc = [None] * len(subs)
    for c in range(D_FF // FF_CHUNK):
        cols = slice(c * FF_CHUNK, (c + 1) * FF_CHUNK)
        for s in range(len(subs)):
            g = _dot(xn[s], wg_ref[:, cols])
            u = _dot(xn[s], wu_ref[:, cols])
            h = (g * jax.nn.sigmoid(g) * u).astype(BF16)
            part = _dot(h, wd_ref[cols, :])
            acc[s] = part if acc[s] is None else acc[s] + part
    for s, rows in enumerate(subs):
        o_ref[rows, :] = x_ref[rows, :] + 0.5 * _rms(acc[s], gpost_ref[...])
    for src, dst in zip(cast_in, cast_out):
        dst[...] = src[...].astype(BF16)


def _cast_spec(shape, steps):
    rows, cols = shape
    span = 1 if (rows // steps) % 16 == 0 else 2
    assert rows % steps == 0 and (span * rows // steps) % 16 == 0
    return pl.BlockSpec((span * rows // steps, cols), lambda i: (i // span, 0))


def _ffn(x2d, g_pre, w_gate, w_up, w_down, g_post, cast=()):
    t = x2d.shape[0]
    steps = t // FFN_TM
    row = pl.BlockSpec((FFN_TM, D_MODEL), lambda i: (i, 0))
    cast_specs = [_cast_spec(w.shape, steps) for w in cast]
    out = pl.pallas_call(
        functools.partial(_ffn_kernel, len(cast)),
        grid=(steps,),
        in_specs=[row, _const_spec((1, D_MODEL)), _const_spec((D_MODEL, D_FF)),
                  _const_spec((D_MODEL, D_FF)), _const_spec((D_FF, D_MODEL)),
                  _const_spec((1, D_MODEL))] + cast_specs,
        out_specs=[row] + cast_specs,
        out_shape=[jax.ShapeDtypeStruct((t, D_MODEL), F32)]
        + [jax.ShapeDtypeStruct(w.shape, BF16) for w in cast],
        compiler_params=_params(1, [False, False, True, True, True, False] + [False] * len(cast)),
        name="ffn",
    )(x2d, g_pre, w_gate, w_up, w_down, g_post, *cast)
    return out[0], tuple(out[1:])


def _inproj_kernel(tiles_per_seq, h_ref, g_ref, wfox_ref, wgla_ref, wsm_ref, bsm_ref, wa_ref,
                   ba_ref, upper_ref, fq_ref, fk_ref, fv_ref, c_ref, gq_ref, gk_ref, gv_ref,
                   la_ref, carry_ref):
    @pl.when(pl.program_id(0) % tiles_per_seq == 0)
    def _():
        carry_ref[...] = jnp.zeros_like(carry_ref)

    subs = [slice(s * IN_SUB, (s + 1) * IN_SUB) for s in range(IN_TM // IN_SUB)]
    scale = FOX_HEAD_DIM ** -0.5 * LOG2_E
    u = [_rms(h_ref[rows, :], g_ref[...]).astype(BF16) for rows in subs]

    zs = [_dot_nt(us, wsm_ref[...]) for us in u]

    for s, rows in enumerate(subs):
        zf = _dot_nt(u[s], wfox_ref[...])
        fq_ref[rows, :] = (zf[:, :FOX_WIDTH] * scale).astype(BF16)
        fk_ref[rows, :] = zf[:, FOX_WIDTH:2 * FOX_WIDTH].astype(BF16)
        fv_ref[rows, :] = zf[:, 2 * FOX_WIDTH:].astype(BF16)

    for s, rows in enumerate(subs):
        a = _dot(zs[s].astype(BF16), wa_ref[...]) + ba_ref[...]
        la_ref[rows, :] = _log_sigmoid(a) * (1.0 / GLA_TAU)

    for s, rows in enumerate(subs):
        zg = _dot_nt(u[s], wgla_ref[...])
        gq_ref[rows, :] = zg[:, :GLA_KEY_WIDTH] * (GLA_DK ** -0.5)
        gk_ref[rows, :] = zg[:, GLA_KEY_WIDTH:2 * GLA_KEY_WIDTH]
        gv_ref[rows, :] = zg[:, 2 * GLA_KEY_WIDTH:].astype(BF16)

    upper = upper_ref[...]
    carry = carry_ref[...]
    n_blocks = IN_SUB // LANES
    for s, rows in enumerate(subs):
        lf_t = _log_sigmoid(zs[s] + bsm_ref[...]).T[:CUM_ROWS]
        parts = jnp.concatenate(_split3(lf_t), axis=0)
        stacked = jnp.concatenate(
            [parts[:, k * LANES:(k + 1) * LANES] for k in range(n_blocks)], axis=0)
        sums = _dot(stacked, upper)
        c_blocks = []
        for k in range(n_blocks):
            blk = sums[k * 3 * CUM_ROWS:(k + 1) * 3 * CUM_ROWS]
            c_blocks.append(blk[:CUM_ROWS] + blk[CUM_ROWS:2 * CUM_ROWS] + blk[2 * CUM_ROWS:] + carry)
            carry = carry + jnp.sum(lf_t[:, k * LANES:(k + 1) * LANES], axis=1, keepdims=True)
        c_ref[0, :, rows] = jnp.concatenate(c_blocks, axis=1)[:FOX_HEADS] * LOG2_E
    carry_ref[...] = carry


def _inproj(h2d, seq, g, wfox, wgla, wsm, bsm, wa, ba, upper):
    t = h2d.shape[0]
    batch = t // seq
    tiles_per_seq = seq // IN_TM
    row = lambda w: pl.BlockSpec((IN_TM, w), lambda i: (i, 0))
    out_shapes = (
        jax.ShapeDtypeStruct((t, FOX_WIDTH), BF16),
        jax.ShapeDtypeStruct((t, FOX_WIDTH), BF16),
        jax.ShapeDtypeStruct((t, FOX_WIDTH), BF16),
        jax.ShapeDtypeStruct((batch, FOX_HEADS, seq), F32),
        jax.ShapeDtypeStruct((t, GLA_KEY_WIDTH), F32),
        jax.ShapeDtypeStruct((t, GLA_KEY_WIDTH), F32),
        jax.ShapeDtypeStruct((t, GLA_VAL_WIDTH), BF16),
        jax.ShapeDtypeStruct((t, GLA_KEY_WIDTH), F32),
    )
    out_specs = (
        row(FOX_WIDTH), row(FOX_WIDTH), row(FOX_WIDTH),
        pl.BlockSpec((1, FOX_HEADS, IN_TM), lambda i: (i // tiles_per_seq, 0, i % tiles_per_seq)),
        row(GLA_KEY_WIDTH), row(GLA_KEY_WIDTH), row(GLA_VAL_WIDTH), row(GLA_KEY_WIDTH),
    )
    return pl.pallas_call(
        functools.partial(_inproj_kernel, tiles_per_seq),
        grid=(t // IN_TM,),
        in_specs=[row(D_MODEL), _const_spec((1, D_MODEL)),
                  _const_spec((3 * FOX_WIDTH, D_MODEL)),
                  _const_spec((2 * GLA_KEY_WIDTH + GLA_VAL_WIDTH, D_MODEL)),
                  _const_spec((SMALL_W, D_MODEL)), _const_spec((1, SMALL_W)),
                  _const_spec((SMALL_W, GLA_KEY_WIDTH)), _const_spec((1, GLA_KEY_WIDTH)),
                  _const_spec((LANES, LANES))],
        out_specs=out_specs,
        out_shape=out_shapes,
        scratch_shapes=[pltpu.VMEM((CUM_ROWS, LANES), F32)],
        compiler_params=_params(1),
        name="inproj",
    )(h2d, g, wfox, wgla, wsm, bsm, wa, ba, upper)


def _fox_kernel(seq, q_ref, k_ref, v_ref, c_ref, o_ref, kaug_ref, qaug_ref, vt_ref):
    lane = lax.broadcasted_iota(jnp.int32, (1, HEAD_PAIR), 1)
    row = lax.broadcasted_iota(jnp.int32, (HEAD_PAIR, 1), 0)
    n_heads = 2 * FOX_PAIRS
    for pair in range(FOX_PAIRS):
        pair_lanes = slice(pair * HEAD_PAIR, (pair + 1) * HEAD_PAIR)
        k_all = k_ref[0, :, pair_lanes]
        q_t = q_ref[0, :, pair_lanes].astype(F32).T
        v_t = v_ref[0, :, pair_lanes].astype(F32).T
        for j in range(2):
            h = 2 * pair + j
            lo_lane, hi_lane = j * FOX_HEAD_DIM, (j + 1) * FOX_HEAD_DIM
            bias0 = (1 - j) * FOX_HEAD_DIM
            parts = [p.astype(F32) for p in _split3(-c_ref[0, pair, j:j + 1, :])]
            bias_t = jnp.zeros((HEAD_PAIR, seq), F32)
            for i, part in enumerate(parts):
                bias_t = jnp.where(row == bias0 + i, part, bias_t)
            own_lane = (lane >= lo_lane) & (lane < hi_lane)
            kaug_ref[h] = jnp.where(own_lane, k_all, bias_t.T.astype(BF16))
            own_row = (row >= lo_lane) & (row < hi_lane)
            one_row = (row >= bias0) & (row < bias0 + len(parts))
            qaug_ref[h] = jnp.where(own_row, q_t, jnp.where(one_row, 1.0, 0.0)).astype(BF16)
            vt_ref[h] = jnp.concatenate(
                [v_t[lo_lane:hi_lane], jnp.ones((FOX_ONES_ROWS, seq), F32)], axis=0).astype(BF16)

    chains = []
    for qi in reversed(range(seq // TQ)):
        q0 = qi * TQ
        for h in range(n_heads):
            steps, k0 = [], 0
            while k0 + FOX_WIDE <= q0:
                steps.append((k0, FOX_WIDE, False))
                k0 += FOX_WIDE
            if k0 < q0:
                steps.append((k0, q0 - k0, False))
            steps.append((q0, TQ, True))
            chains.append((q0, h, steps))
    items = []
    for g in range(0, len(chains), FOX_GROUP):
        group = chains[g:g + FOX_GROUP]
        for t in range(max(len(c[2]) for c in group)):
            for q0, h, steps in group:
                if t < len(steps):
                    items.append((q0, h) + steps[t])

    def scores(item):
        q0, h, k0, width, _ = item
        return _dot(kaug_ref[h, k0:k0 + width, :], qaug_ref[h, :, q0:q0 + TQ])

    state, out_t = {}, {}
    pending = [scores(item) for item in items[:FOX_AHEAD]]
    for t, item in enumerate(items):
        q0, h, k0, width, diagonal = item
        s_t = pending.pop(0)
        if t + FOX_AHEAD < len(items):
            pending.append(scores(items[t + FOX_AHEAD]))
        m, acc = state.get((q0, h), (jnp.full((1, TQ), -jnp.inf, F32),
                                     jnp.zeros((FOX_HEAD_DIM + FOX_ONES_ROWS, TQ), F32)))
        if diagonal:
            key = lax.broadcasted_iota(jnp.int32, (width, TQ), 0) + k0
            qry = lax.broadcasted_iota(jnp.int32, (width, TQ), 1) + q0
            s_t = jnp.where(key <= qry, s_t, -jnp.inf)
        m_next = jnp.maximum(m, jnp.max(s_t, axis=0, keepdims=True))
        p_t = jnp.exp2(s_t - m_next)
        acc = acc * jnp.exp2(m - m_next) + _dot(vt_ref[h, :, k0:k0 + width], p_t.astype(BF16))
        state[(q0, h)] = (m_next, acc)
        if diagonal:
            denom = acc[FOX_HEAD_DIM:FOX_HEAD_DIM + 8]
            out_t[(q0, h)] = acc[:FOX_HEAD_DIM] / jnp.concatenate(
                [denom] * (FOX_HEAD_DIM // 8), axis=0)
            if (q0, h ^ 1) in out_t:
                pair = h // 2
                both = jnp.concatenate([out_t[(q0, 2 * pair)], out_t[(q0, 2 * pair + 1)]], axis=0)
                o_ref[0, q0:q0 + TQ, pair * HEAD_PAIR:(pair + 1) * HEAD_PAIR] = both.T.astype(BF16)


def _gla_kernel(seq, q_ref, k_ref, v_ref, la_ref, ltri_ref, o_ref):
    n_chunks = TT // GLA_CHUNK
    lane = lax.broadcasted_iota(jnp.int32, (1, HEAD_PAIR), 1)
    head_mask = (lane < GLA_DK, lane >= GLA_DK)
    row_chunk = lax.broadcasted_iota(jnp.int32, (TT, 1), 0) // GLA_CHUNK
    rr = lax.broadcasted_iota(jnp.int32, (TT, TT), 0)
    cc = lax.broadcasted_iota(jnp.int32, (TT, TT), 1)
    intra = (rr >= cc) & (rr // GLA_CHUNK == cc // GLA_CHUNK)
    ltri = ltri_ref[...]

    n_pairs = GLA_HEADS // 2
    n_tiles = seq // TT

    def rows_of(t):
        return slice(t * TT, (t + 1) * TT)

    def cumulate(t):
        hi, lo = _split2(la_ref[rows_of(t), :])
        return _dot(ltri, hi) + _dot(ltri, lo)

    def products(t, b_all):
        rows = rows_of(t)
        out = []
        for p in range(n_pairs):
            ksl = slice(p * HEAD_PAIR, (p + 1) * HEAD_PAIR)
            b = b_all[:, ksl]
            last_rows = [b[(i + 1) * GLA_CHUNK - 1:(i + 1) * GLA_CHUNK, :] for i in range(n_chunks)]
            b_last = jnp.concatenate(
                [jnp.broadcast_to(r, (GLA_CHUNK, HEAD_PAIR)) for r in last_rows], axis=0)
            q_in = (q_ref[rows, ksl] * jnp.exp(b)).astype(BF16)
            k = k_ref[rows, ksl]
            k_in = (k * jnp.exp(-b)).astype(BF16)
            k_st = (k * jnp.exp(b_last - b)).astype(BF16)
            kv = None
            attn = []
            for j in range(2):
                h = 2 * p + j
                vj = v_ref[rows, h * GLA_DV:(h + 1) * GLA_DV]
                kcat = jnp.concatenate(
                    [jnp.where((row_chunk == i) & head_mask[j], k_st, jnp.zeros_like(k_st))
                     for i in range(n_chunks)], axis=1)
                contrib = _dot_tn(vj, kcat)
                kv = contrib if kv is None else kv + contrib
                qh = jnp.where(head_mask[j], q_in, jnp.zeros_like(q_in))
                attn.append(_dot_nt(qh, k_in))
            out.append((q_in, [jnp.exp(r) for r in last_rows], kv, attn))
        return out

    state = [jnp.zeros((GLA_DV, HEAD_PAIR), F32) for _ in range(n_pairs)]

    def finish(t, prods):
        rows = rows_of(t)
        for p in range(n_pairs):
            q_in, decays, kv, attn = prods[p]
            o_inter = []
            for i in range(n_chunks):
                st = state[p].astype(BF16)
                st2 = jnp.concatenate(
                    [jnp.where(head_mask[j], st, jnp.zeros_like(st)) for j in range(2)], axis=0)
                o_inter.append(_dot_nt(q_in[i * GLA_CHUNK:(i + 1) * GLA_CHUNK, :], st2))
                state[p] = state[p] * decays[i] + kv[:, i * HEAD_PAIR:(i + 1) * HEAD_PAIR]
            o_intra = []
            for j in range(2):
                h = 2 * p + j
                vj = v_ref[rows, h * GLA_DV:(h + 1) * GLA_DV]
                o_intra.append(_dot(jnp.where(intra, attn[j], 0.0).astype(BF16), vj))
            o_ref[rows, 2 * p * GLA_DV:(2 * p + 2) * GLA_DV] = (
                jnp.concatenate(o_intra, axis=1) + jnp.concatenate(o_inter, axis=0))

    b_alls, prods = {}, {}
    for step in range(n_tiles + 2):
        if step < n_tiles:
            b_alls[step] = cumulate(step)
        if 0 <= step - 1 < n_tiles:
            prods[step - 1] = products(step - 1, b_alls.pop(step - 1))
        if 0 <= step - 2 < n_tiles:
            finish(step - 2, prods.pop(step - 2))


def _mixers_kernel(seq, fq_ref, fk_ref, fv_ref, c_ref, gq_ref, gk_ref, gv_ref, la_ref, ltri_ref,
                   ofox_ref, ogla_ref, kaug_ref, qaug_ref, vt_ref):
    _fox_kernel(seq, fq_ref, fk_ref, fv_ref, c_ref, ofox_ref, kaug_ref, qaug_ref, vt_ref)
    _gla_kernel(seq, gq_ref, gk_ref, gv_ref, la_ref, ltri_ref, ogla_ref)


def _mixers(fq, fk, fv, c, gq, gk, gv, la, ltri):
    batch, seq, _ = fq.shape
    assert FOX_PAIRS * HEAD_PAIR == FOX_WIDTH
    n_heads = 2 * FOX_PAIRS
    qkv = pl.BlockSpec((1, seq, FOX_WIDTH), lambda b: (b, 0, 0))
    row = lambda w: pl.BlockSpec((seq, w), lambda b: (b, 0))
    return pl.pallas_call(
        functools.partial(_mixers_kernel, seq),
        grid=(batch,),
        in_specs=[qkv, qkv, qkv, pl.BlockSpec((1, FOX_PAIRS, 2, seq), lambda b: (b, 0, 0, 0)),
                  row(GLA_KEY_WIDTH), row(GLA_KEY_WIDTH), row(GLA_VAL_WIDTH), row(GLA_KEY_WIDTH),
                  _const_spec((TT, TT))],
        out_specs=[qkv, row(GLA_VAL_WIDTH)],
        out_shape=[jax.ShapeDtypeStruct((batch, seq, FOX_WIDTH), BF16),
                   jax.ShapeDtypeStruct((batch * seq, GLA_VAL_WIDTH), F32)],
        scratch_shapes=[pltpu.VMEM((n_heads, seq, HEAD_PAIR), BF16),
                        pltpu.VMEM((n_heads, HEAD_PAIR, seq), BF16),
                        pltpu.VMEM((n_heads, FOX_HEAD_DIM + FOX_ONES_ROWS, seq), BF16)],
        compiler_params=_params(1),
        name="mixers",
    )(fq, fk, fv, c, gq, gk, gv, la, ltri)


def _outproj_kernel(h_ref, ofox_ref, ogla_ref, gpre_ref, wgate_ref, gnorm_ref, wbf_ref, wbg_ref,
                    wout_ref, gpost_ref, o_ref):
    subs = [slice(s * OUT_SUB, (s + 1) * OUT_SUB) for s in range(OUT_TM // OUT_SUB)]
    u = [_rms(h_ref[rows, :], gpre_ref[...]).astype(BF16) for rows in subs]
    gates = [_dot_nt(us, wgate_ref[...]) for us in u]
    branch_fox = [_dot(ofox_ref[rows, :], wbf_ref[...]) for rows in subs]
    branch_gla = []
    for s, rows in enumerate(subs):
        g_r = gates[s][:, :GLA_VAL_WIDTH]
        heads = []
        for hd in range(GLA_HEADS):
            sl = slice(hd * GLA_DV, (hd + 1) * GLA_DV)
            heads.append(_rms(ogla_ref[rows, sl], gnorm_ref[:, sl]))
        o_gla = jnp.concatenate(heads, axis=1)
        o_gla = (o_gla * (g_r * jax.nn.sigmoid(g_r))).astype(BF16)
        branch_gla.append(_dot(o_gla, wbg_ref[...]))
    merged = []
    for s in range(len(subs)):
        gate_fox = gates[s][:, GLA_VAL_WIDTH:GLA_VAL_WIDTH + D_MODEL]
        gate_gla = gates[s][:, GLA_VAL_WIDTH + D_MODEL:]
        y = jax.nn.sigmoid(gate_fox) * branch_fox[s] + jax.nn.sigmoid(gate_gla) * branch_gla[s]
        merged.append(_dot(y.astype(BF16), wout_ref[...]))
    for s, rows in enumerate(subs):
        o_ref[rows, :] = h_ref[rows, :] + _rms(merged[s], gpost_ref[...])


def _outproj(h2d, o_fox, o_gla, g_pre, wgate, gnorm, wbf, wbg, wout, g_post):
    t = h2d.shape[0]
    row = lambda w: pl.BlockSpec((OUT_TM, w), lambda i: (i, 0))
    return pl.pallas_call(
        _outproj_kernel,
        grid=(t // OUT_TM,),
        in_specs=[row(D_MODEL), row(FOX_WIDTH), row(GLA_VAL_WIDTH), _const_spec((1, D_MODEL)),
                  _const_spec((GLA_VAL_WIDTH + 2 * D_MODEL, D_MODEL)),
                  _const_spec((1, GLA_VAL_WIDTH)),
                  _const_spec((FOX_WIDTH, D_MODEL)), _const_spec((GLA_VAL_WIDTH, D_MODEL)),
                  _const_spec((D_MODEL, D_MODEL)), _const_spec((1, D_MODEL))],
        out_specs=row(D_MODEL),
        out_shape=jax.ShapeDtypeStruct((t, D_MODEL), F32),
        compiler_params=_params(1),
        name="outproj",
    )(h2d, o_fox, o_gla, g_pre, wgate, gnorm, wbf, wbg, wout, g_post)


def _chunk_tril(n):
    r = jnp.arange(n)[:, None]
    c = jnp.arange(n)[None, :]
    return (((r // GLA_CHUNK) == (c // GLA_CHUNK)) & (r >= c)).astype(BF16)


def _layer(h2d, batch, seq, ffn1_pre_g, ffn1_w_gate, ffn1_w_up, ffn1_w_down, ffn1_post_g,
           mix_pre_g, w_in, b_forget, w_alpha_up, b_alpha, gla_norm_g, w_branch_fox,
           w_branch_gla, w_out, mix_post_g, ffn2_pre_g, ffn2_w_gate, ffn2_w_up, ffn2_w_down,
           ffn2_post_g):
    vec = lambda g: g.reshape(1, -1)
    bf = lambda w: w.astype(BF16)

    splits = [FOX_WIDTH, FOX_WIDTH, FOX_WIDTH, FOX_HEADS, GLA_KEY_WIDTH, GLA_KEY_WIDTH,
              GLA_VAL_WIDTH, GLA_GATE_RANK, GLA_VAL_WIDTH, D_MODEL, D_MODEL]
    offs = [0]
    for s in splits:
        offs.append(offs[-1] + s)
    w_in_t = w_in.T
    col = lambda a, b: w_in_t[offs[a]:offs[b], :]
    wfox = bf(col(0, 3))
    wgla = bf(col(4, 7))
    pad_cols = SMALL_W - FOX_HEADS - GLA_GATE_RANK
    wsm = bf(jnp.concatenate([col(3, 4), col(7, 8), jnp.zeros((pad_cols, D_MODEL), F32)], axis=0))
    bsm = jnp.concatenate([b_forget, jnp.zeros((SMALL_W - FOX_HEADS,), F32)]).reshape(1, SMALL_W)
    wa = bf(jnp.concatenate([jnp.zeros((FOX_HEADS, GLA_KEY_WIDTH), F32), w_alpha_up,
                             jnp.zeros((pad_cols, GLA_KEY_WIDTH), F32)], axis=0))
    wgate = bf(col(8, 11))
    tok = jnp.arange(LANES)
    upper = (tok[:, None] <= tok[None, :]).astype(BF16)
    ltri = _chunk_tril(TT)

    h1, ffn2_w = _ffn(h2d, vec(ffn1_pre_g), bf(ffn1_w_gate), bf(ffn1_w_up), bf(ffn1_w_down),
                      vec(ffn1_post_g), cast=(ffn2_w_gate, ffn2_w_up, ffn2_w_down))

    fq, fk, fv, c, gq, gk, gv, la = _inproj(h1, seq, vec(mix_pre_g), wfox, wgla, wsm, bsm, wa,
                                            vec(b_alpha), upper)
    shape3 = (batch, seq, FOX_WIDTH)
    o_fox, o_gla = _mixers(fq.reshape(shape3), fk.reshape(shape3), fv.reshape(shape3),
                           c.reshape(batch, FOX_HEADS // 2, 2, seq), gq, gk, gv, la, ltri)
    o_fox = o_fox.reshape(batch * seq, FOX_WIDTH)

    h2 = _outproj(h1, o_fox, o_gla, vec(mix_pre_g), wgate, vec(gla_norm_g), bf(w_branch_fox),
                  bf(w_branch_gla), bf(w_out), vec(mix_post_g))

    return _ffn(h2, vec(ffn2_pre_g), *ffn2_w, vec(ffn2_post_g))[0]


def kernel(x, ffn1_pre_g, ffn1_w_gate, ffn1_w_up, ffn1_w_down, ffn1_post_g, mix_pre_g, w_in,
           b_forget, w_alpha_up, b_alpha, gla_norm_g, w_branch_fox, w_branch_gla, w_out,
           mix_post_g, ffn2_pre_g, ffn2_w_gate, ffn2_w_up, ffn2_w_down, ffn2_post_g):
    batch, seq, d = x.shape
    h = x.reshape(batch * seq, d)
    depth = ffn1_pre_g.shape[0]
    for l in range(depth):
        h = _layer(h, batch, seq, ffn1_pre_g[l], ffn1_w_gate[l], ffn1_w_up[l], ffn1_w_down[l],
                   ffn1_post_g[l], mix_pre_g[l], w_in[l], b_forget[l], w_alpha_up[l], b_alpha[l],
                   gla_norm_g[l], w_branch_fox[l], w_branch_gla[l], w_out[l], mix_post_g[l],
                   ffn2_pre_g[l], ffn2_w_gate[l], ffn2_w_up[l], ffn2_w_down[l], ffn2_post_g[l])
    return h.reshape(batch, seq, d)
```
